```python
import jax, jax.numpy as jnp
from jax import lax
import numpy as np

D_MODEL = 1024
BATCH = 8
SEQ = 2048
DEPTH = 2
DEC_BATCH = 128
DEC_SEQ = 1
PAST_LEN = 16384
PAGE_SIZE = 128

N_MIXERS = 4
D_MIX = D_MODEL
D_GROUP = D_MIX // N_MIXERS
D_IN_PROJ = 8 * D_GROUP
GMLP_HEADS = 4
GMLP_HEAD_DIM = D_GROUP // GMLP_HEADS
CHUNK = 128
CONF_WIDTH = 31
SC_WIDTH = 3
POOL_WINDOWS = (2, 4, 8, 16)
POOL_GROUPS = len(POOL_WINDOWS)
POOL_GROUP_DIM = D_GROUP // POOL_GROUPS
POOL_BUF = max(POOL_WINDOWS) - 1
MEM_LEN = 256
XATTN_HEADS = 4
XATTN_HEAD_DIM = D_MODEL // XATTN_HEADS
D_FF = 4 * D_MODEL
EPS = 1e-6

kernel_name = 'hybrid_headgroup_decoder_step'


def rms_norm(x, g):
    xf = x.astype(jnp.float32)
    y = xf * lax.rsqrt(jnp.mean(xf * xf, axis=-1, keepdims=True) + EPS)
    return (y * g.astype(jnp.float32)).astype(x.dtype)


def layer_norm(x, g, b):
    xf = x.astype(jnp.float32)
    xc = xf - jnp.mean(xf, axis=-1, keepdims=True)
    y = xc * lax.rsqrt(jnp.mean(xc * xc, axis=-1, keepdims=True) + EPS)
    return (y * g.astype(jnp.float32) + b.astype(jnp.float32)).astype(x.dtype)


def causal_depthwise_conv(buf, x, w):
    xx = jnp.concatenate([buf.astype(x.dtype), x], axis=1)
    out = lax.conv_general_dilated(xx, w[:, None, :].astype(x.dtype), window_strides=(1,), padding='VALID',
                                   dimension_numbers=('NWC', 'WIO', 'NWC'), feature_group_count=x.shape[-1])
    return out, xx[:, -(w.shape[0] - 1):]


def chunk_spatial_gate(v, ws, bs):
    bsz, t, c = v.shape
    n_chunks = -(-t // CHUNK)
    vp = jnp.pad(v, ((0, 0), (0, n_chunks * CHUNK - t), (0, 0)))
    vp = vp.reshape(bsz, n_chunks, CHUNK, GMLP_HEADS, GMLP_HEAD_DIM)
    mask = jnp.tril(jnp.ones((CHUNK, CHUNK), dtype=bool))
    wsm = jnp.where(mask[None], ws, jnp.zeros_like(ws)).astype(v.dtype)
    z = jnp.einsum('hts,bcshd->bcthd', wsm, vp) + bs.T.astype(v.dtype)[None, None, :, :, None]
    return z.reshape(bsz, n_chunks * CHUNK, c)[:, :t]


def multiscale_pool(buf, x, pos0, w_pool, scale):
    bsz, t, c = x.shape
    xx = jnp.concatenate([buf.astype(x.dtype), x], axis=1)
    cs = jnp.cumsum(xx.astype(jnp.float32), axis=1)
    cs = jnp.concatenate([jnp.zeros((bsz, 1, c), jnp.float32), cs], axis=1)
    ends = cs[:, POOL_BUF + 1:POOL_BUF + 1 + t]
    pos = pos0 + jnp.arange(t, dtype=jnp.int32)
    outs = []
    for g, w in enumerate(POOL_WINDOWS):
        lo, hi = g * POOL_GROUP_DIM, (g + 1) * POOL_GROUP_DIM
        starts = cs[:, POOL_BUF + 1 - w:POOL_BUF + 1 - w + t, lo:hi]
        cnt = jnp.minimum(w, pos + 1).astype(jnp.float32)[None, :, None]
        outs.append((ends[..., lo:hi] - starts) / cnt)
    pooled = (jnp.concatenate(outs, axis=-1) - x.astype(jnp.float32)).astype(x.dtype)
    pooled = pooled.reshape(bsz, t, POOL_GROUPS, POOL_GROUP_DIM)
    y = jnp.einsum('btgc,gcd->btgd', pooled, w_pool).reshape(bsz, t, c) * scale
    return y, xx[:, -POOL_BUF:]


def token_mixers(h, buf_glu, buf_short, buf_pool, pos0, w_in, gmlp_ln_g, gmlp_ln_b, gmlp_ws, gmlp_bs,
                 conf_dw, conf_dw_b, conf_ln_g, conf_ln_b, sc_dw, pool_w, pool_scale, mix_out_g, w_out):
    bsz, t, _ = h.shape
    z = h @ w_in
    u, v, glu_a, glu_g, sc_b, sc_c, sc_x, pool_x = jnp.split(z, 8, axis=-1)
    vn = layer_norm(v, gmlp_ln_g, gmlp_ln_b)
    y_a = u * chunk_spatial_gate(vn, gmlp_ws, gmlp_bs)
    glu = glu_a * jax.nn.sigmoid(glu_g)
    conv_b, new_glu = causal_depthwise_conv(buf_glu, glu, conf_dw)
    y_b = jax.nn.silu(layer_norm(conv_b + conf_dw_b, conf_ln_g, conf_ln_b))
    conv_c, new_short = causal_depthwise_conv(buf_short, sc_c * sc_x, sc_dw)
    y_c = sc_b * conv_c
    y_d, new_pool = multiscale_pool(buf_pool, pool_x, pos0, pool_w, pool_scale)
    y = jnp.stack([y_a, y_b, y_c, y_d], axis=2)
    y = rms_norm(y, mix_out_g.reshape(N_MIXERS, D_GROUP)).reshape(bsz, t, D_MIX)
    return y @ w_out, vn, new_glu, new_short, new_pool


def mem_kv(mem, g_mem, w_k, w_v):
    bsz = mem.shape[0]
    m = rms_norm(mem, g_mem)
    k = (m @ w_k).reshape(bsz, MEM_LEN, XATTN_HEADS, XATTN_HEAD_DIM)
    v = (m @ w_v).reshape(bsz, MEM_LEN, XATTN_HEADS, XATTN_HEAD_DIM)
    return k, v


def cross_attend(h, k, v, w_q, w_o):
    bsz, t, _ = h.shape
    q = (h @ w_q).reshape(bsz, t, XATTN_HEADS, XATTN_HEAD_DIM)
    s = jnp.einsum('bthd,bmhd->bhtm', q, k.astype(h.dtype)).astype(jnp.float32) * (XATTN_HEAD_DIM ** -0.5)
    p = jax.nn.softmax(s, axis=-1).astype(h.dtype)
    o = jnp.einsum('bhtm,bmhd->bthd', p, v.astype(h.dtype)).reshape(bsz, t, XATTN_HEADS * XATTN_HEAD_DIM)
    return o @ w_o


def sq_relu_mlp(h, w1, w2):
    a = jax.nn.relu(h @ w1)
    return (a * a) @ w2


def nrm(k, shape, scale):
    return jax.random.normal(k, shape, jnp.float32) * scale


def gain(k, shape):
    return 1.0 + 0.02 * jax.random.normal(k, shape, jnp.float32)


def setup_inputs(seed: int = 0) -> dict:
    key = jax.random.key(seed)
    ks = jax.random.split(key, 33)
    hd = XATTN_HEADS * XATTN_HEAD_DIM
    return {
        'x_prompt': nrm(ks[0], (BATCH, SEQ, D_MODEL), 1.0),
        'x_sample': nrm(ks[1], (DEC_BATCH, DEC_SEQ, D_MODEL), 1.0),
        'mem_prompt': nrm(ks[2], (BATCH, MEM_LEN, D_MODEL), 1.0),
        'cache_mem_k': nrm(ks[3], (DEPTH, DEC_BATCH, MEM_LEN, XATTN_HEADS, XATTN_HEAD_DIM), 1.0),
        'cache_mem_v': nrm(ks[4], (DEPTH, DEC_BATCH, MEM_LEN, XATTN_HEADS, XATTN_HEAD_DIM), 1.0),
        'state_conv_glu': nrm(ks[5], (DEPTH, DEC_BATCH, CONF_WIDTH - 1, D_GROUP), 0.5),
        'state_conv_short': nrm(ks[6], (DEPTH, DEC_BATCH, SC_WIDTH - 1, D_GROUP), 0.5),
        'state_pool': nrm(ks[7], (DEPTH, DEC_BATCH, POOL_BUF, D_GROUP), 1.0),
        'norm_mix': gain(ks[8], (DEPTH, D_MODEL)),
        'w_in': nrm(ks[9], (DEPTH, D_MODEL, D_IN_PROJ), D_MODEL ** -0.5),
        'gmlp_ln_g': gain(ks[10], (DEPTH, D_GROUP)),
        'gmlp_ln_b': nrm(ks[11], (DEPTH, D_GROUP), 0.02),
        'gmlp_ws': nrm(ks[12], (DEPTH, GMLP_HEADS, CHUNK, CHUNK), 0.5 * CHUNK ** -0.5),
        'gmlp_bs': gain(ks[13], (DEPTH, GMLP_HEADS, CHUNK)),
        'conf_dw': nrm(ks[14], (DEPTH, CONF_WIDTH, D_GROUP), CONF_WIDTH ** -0.5),
        'conf_dw_b': nrm(ks[15], (DEPTH, D_GROUP), 0.02),
        'conf_ln_g': gain(ks[16], (DEPTH, D_GROUP)),
        'conf_ln_b': nrm(ks[17], (DEPTH, D_GROUP), 0.02),
        'sc_dw': nrm(ks[18], (DEPTH, SC_WIDTH, D_GROUP), SC_WIDTH ** -0.5),
        'pool_w': nrm(ks[19], (DEPTH, POOL_GROUPS, POOL_GROUP_DIM, POOL_GROUP_DIM), POOL_GROUP_DIM ** -0.5),
        'pool_scale': gain(ks[20], (DEPTH, D_GROUP)),
        'mix_out_g': gain(ks[21], (DEPTH, D_MIX)),
        'w_out': nrm(ks[22], (DEPTH, D_MIX, D_MODEL), D_MIX ** -0.5),
        'norm_xattn': gain(ks[23], (DEPTH, D_MODEL)),
        'norm_mem': gain(ks[24], (DEPTH, D_MODEL)),
        'w_xq': nrm(ks[25], (DEPTH, D_MODEL, hd), D_MODEL ** -0.5),
        'w_xk': nrm(ks[26], (DEPTH, D_MODEL, hd), D_MODEL ** -0.5),
        'w_xv': nrm(ks[27], (DEPTH, D_MODEL, hd), D_MODEL ** -0.5),
        'w_xo': nrm(ks[28], (DEPTH, hd, D_MODEL), hd ** -0.5),
        'norm_ffn': gain(ks[29], (DEPTH, D_MODEL)),
        'w_ff1': nrm(ks[30], (DEPTH, D_MODEL, D_FF), D_MODEL ** -0.5),
        'w_ff2': nrm(ks[31], (DEPTH, D_FF, D_MODEL), D_FF ** -0.5),
        'norm_final': gain(ks[32], (D_MODEL,)),
    }


def reference(x_prompt, x_sample, mem_prompt, cache_mem_k, cache_mem_v, state_conv_glu, state_conv_short,
              state_pool, norm_mix, w_in, gmlp_ln_g, gmlp_ln_b, gmlp_ws, gmlp_bs, conf_dw, conf_dw_b,
              conf_ln_g, conf_ln_b, sc_dw, pool_w, pool_scale, mix_out_g, w_out, norm_xattn, norm_mem,
              w_xq, w_xk, w_xv, w_xo, norm_ffn, w_ff1, w_ff2, norm_final):
    xp, xs = x_prompt, x_sample
    bp = xp.shape[0]
    mk_p, mv_p, glu_p, glu_s, sh_p, sh_s, pl_p, pl_s, v_s = [], [], [], [], [], [], [], [], []
    for l in range(DEPTH):
        mix_l = (w_in[l], gmlp_ln_g[l], gmlp_ln_b[l], gmlp_ws[l], gmlp_bs[l], conf_dw[l], conf_dw_b[l],
                 conf_ln_g[l], conf_ln_b[l], sc_dw[l], pool_w[l], pool_scale[l], mix_out_g[l], w_out[l])
        zb_glu = jnp.zeros((bp, CONF_WIDTH - 1, D_GROUP), xp.dtype)
        zb_sh = jnp.zeros((bp, SC_WIDTH - 1, D_GROUP), xp.dtype)
        zb_pl = jnp.zeros((bp, POOL_BUF, D_GROUP), xp.dtype)
        m_out, _, ng, nsh, npl = token_mixers(rms_norm(xp, norm_mix[l]), zb_glu, zb_sh, zb_pl, 0, *mix_l)
        xp = xp + m_out
        k_p, v_p = mem_kv(mem_prompt, norm_mem[l], w_xk[l], w_xv[l])
        xp = xp + cross_attend(rms_norm(xp, norm_xattn[l]), k_p, v_p, w_xq[l], w_xo[l])
        xp = xp + sq_relu_mlp(rms_norm(xp, norm_ffn[l]), w_ff1[l], w_ff2[l])
        mk_p.append(k_p); mv_p.append(v_p); glu_p.append(ng); sh_p.append(nsh); pl_p.append(npl)
        m_out, vn_s, ng, nsh, npl = token_mixers(rms_norm(xs, norm_mix[l]), state_conv_glu[l],
                                                 state_conv_short[l], state_pool[l], PAST_LEN, *mix_l)
        xs = xs + m_out
        xs = xs + cross_attend(rms_norm(xs, norm_xattn[l]), cache_mem_k[l], cache_mem_v[l], w_xq[l], w_xo[l])
        xs = xs + sq_relu_mlp(rms_norm(xs, norm_ffn[l]), w_ff1[l], w_ff2[l])
        glu_s.append(ng); sh_s.append(nsh); pl_s.append(npl); v_s.append(vn_s)
    y_prompt = rms_norm(xp, norm_final)
    y_sample = rms_norm(xs, norm_final)
    new_mem_k_prompt = jnp.stack(mk_p, axis=0)
    new_mem_v_prompt = jnp.stack(mv_p, axis=0)
    new_conv_glu_prompt = jnp.stack(glu_p, axis=0)
    new_conv_glu_sample = jnp.stack(glu_s, axis=0)
    new_conv_short_prompt = jnp.stack(sh_p, axis=0)
    new_conv_short_sample = jnp.stack(sh_s, axis=0)
    new_pool_prompt = jnp.stack(pl_p, axis=0)
    new_pool_sample = jnp.stack(pl_s, axis=0)
    new_gmlp_v_sample = jnp.stack(v_s, axis=0)
    return (y_prompt, y_sample, new_mem_k_prompt, new_mem_v_prompt, new_conv_glu_prompt, new_conv_glu_sample,
            new_conv_short_prompt, new_conv_short_sample, new_pool_prompt, new_pool_sample, new_gmlp_v_sample)
```

```python
import functools

import jax
import jax.numpy as jnp
from jax import lax
from jax.experimental import pallas as pl
from jax.experimental.pallas import tpu as pltpu

F32 = jnp.float32
BF16 = jnp.bfloat16

D_MODEL = 1024
D_GROUP = 256
N_GROUPS = 4
D_IN_PROJ = 8 * D_GROUP
GMLP_HEADS = 4
GMLP_HEAD_DIM = D_GROUP // GMLP_HEADS
CHUNK = 128
CONF_WIDTH = 31
SC_WIDTH = 3
POOL_WINDOWS = (2, 4, 8, 16)
POOL_GROUP_DIM = D_GROUP // len(POOL_WINDOWS)
POOL_BUF = max(POOL_WINDOWS) - 1
MEM_LEN = 256
XATTN_HEADS = 4
XATTN_HEAD_DIM = D_MODEL // XATTN_HEADS
D_FF = 4 * D_MODEL
PAST_LEN = 16384
EPS = 1e-6

GLU_HALO = 32
SC_HALO = 8
POOL_HALO = 32
CONV_ROWS = 32
FF_CHUNK = 1024
VMEM_LIMIT = 56 * 1024 * 1024


def _rms(x, g):
    return x * lax.rsqrt(jnp.mean(x * x, axis=-1, keepdims=True) + EPS) * g


def _layer_norm(x, g, b):
    xc = x - jnp.mean(x, axis=-1, keepdims=True)
    return xc * lax.rsqrt(jnp.mean(xc * xc, axis=-1, keepdims=True) + EPS) * g + b


def _sigmoid(x):
    return 1.0 / (1.0 + jnp.exp(-x))


def _dot(a, b):
    return jnp.dot(a, b, preferred_element_type=F32)


def _const_spec(shape):
    zeros = (0,) * len(shape)
    return pl.BlockSpec(shape, lambda *_: zeros, pipeline_mode=pl.Buffered(1))


def _pool_window_lanes(shape):
    grp = lax.broadcasted_iota(jnp.int32, shape, len(shape) - 1) // POOL_GROUP_DIM
    win = jnp.full(shape, float(POOL_WINDOWS[-1]), F32)
    for g in range(len(POOL_WINDOWS) - 2, -1, -1):
        win = jnp.where(grp == g, float(POOL_WINDOWS[g]), win)
    return grp, win


def _select_by_group(grp, vals):
    out = vals[-1]
    for g in range(len(vals) - 2, -1, -1):
        out = jnp.where(grp == g, vals[g], out)
    return out


def _mix_out(x, y_groups, mog_ref, wout_ref):
    yn = [_rms(y, mog_ref[:, g * D_GROUP:(g + 1) * D_GROUP]).astype(BF16) for g, y in enumerate(y_groups)]
    return x + _dot(jnp.concatenate(yn, axis=1), wout_ref[...])


def _mixer_prompt_kernel(x_ref, g_ref, win_ref, lng_ref, lnb_ref, wcat_ref, bst_ref, cw_ref, cb_ref, clg_ref,
                         clb_ref, sw_ref, pw_ref, ps_ref, mog_ref, wout_ref,
                         o_ref, oglu_ref, osh_ref, opool_ref,
                         glu_buf, sc_buf, p_buf, s2_buf, s4_buf, s8_buf, *, tt, n_t):
    t = pl.program_id(1)

    @pl.when(t == 0)
    def _():
        glu_buf[0:GLU_HALO, :] = jnp.zeros((GLU_HALO, D_GROUP), F32)
        sc_buf[0:SC_HALO, :] = jnp.zeros((SC_HALO, D_GROUP), F32)
        p_buf[0:POOL_HALO, :] = jnp.zeros((POOL_HALO, D_GROUP), F32)

    x = x_ref[0]
    h = _rms(x, g_ref[...]).astype(BF16)
    z = _dot(h, win_ref[...])
    u, v, glu_a, glu_g, sc_b, sc_c, sc_x, pool_x = [z[:, i * D_GROUP:(i + 1) * D_GROUP] for i in range(8)]

    vn = _layer_norm(v, lng_ref[...], lnb_ref[...])
    head = lax.broadcasted_iota(jnp.int32, (CHUNK, D_GROUP), 1) // GMLP_HEAD_DIM
    gates = []
    for c in range(tt // CHUNK):
        vc = vn[c * CHUNK:(c + 1) * CHUNK]
        stack = jnp.concatenate([jnp.where(head == hh, vc, 0.0) for hh in range(GMLP_HEADS)], axis=0)
        gates.append(_dot(wcat_ref[...], stack.astype(BF16)) + bst_ref[...])
    y_a = u * jnp.concatenate(gates, axis=0)

    glu_buf[GLU_HALO:GLU_HALO + tt, :] = glu_a * _sigmoid(glu_g)
    first = GLU_HALO - (CONF_WIDTH - 1)
    yb_blocks = []
    for r0 in range(0, tt, CONV_ROWS):
        acc = cw_ref[0:1, :] * glu_buf[first + r0:first + r0 + CONV_ROWS, :]
        for k in range(1, CONF_WIDTH):
            acc = acc + cw_ref[k:k + 1, :] * glu_buf[first + k + r0:first + k + r0 + CONV_ROWS, :]
        ln = _layer_norm(acc + cb_ref[...], clg_ref[...], clb_ref[...])
        yb_blocks.append(ln * _sigmoid(ln))
    y_b = jnp.concatenate(yb_blocks, axis=0)

    sxc = sc_c * sc_x
    sc_buf[SC_HALO:SC_HALO + tt, :] = sxc
    conv_c = (sw_ref[0:1, :] * sc_buf[SC_HALO - 2:SC_HALO - 2 + tt, :]
              + sw_ref[1:2, :] * sc_buf[SC_HALO - 1:SC_HALO - 1 + tt, :]
              + sw_ref[2:3, :] * sxc)
    y_c = sc_b * conv_c

    p_buf[POOL_HALO:POOL_HALO + tt, :] = pool_x
    end = POOL_HALO + tt
    s2_buf[8:end, :] = p_buf[8:end, :] + p_buf[7:end - 1, :]
    s4_buf[16:end, :] = s2_buf[16:end, :] + s2_buf[14:end - 2, :]
    s8_buf[24:end, :] = s4_buf[24:end, :] + s4_buf[20:end - 4, :]
    s16 = s8_buf[POOL_HALO:end, :] + s8_buf[POOL_HALO - 8:end - 8, :]
    grp, win = _pool_window_lanes((tt, D_GROUP))
    ssum = _select_by_group(grp, [s2_buf[POOL_HALO:end, :], s4_buf[POOL_HALO:end, :],
                                  s8_buf[POOL_HALO:end, :], s16])
    pos = t * tt + lax.broadcasted_iota(jnp.int32, (tt, D_GROUP), 0)
    cnt = jnp.minimum(win, (pos + 1).astype(F32))
    pooled = ssum / cnt - pool_x
    y_d = _dot(pooled.astype(BF16), pw_ref[...]) * ps_ref[...]

    o_ref[0] = _mix_out(x, [y_a, y_b, y_c, y_d], mog_ref, wout_ref)

    @pl.when(t == n_t - 1)
    def _():
        oglu_ref[0] = glu_buf[GLU_HALO + tt - (CONF_WIDTH - 1):GLU_HALO + tt, :]
        osh_ref[0] = sc_buf[SC_HALO + tt - (SC_WIDTH - 1):SC_HALO + tt, :]
        opool_ref[0] = p_buf[POOL_HALO + tt - POOL_BUF:POOL_HALO + tt, :]

    glu_buf[0:GLU_HALO, :] = glu_buf[tt:tt + GLU_HALO, :]
    sc_buf[0:SC_HALO, :] = sc_buf[tt:tt + SC_HALO, :]
    p_buf[0:POOL_HALO, :] = p_buf[tt:tt + POOL_HALO, :]


def _mixer_prompt(x, lw, *, tt=512):
    bsz, seq, _ = x.shape
    n_t = seq // tt
    row = lambda a: a.reshape(1, -1)
    small = [row(lw['norm_mix']), lw['w_in'], row(lw['gmlp_ln_g']), row(lw['gmlp_ln_b']), lw['gmlp_wcat'],
             lw['gmlp_bias_tile'], lw['conf_dw'], row(lw['conf_dw_b']), row(lw['conf_ln_g']), row(lw['conf_ln_b']),
             lw['sc_dw'], lw['pool_w_bd'], row(lw['pool_scale']), row(lw['mix_out_g']), lw['w_out']]
    state_spec = lambda rows: pl.BlockSpec((1, rows, D_GROUP), lambda b, t: (b, 0, 0))
    return pl.pallas_call(
        functools.partial(_mixer_prompt_kernel, tt=tt, n_t=n_t),
        grid=(bsz, n_t),
        in_specs=[pl.BlockSpec((1, tt, D_MODEL), lambda b, t: (b, t, 0))] + [_const_spec(a.shape) for a in small],
        out_specs=[pl.BlockSpec((1, tt, D_MODEL), lambda b, t: (b, t, 0)),
                   state_spec(CONF_WIDTH - 1), state_spec(SC_WIDTH - 1), state_spec(POOL_BUF)],
        out_shape=[jax.ShapeDtypeStruct(x.shape, F32),
                   jax.ShapeDtypeStruct((bsz, CONF_WIDTH - 1, D_GROUP), F32),
                   jax.ShapeDtypeStruct((bsz, SC_WIDTH - 1, D_GROUP), F32),
                   jax.ShapeDtypeStruct((bsz, POOL_BUF, D_GROUP), F32)],
        scratch_shapes=[pltpu.VMEM((GLU_HALO + tt, D_GROUP), F32), pltpu.VMEM((SC_HALO + tt, D_GROUP), F32)]
        + [pltpu.VMEM((POOL_HALO + tt, D_GROUP), F32)] * 4,
        compiler_params=pltpu.CompilerParams(dimension_semantics=("arbitrary", "arbitrary"),
                                             vmem_limit_bytes=VMEM_LIMIT),
        name="mixer_prompt",
    )(x, *small)


def _mixer_sample_kernel(x_ref, stg_ref, sts_ref, stp_ref, g_ref, win_ref, lng_ref, lnb_ref, w00_ref, b0_ref,
                         cw_ref, cb_ref, clg_ref, clb_ref, sw_ref, pw_ref, ps_ref, mog_ref, wout_ref,
                         o_ref, ovn_ref, oglu_ref, osh_ref, opool_ref):
    x = x_ref[...]
    h = _rms(x, g_ref[...]).astype(BF16)
    z = _dot(h, win_ref[...])
    u, v, glu_a, glu_g, sc_b, sc_c, sc_x, pool_x = [z[:, i * D_GROUP:(i + 1) * D_GROUP] for i in range(8)]
    hist = lambda ref, k: ref[:, k * D_GROUP:(k + 1) * D_GROUP]

    vn = _layer_norm(v, lng_ref[...], lnb_ref[...])
    ovn_ref[...] = vn
    y_a = u * (w00_ref[...] * vn + b0_ref[...])

    glu = glu_a * _sigmoid(glu_g)
    n_hist = CONF_WIDTH - 1
    acc = cw_ref[n_hist:n_hist + 1, :] * glu
    for k in range(n_hist):
        acc = acc + cw_ref[k:k + 1, :] * hist(stg_ref, k)
    ln = _layer_norm(acc + cb_ref[...], clg_ref[...], clb_ref[...])
    y_b = ln * _sigmoid(ln)
    oglu_ref[:, 0:(n_hist - 1) * D_GROUP] = stg_ref[:, D_GROUP:]
    oglu_ref[:, (n_hist - 1) * D_GROUP:] = glu

    sxc = sc_c * sc_x
    y_c = sc_b * (sw_ref[0:1, :] * hist(sts_ref, 0) + sw_ref[1:2, :] * hist(sts_ref, 1) + sw_ref[2:3, :] * sxc)
    osh_ref[:, 0:D_GROUP] = hist(sts_ref, 1)
    osh_ref[:, D_GROUP:] = sxc

    run = pool_x
    sums = []
    back = 0
    for w in POOL_WINDOWS:
        while back < w - 1:
            run = run + hist(stp_ref, POOL_BUF - 1 - back)
            back += 1
        sums.append(run)
    grp, win = _pool_window_lanes(pool_x.shape)
    cnt = jnp.minimum(win, float(PAST_LEN + 1))
    pooled = _select_by_group(grp, sums) / cnt - pool_x
    y_d = _dot(pooled.astype(BF16), pw_ref[...]) * ps_ref[...]
    opool_ref[:, 0:(POOL_BUF - 1) * D_GROUP] = stp_ref[:, D_GROUP:]
    opool_ref[:, (POOL_BUF - 1) * D_GROUP:] = pool_x

    o_ref[...] = _mix_out(x, [y_a, y_b, y_c, y_d], mog_ref, wout_ref)


def _mixer_sample(xs, st_glu, st_sh, st_pool, lw):
    n = xs.shape[0]
    row = lambda a: a.reshape(1, -1)
    args = [xs, st_glu.reshape(n, -1), st_sh.reshape(n, -1), st_pool.reshape(n, -1),
            row(lw['norm_mix']), lw['w_in'], row(lw['gmlp_ln_g']), row(lw['gmlp_ln_b']), row(lw['gmlp_w00']),
            row(lw['gmlp_b0']), lw['conf_dw'], row(lw['conf_dw_b']), row(lw['conf_ln_g']), row(lw['conf_ln_b']),
            lw['sc_dw'], lw['pool_w_bd'], row(lw['pool_scale']), row(lw['mix_out_g']), lw['w_out']]
    out_shapes = [(n, D_MODEL), (n, D_GROUP), (n, (CONF_WIDTH - 1) * D_GROUP), (n, (SC_WIDTH - 1) * D_GROUP),
                  (n, POOL_BUF * D_GROUP)]
    xo, vn, nglu, nsh, npool = pl.pallas_call(
        _mixer_sample_kernel,
        grid=(1,),
        in_specs=[_const_spec(a.shape) for a in args],
        out_specs=[pl.BlockSpec(s, lambda i: (0, 0)) for s in out_shapes],
        out_shape=[jax.ShapeDtypeStruct(s, F32) for s in out_shapes],
        compiler_params=pltpu.CompilerParams(dimension_semantics=("arbitrary",), vmem_limit_bytes=VMEM_LIMIT),
        name="mixer_sample",
    )(*args)
    return (xo, vn, nglu.reshape(n, CONF_WIDTH - 1, D_GROUP), nsh.reshape(n, SC_WIDTH - 1, D_GROUP),
            npool.reshape(n, POOL_BUF, D_GROUP))


def _mem_kv_kernel(m_ref, g_ref, wkv_ref, ok_ref, ov_ref, okt_ref, ovb_ref):
    m = _rms(m_ref[0], g_ref[...]).astype(BF16)
    kv = _dot(m, wkv_ref[...])
    k, v = kv[:, :D_MODEL], kv[:, D_MODEL:]
    ok_ref[0] = k
    ov_ref[0] = v
    okt_ref[0] = k.T.astype(BF16)
    ovb_ref[0] = v.astype(BF16)


def _mem_kv(mem, lw):
    bsz = mem.shape[0]
    blk = lambda r, c: pl.BlockSpec((1, r, c), lambda b: (b, 0, 0))
    g = lw['norm_mem'].reshape(1, -1)
    return pl.pallas_call(
        _mem_kv_kernel,
        grid=(bsz,),
        in_specs=[blk(MEM_LEN, D_MODEL), _const_spec(g.shape), _const_spec(lw['w_xkv'].shape)],
        out_specs=[blk(MEM_LEN, D_MODEL), blk(MEM_LEN, D_MODEL), blk(D_MODEL, MEM_LEN), blk(MEM_LEN, D_MODEL)],
        out_shape=[jax.ShapeDtypeStruct((bsz, MEM_LEN, D_MODEL), F32)] * 2
        + [jax.ShapeDtypeStruct((bsz, D_MODEL, MEM_LEN), BF16), jax.ShapeDtypeStruct((bsz, MEM_LEN, D_MODEL), BF16)],
        compiler_params=pltpu.CompilerParams(dimension_semantics=("arbitrary",), vmem_limit_bytes=VMEM_LIMIT),
        name="mem_kv",
    )(mem, g, lw['w_xkv'])


def _xattn_prompt_kernel(x_ref, g_ref, wq_ref, kt_ref, vb_ref, wo_ref, o_ref):
    x = x_ref[0]
    h = _rms(x, g_ref[...]).astype(BF16)
    q = (_dot(h, wq_ref[...]) * (XATTN_HEAD_DIM ** -0.5)).astype(BF16)
    heads = []
    for hh in range(XATTN_HEADS):
        lo, hi = hh * XATTN_HEAD_DIM, (hh + 1) * XATTN_HEAD_DIM
        s = _dot(q[:, lo:hi], kt_ref[0, lo:hi, :])
        e = jnp.exp(s - jnp.max(s, axis=-1, keepdims=True))
        p = e * (1.0 / jnp.sum(e, axis=-1, keepdims=True))
        heads.append(_dot(p.astype(BF16), vb_ref[0, :, lo:hi]).astype(BF16))
    o_ref[0] = x + _dot(jnp.concatenate(heads, axis=1), wo_ref[...])


def _xattn_prompt(x, kt, vb, lw, *, tt=512):
    bsz, seq, _ = x.shape
    g = lw['norm_xattn'].reshape(1, -1)
    return pl.pallas_call(
        _xattn_prompt_kernel,
        grid=(bsz, seq // tt),
        in_specs=[pl.BlockSpec((1, tt, D_MODEL), lambda b, t: (b, t, 0)), _const_spec(g.shape),
                  _const_spec(lw['w_xq'].shape),
                  pl.BlockSpec((1, D_MODEL, MEM_LEN), lambda b, t: (b, 0, 0)),
                  pl.BlockSpec((1, MEM_LEN, D_MODEL), lambda b, t: (b, 0, 0)),
                  _const_spec(lw['w_xo'].shape)],
        out_specs=pl.BlockSpec((1, tt, D_MODEL), lambda b, t: (b, t, 0)),
        out_shape=jax.ShapeDtypeStruct(x.shape, F32),
        compiler_params=pltpu.CompilerParams(dimension_semantics=("arbitrary", "arbitrary"),
                                             vmem_limit_bytes=VMEM_LIMIT),
        name="xattn_prompt",
    )(x, g, lw['w_xq'], kt, vb, lw['w_xo'])


def _norm_proj_kernel(x_ref, g_ref, w_ref, o_ref, *, scale):
    o_ref[...] = _dot(_rms(x_ref[...], g_ref[...]).astype(BF16), w_ref[...]) * scale


def _norm_proj(x, g, w, scale):
    g = g.reshape(1, -1)
    out_shape = (x.shape[0], w.shape[1])
    return pl.pallas_call(
        functools.partial(_norm_proj_kernel, scale=scale),
        grid=(1,),
        in_specs=[_const_spec(x.shape), _const_spec(g.shape), _const_spec(w.shape)],
        out_specs=pl.BlockSpec(out_shape, lambda i: (0, 0)),
        out_shape=jax.ShapeDtypeStruct(out_shape, F32),
        compiler_params=pltpu.CompilerParams(dimension_semantics=("arbitrary",), vmem_limit_bytes=VMEM_LIMIT),
        name="norm_proj",
    )(x, g, w)


def _proj_residual_kernel(x_ref, a_ref, w_ref, o_ref):
    o_ref[...] = x_ref[...] + _dot(a_ref[...].astype(BF16), w_ref[...])


def _proj_residual(x, a, w):
    return pl.pallas_call(
        _proj_residual_kernel,
        grid=(1,),
        in_specs=[_const_spec(x.shape), _const_spec(a.shape), _const_spec(w.shape)],
        out_specs=pl.BlockSpec(x.shape, lambda i: (0, 0)),
        out_shape=jax.ShapeDtypeStruct(x.shape, F32),
        compiler_params=pltpu.CompilerParams(dimension_semantics=("arbitrary",), vmem_limit_bytes=VMEM_LIMIT),
        name="proj_residual",
    )(x, a, w)


def _attend_sample_kernel(q_ref, k_ref, v_ref, o_ref, *, rows):
    for r in range(rows):
        s = jnp.sum(k_ref[r] * q_ref[r][None], axis=-1, keepdims=True)
        e = jnp.exp(s - jnp.max(s, axis=0, keepdims=True))
        p = e * (1.0 / jnp.sum(e, axis=0, keepdims=True))
        o_ref[r] = jnp.sum(p * v_ref[r], axis=0)


def _attend_sample(q, cache_k, cache_v, layer, *, rows=2):
    n = q.shape[0]
    q_spec = pl.BlockSpec((rows, XATTN_HEADS, XATTN_HEAD_DIM), lambda i: (i, 0, 0))
    kv_spec = pl.BlockSpec((None, rows, MEM_LEN, XATTN_HEADS, XATTN_HEAD_DIM), lambda i: (layer, i, 0, 0, 0))
    return pl.pallas_call(
        functools.partial(_attend_sample_kernel, rows=rows),
        grid=(n // rows,),
        in_specs=[q_spec, kv_spec, kv_spec],
        out_specs=q_spec,
        out_shape=jax.ShapeDtypeStruct(q.shape, F32),
        compiler_params=pltpu.CompilerParams(dimension_semantics=("arbitrary",), vmem_limit_bytes=VMEM_LIMIT),
        name="attend_sample",
    )(q, cache_k, cache_v)


def _xattn_sample(xs, cache_k, cache_v, layer, lw):
    n = xs.shape[0]
    q = _norm_proj(xs, lw['norm_xattn'], lw['w_xq'], XATTN_HEAD_DIM ** -0.5)
    o = _attend_sample(q.reshape(n, XATTN_HEADS, XATTN_HEAD_DIM), cache_k, cache_v, layer)
    return _proj_residual(xs, o.reshape(n, D_MODEL), lw['w_xo'])


def _ffn_kernel(x_ref, g_ref, w1_ref, w2_ref, gf_ref, o_ref, *, final_norm):
    x = x_ref[...]
    h = _rms(x, g_ref[...]).astype(BF16)
    y = x
    for c in range(0, D_FF, FF_CHUNK):
        a = jnp.maximum(_dot(h, w1_ref[:, c:c + FF_CHUNK]), 0.0)
        y = y + _dot((a * a).astype(BF16), w2_ref[c:c + FF_CHUNK, :])
    o_ref[...] = _rms(y, gf_ref[...]) if final_norm else y


def _ffn(x2d, lw, norm_final, *, final_norm, tm=512):
    n = x2d.shape[0]
    tm = min(tm, n)
    g = lw['norm_ffn'].reshape(1, -1)
    gf = norm_final.reshape(1, -1)
    return pl.pallas_call(
        functools.partial(_ffn_kernel, final_norm=final_norm),
        grid=(n // tm,),
        in_specs=[pl.BlockSpec((tm, D_MODEL), lambda i: (i, 0)), _const_spec(g.shape),
                  _const_spec(lw['w_ff1'].shape), _const_spec(lw['w_ff2'].shape), _const_spec(gf.shape)],
        out_specs=pl.BlockSpec((tm, D_MODEL), lambda i: (i, 0)),
        out_shape=jax.ShapeDtypeStruct(x2d.shape, F32),
        compiler_params=pltpu.CompilerParams(dimension_semantics=("arbitrary",), vmem_limit_bytes=VMEM_LIMIT),
        name="ffn",
    )(x2d, g, lw['w_ff1'], lw['w_ff2'], gf)


def _layer_weights(l, p):
    tril = jnp.tril(jnp.ones((CHUNK, CHUNK), dtype=bool))
    ws = jnp.where(tril[None], p['gmlp_ws'][l], 0.0)
    pool_bd = jax.scipy.linalg.block_diag(*[p['pool_w'][l, g] for g in range(len(POOL_WINDOWS))])
    lw = {k: p[k][l] for k in ('norm_mix', 'gmlp_ln_g', 'gmlp_ln_b', 'conf_dw', 'conf_dw_b', 'conf_ln_g',
                               'conf_ln_b', 'sc_dw', 'pool_scale', 'mix_out_g', 'norm_xattn', 'norm_mem',
                               'norm_ffn')}
    lw.update(
        w_in=p['w_in'][l].astype(BF16), w_out=p['w_out'][l].astype(BF16),
        gmlp_wcat=jnp.concatenate([ws[h] for h in range(GMLP_HEADS)], axis=1).astype(BF16),
        gmlp_bias_tile=jnp.repeat(p['gmlp_bs'][l].T, GMLP_HEAD_DIM, axis=1),
        gmlp_w00=jnp.repeat(ws[:, 0, 0], GMLP_HEAD_DIM), gmlp_b0=jnp.repeat(p['gmlp_bs'][l][:, 0], GMLP_HEAD_DIM),
        pool_w_bd=pool_bd.astype(BF16),
        w_xq=p['w_xq'][l].astype(BF16), w_xo=p['w_xo'][l].astype(BF16),
        w_xkv=jnp.concatenate([p['w_xk'][l], p['w_xv'][l]], axis=1).astype(BF16),
        w_ff1=p['w_ff1'][l].astype(BF16), w_ff2=p['w_ff2'][l].astype(BF16))
    return lw


def kernel(x_prompt, x_sample, mem_prompt, cache_mem_k, cache_mem_v, state_conv_glu, state_conv_short, state_pool, norm_mix, w_in, gmlp_ln_g, gmlp_ln_b, gmlp_ws, gmlp_bs, conf_dw, conf_dw_b, conf_ln_g, conf_ln_b, sc_dw, pool_w, pool_scale, mix_out_g, w_out, norm_xattn, norm_mem, w_xq, w_xk, w_xv, w_xo, norm_ffn, w_ff1, w_ff2, norm_final):
    params = dict(norm_mix=norm_mix, w_in=w_in, gmlp_ln_g=gmlp_ln_g, gmlp_ln_b=gmlp_ln_b, gmlp_ws=gmlp_ws,
                  gmlp_bs=gmlp_bs, conf_dw=conf_dw, conf_dw_b=conf_dw_b, conf_ln_g=conf_ln_g, conf_ln_b=conf_ln_b,
                  sc_dw=sc_dw, pool_w=pool_w, pool_scale=pool_scale, mix_out_g=mix_out_g, w_out=w_out,
                  norm_xattn=norm_xattn, norm_mem=norm_mem, w_xq=w_xq, w_xk=w_xk, w_xv=w_xv, w_xo=w_xo,
                  norm_ffn=norm_ffn, w_ff1=w_ff1, w_ff2=w_ff2)
    depth = w_in.shape[0]
    bsz, seq, _ = x_prompt.shape
    n_s = x_sample.shape[0]
    xp = x_prompt
    xs = x_sample.reshape(n_s, D_MODEL)
    outs = {k: [] for k in ('mk', 'mv', 'glu_p', 'glu_s', 'sh_p', 'sh_s', 'pl_p', 'pl_s', 'v_s')}
    for l in range(depth):
        lw = _layer_weights(l, params)
        last = l == depth - 1
        xp, glu_p, sh_p, pool_p = _mixer_prompt(xp, lw)
        k_p, v_p, kt, vb = _mem_kv(mem_prompt, lw)
        xp = _xattn_prompt(xp, kt, vb, lw)
        xp = _ffn(xp.reshape(bsz * seq, D_MODEL), lw, norm_final, final_norm=last).reshape(bsz, seq, D_MODEL)
        outs['mk'].append(k_p.reshape(bsz, MEM_LEN, XATTN_HEADS, XATTN_HEAD_DIM))
        outs['mv'].append(v_p.reshape(bsz, MEM_LEN, XATTN_HEADS, XATTN_HEAD_DIM))
        outs['glu_p'].append(glu_p); outs['sh_p'].append(sh_p); outs['pl_p'].append(pool_p)
        xs, vn_s, glu_s, sh_s, pool_s = _mixer_sample(xs, state_conv_glu[l], state_conv_short[l], state_pool[l], lw)
        xs = _xattn_sample(xs, cache_mem_k, cache_mem_v, l, lw)
        xs = _ffn(xs, lw, norm_final, final_norm=last)
        outs['glu_s'].append(glu_s); outs['sh_s'].append(sh_s); outs['pl_s'].append(pool_s)
        outs['v_s'].append(vn_s.reshape(n_s, 1, D_GROUP))
    st = lambda k: jnp.stack(outs[k], axis=0)
    return (xp, xs.reshape(n_s, 1, D_MODEL), st('mk'), st('mv'), st('glu_p'), st('glu_s'), st('sh_p'), st('sh_s'),
            st('pl_p'), st('pl_s'), st('v_s'))
```

```python
import functools

import jax
import jax.numpy as jnp
from jax import lax
from jax.experimental import pallas as pl
from jax.experimental.pallas import tpu as pltpu

F32 = jnp.float32
BF16 = jnp.bfloat16

D_MODEL = 1024
D_GROUP = 256
N_GROUPS = 4
D_IN_PROJ = 8 * D_GROUP
GMLP_HEADS = 4
GMLP_HEAD_DIM = D_GROUP // GMLP_HEADS
CHUNK = 128
CONF_WIDTH = 31
SC_WIDTH = 3
POOL_WINDOWS = (2, 4, 8, 16)
POOL_GROUP_DIM = D_GROUP // len(POOL_WINDOWS)
POOL_BUF = max(POOL_WINDOWS) - 1
MEM_LEN = 256
XATTN_HEADS = 4
XATTN_HEAD_DIM = D_MODEL // XATTN_HEADS
D_FF = 4 * D_MODEL
PAST_LEN = 16384
EPS = 1e-6

GLU_HALO = 32
SC_HALO = 8
POOL_HALO = 32
LANES = 128
SUBLANES = 8
LANE_SLABS = D_GROUP // LANES
CONV_STRIDE = 4
CONV_ROWS = 64
FF_CHUNK = 1024
VMEM_LIMIT = 56 * 1024 * 1024


def _rms(x, g):
    return x * lax.rsqrt(jnp.mean(x * x, axis=-1, keepdims=True) + EPS) * g


def _layer_norm(x, g, b):
    xc = x - jnp.mean(x, axis=-1, keepdims=True)
    return xc * lax.rsqrt(jnp.mean(xc * xc, axis=-1, keepdims=True) + EPS) * g + b


def _sigmoid(x):
    return 0.5 * jnp.tanh(0.5 * x) + 0.5


def _dot(a, b):
    return jnp.dot(a, b, preferred_element_type=F32)


def _const_spec(shape):
    zeros = (0,) * len(shape)
    return pl.BlockSpec(shape, lambda *_: zeros, pipeline_mode=pl.Buffered(1))


def _pool_window_lanes(shape):
    grp = lax.broadcasted_iota(jnp.int32, shape, len(shape) - 1) // POOL_GROUP_DIM
    win = jnp.full(shape, float(POOL_WINDOWS[-1]), F32)
    for g in range(len(POOL_WINDOWS) - 2, -1, -1):
        win = jnp.where(grp == g, float(POOL_WINDOWS[g]), win)
    return grp, win


def _select_by_group(grp, vals):
    out = vals[-1]
    for g in range(len(vals) - 2, -1, -1):
        out = jnp.where(grp == g, vals[g], out)
    return out


def _mix_out(x, y_groups, mog_ref, wout_ref):
    yn = [_rms(y, mog_ref[:, g * D_GROUP:(g + 1) * D_GROUP]).astype(BF16) for g, y in enumerate(y_groups)]
    return x + _dot(jnp.concatenate(yn, axis=1), wout_ref[...])


def _mixer_prompt_kernel(x_ref, g_ref, win_ref, lng_ref, lnb_ref, wcat_ref, bst_ref, cw_ref, cb_ref, clg_ref,
                         clb_ref, sw_ref, pw_ref, ps_ref, mog_ref, wout_ref,
                         o_ref, oglu_ref, osh_ref, opool_ref,
                         glu_buf0, glu_buf1, yb_buf0, yb_buf1, sc_buf, p_buf, s2_buf, s4_buf, s8_buf, *, tt, n_t):
    t = pl.program_id(1)
    glu_buf = (glu_buf0, glu_buf1)
    yb_buf = (yb_buf0, yb_buf1)

    @pl.when(t == 0)
    def _():
        for s in range(LANE_SLABS):
            glu_buf[s][0:GLU_HALO, :] = jnp.zeros((GLU_HALO, LANES), F32)
        sc_buf[0:SC_HALO, :] = jnp.zeros((SC_HALO, D_GROUP), F32)
        p_buf[0:POOL_HALO, :] = jnp.zeros((POOL_HALO, D_GROUP), F32)

    x = x_ref[0]
    h = _rms(x, g_ref[...]).astype(BF16)
    z = _dot(h, win_ref[...])
    u, v, glu_a, glu_g, sc_b, sc_c, sc_x, pool_x = [z[:, i * D_GROUP:(i + 1) * D_GROUP] for i in range(8)]

    vn = _layer_norm(v, lng_ref[...], lnb_ref[...])
    head = lax.broadcasted_iota(jnp.int32, (CHUNK, D_GROUP), 1) // GMLP_HEAD_DIM
    gates = []
    for c in range(tt // CHUNK):
        vc = vn[c * CHUNK:(c + 1) * CHUNK]
        stack = jnp.concatenate([jnp.where(head == hh, vc, 0.0) for hh in range(GMLP_HEADS)], axis=0)
        gates.append(_dot(wcat_ref[...], stack.astype(BF16)) + bst_ref[...])
    y_a = u * jnp.concatenate(gates, axis=0)

    glu = glu_a * _sigmoid(glu_g)
    for s in range(LANE_SLABS):
        glu_buf[s][GLU_HALO:GLU_HALO + tt, :] = glu[:, s * LANES:(s + 1) * LANES]
    first = GLU_HALO - (CONF_WIDTH - 1)
    span = SUBLANES * CONV_STRIDE
    for r0 in range(0, tt, CONV_ROWS):
        starts = [r0 + (i // CONV_STRIDE) * span + i % CONV_STRIDE for i in range(CONV_ROWS // SUBLANES)]
        accs = [[None] * len(starts) for _ in range(LANE_SLABS)]
        for k in range(CONF_WIDTH):
            for s in range(LANE_SLABS):
                wk = jnp.broadcast_to(cw_ref[k:k + 1, s * LANES:(s + 1) * LANES], (SUBLANES, LANES))
                for i, t0 in enumerate(starts):
                    term = wk * glu_buf[s][pl.ds(first + t0 + k, SUBLANES, stride=CONV_STRIDE), :]
                    accs[s][i] = term if k == 0 else accs[s][i] + term
        conv = jnp.concatenate([jnp.concatenate(a, axis=0) for a in accs], axis=1)
        ln = _layer_norm(conv + cb_ref[...], clg_ref[...], clb_ref[...])
        yb = ln * _sigmoid(ln)
        for s in range(LANE_SLABS):
            for i, t0 in enumerate(starts):
                yb_buf[s][pl.ds(t0, SUBLANES, stride=CONV_STRIDE), :] = (
                    yb[i * SUBLANES:(i + 1) * SUBLANES, s * LANES:(s + 1) * LANES])
    y_b = jnp.concatenate([yb_buf[s][...] for s in range(LANE_SLABS)], axis=1)

    sxc = sc_c * sc_x
    sc_buf[SC_HALO:SC_HALO + tt, :] = sxc
    conv_c = (sw_ref[0:1, :] * sc_buf[SC_HALO - 2:SC_HALO - 2 + tt, :]
              + sw_ref[1:2, :] * sc_buf[SC_HALO - 1:SC_HALO - 1 + tt, :]
              + sw_ref[2:3, :] * sxc)
    y_c = sc_b * conv_c

    p_buf[POOL_HALO:POOL_HALO + tt, :] = pool_x
    end = POOL_HALO + tt
    s2_buf[8:end, :] = p_buf[8:end, :] + p_buf[7:end - 1, :]
    s4_buf[16:end, :] = s2_buf[16:end, :] + s2_buf[14:end - 2, :]
    s8_buf[24:end, :] = s4_buf[24:end, :] + s4_buf[20:end - 4, :]
    s16 = s8_buf[POOL_HALO:end, :] + s8_buf[POOL_HALO - 8:end - 8, :]
    grp, win = _pool_window_lanes((tt, D_GROUP))
    ssum = _select_by_group(grp, [s2_buf[POOL_HALO:end, :], s4_buf[POOL_HALO:end, :],
                                  s8_buf[POOL_HALO:end, :], s16])
    head_rows = POOL_BUF + 1
    pos = t * tt + lax.broadcasted_iota(jnp.int32, (head_rows, D_GROUP), 0)
    cnt = jnp.minimum(_pool_window_lanes((head_rows, D_GROUP))[1], (pos + 1).astype(F32))
    inv_win = _select_by_group(_pool_window_lanes((tt - head_rows, D_GROUP))[0], [1.0 / w for w in POOL_WINDOWS])
    mean = jnp.concatenate([ssum[:head_rows] / cnt, ssum[head_rows:] * inv_win], axis=0)
    pooled = mean - pool_x
    y_d = _dot(pooled.astype(BF16), pw_ref[...]) * ps_ref[...]

    o_ref[0] = _mix_out(x, [y_a, y_b, y_c, y_d], mog_ref, wout_ref)

    @pl.when(t == n_t - 1)
    def _():
        oglu_ref[0] = jnp.concatenate(
            [glu_buf[s][GLU_HALO + tt - (CONF_WIDTH - 1):GLU_HALO + tt, :] for s in range(LANE_SLABS)], axis=1)
        osh_ref[0] = sc_buf[SC_HALO + tt - (SC_WIDTH - 1):SC_HALO + tt, :]
        opool_ref[0] = p_buf[POOL_HALO + tt - POOL_BUF:POOL_HALO + tt, :]

    for s in range(LANE_SLABS):
        glu_buf[s][0:GLU_HALO, :] = glu_buf[s][tt:tt + GLU_HALO, :]
    sc_buf[0:SC_HALO, :] = sc_buf[tt:tt + SC_HALO, :]
    p_buf[0:POOL_HALO, :] = p_buf[tt:tt + POOL_HALO, :]


def _mixer_prompt(x, lw, *, tt=512):
    bsz, seq, _ = x.shape
    n_t = seq // tt
    row = lambda a: a.reshape(1, -1)
    small = [row(lw['norm_mix']), lw['w_in'], row(lw['gmlp_ln_g']), row(lw['gmlp_ln_b']), lw['gmlp_wcat'],
             lw['gmlp_bias_tile'], lw['conf_dw'], row(lw['conf_dw_b']), row(lw['conf_ln_g']), row(lw['conf_ln_b']),
             lw['sc_dw'], lw['pool_w_bd'], row(lw['pool_scale']), row(lw['mix_out_g']), lw['w_out']]
    state_spec = lambda rows: pl.BlockSpec((1, rows, D_GROUP), lambda b, t: (b, 0, 0))
    return pl.pallas_call(
        functools.partial(_mixer_prompt_kernel, tt=tt, n_t=n_t),
        grid=(bsz, n_t),
        in_specs=[pl.BlockSpec((1, tt, D_MODEL), lambda b, t: (b, t, 0))] + [_const_spec(a.shape) for a in small],
        out_specs=[pl.BlockSpec((1, tt, D_MODEL), lambda b, t: (b, t, 0)),
                   state_spec(CONF_WIDTH - 1), state_spec(SC_WIDTH - 1), state_spec(POOL_BUF)],
        out_shape=[jax.ShapeDtypeStruct(x.shape, F32),
                   jax.ShapeDtypeStruct((bsz, CONF_WIDTH - 1, D_GROUP), F32),
                   jax.ShapeDtypeStruct((bsz, SC_WIDTH - 1, D_GROUP), F32),
                   jax.ShapeDtypeStruct((bsz, POOL_BUF, D_GROUP), F32)],
        scratch_shapes=[pltpu.VMEM((GLU_HALO + tt, LANES), F32)] * LANE_SLABS + [pltpu.VMEM((tt, LANES), F32)] * LANE_SLABS
        + [pltpu.VMEM((SC_HALO + tt, D_GROUP), F32)]
        + [pltpu.VMEM((POOL_HALO + tt, D_GROUP), F32)] * 4,
        compiler_params=pltpu.CompilerParams(dimension_semantics=("arbitrary", "arbitrary"),
                                             vmem_limit_bytes=VMEM_LIMIT),
        name="mixer_prompt",
    )(x, *small)


def _mixer_sample_kernel(x_ref, stg_ref, sts_ref, stp_ref, g_ref, win_ref, lng_ref, lnb_ref, w00_ref, b0_ref,
                         cw_ref, cb_ref, clg_ref, clb_ref, sw_ref, pw_ref, ps_ref, mog_ref, wout_ref,
                         o_ref, ovn_ref, oglu_ref, osh_ref, opool_ref):
    x = x_ref[...]
    h = _rms(x, g_ref[...]).astype(BF16)
    z = _dot(h, win_ref[...])
    u, v, glu_a, glu_g, sc_b, sc_c, sc_x, pool_x = [z[:, i * D_GROUP:(i + 1) * D_GROUP] for i in range(8)]
    hist = lambda ref, k: ref[:, k * D_GROUP:(k + 1) * D_GROUP]

    vn = _layer_norm(v, lng_ref[...], lnb_ref[...])
    ovn_ref[...] = vn
    y_a = u * (w00_ref[...] * vn + b0_ref[...])

    glu = glu_a * _sigmoid(glu_g)
    n_hist = CONF_WIDTH - 1
    acc = cw_ref[n_hist:n_hist + 1, :] * glu
    for k in range(n_hist):
        acc = acc + cw_ref[k:k + 1, :] * hist(stg_ref, k)
    ln = _layer_norm(acc + cb_ref[...], clg_ref[...], clb_ref[...])
    y_b = ln * _sigmoid(ln)
    oglu_ref[:, 0:(n_hist - 1) * D_GROUP] = stg_ref[:, D_GROUP:]
    oglu_ref[:, (n_hist - 1) * D_GROUP:] = glu

    sxc = sc_c * sc_x
    y_c = sc_b * (sw_ref[0:1, :] * hist(sts_ref, 0) + sw_ref[1:2, :] * hist(sts_ref, 1) + sw_ref[2:3, :] * sxc)
    osh_ref[:, 0:D_GROUP] = hist(sts_ref, 1)
    osh_ref[:, D_GROUP:] = sxc

    run = pool_x
    sums = []
    back = 0
    for w in POOL_WINDOWS:
        while back < w - 1:
            run = run + hist(stp_ref, POOL_BUF - 1 - back)
            back += 1
        sums.append(run)
    grp, win = _pool_window_lanes(pool_x.shape)
    cnt = jnp.minimum(win, float(PAST_LEN + 1))
    pooled = _select_by_group(grp, sums) / cnt - pool_x
    y_d = _dot(pooled.astype(BF16), pw_ref[...]) * ps_ref[...]
    opool_ref[:, 0:(POOL_BUF - 1) * D_GROUP] = stp_ref[:, D_GROUP:]
    opool_ref[:, (POOL_BUF - 1) * D_GROUP:] = pool_x

    o_ref[...] = _mix_out(x, [y_a, y_b, y_c, y_d], mog_ref, wout_ref)


def _mixer_sample(xs, st_glu, st_sh, st_pool, lw):
    n = xs.shape[0]
    row = lambda a: a.reshape(1, -1)
    args = [xs, st_glu.reshape(n, -1), st_sh.reshape(n, -1), st_pool.reshape(n, -1),
            row(lw['norm_mix']), lw['w_in'], row(lw['gmlp_ln_g']), row(lw['gmlp_ln_b']), row(lw['gmlp_w00']),
            row(lw['gmlp_b0']), lw['conf_dw'], row(lw['conf_dw_b']), row(lw['conf_ln_g']), row(lw['conf_ln_b']),
            lw['sc_dw'], lw['pool_w_bd'], row(lw['pool_scale']), row(lw['mix_out_g']), lw['w_out']]
    out_shapes = [(n, D_MODEL), (n, D_GROUP), (n, (CONF_WIDTH - 1) * D_GROUP), (n, (SC_WIDTH - 1) * D_GROUP),
                  (n, POOL_BUF * D_GROUP)]
    xo, vn, nglu, nsh, npool = pl.pallas_call(
        _mixer_sample_kernel,
        grid=(1,),
        in_specs=[_const_spec(a.shape) for a in args],
        out_specs=[pl.BlockSpec(s, lambda i: (0, 0)) for s in out_shapes],
        out_shape=[jax.ShapeDtypeStruct(s, F32) for s in out_shapes],
        compiler_params=pltpu.CompilerParams(dimension_semantics=("arbitrary",), vmem_limit_bytes=VMEM_LIMIT),
        name="mixer_sample",
    )(*args)
    return (xo, vn, nglu.reshape(n, CONF_WIDTH - 1, D_GROUP), nsh.reshape(n, SC_WIDTH - 1, D_GROUP),
            npool.reshape(n, POOL_BUF, D_GROUP))


def _mem_kv_kernel(m_ref, g_ref, wkv_ref, ok_ref, ov_ref, okt_ref, ovb_ref):
    m = _rms(m_ref[0], g_ref[...]).astype(BF16)
    kv = _dot(m, wkv_ref[...])
    k, v = kv[:, :D_MODEL], kv[:, D_MODEL:]
    ok_ref[0] = k
    ov_ref[0] = v
    okt_ref[0] = k.T.astype(BF16)
    ovb_ref[0] = v.astype(BF16)


def _mem_kv(mem, lw):
    bsz = mem.shape[0]
    blk = lambda r, c: pl.BlockSpec((1, r, c), lambda b: (b, 0, 0))
    g = lw['norm_mem'].reshape(1, -1)
    return pl.pallas_call(
        _mem_kv_kernel,
        grid=(bsz,),
        in_specs=[blk(MEM_LEN, D_MODEL), _const_spec(g.shape), _const_spec(lw['w_xkv'].shape)],
        out_specs=[blk(MEM_LEN, D_MODEL), blk(MEM_LEN, D_MODEL), blk(D_MODEL, MEM_LEN), blk(MEM_LEN, D_MODEL)],
        out_shape=[jax.ShapeDtypeStruct((bsz, MEM_LEN, D_MODEL), F32)] * 2
        + [jax.ShapeDtypeStruct((bsz, D_MODEL, MEM_LEN), BF16), jax.ShapeDtypeStruct((bsz, MEM_LEN, D_MODEL), BF16)],
        compiler_params=pltpu.CompilerParams(dimension_semantics=("arbitrary",), vmem_limit_bytes=VMEM_LIMIT),
        name="mem_kv",
    )(mem, g, lw['w_xkv'])


def _xattn_prompt_kernel(x_ref, g_ref, wq_ref, kt_ref, vb_ref, wo_ref, o_ref):
    x = x_ref[0]
    h = _rms(x, g_ref[...]).astype(BF16)
    q = (_dot(h, wq_ref[...]) * (XATTN_HEAD_DIM ** -0.5)).astype(BF16)
    heads = []
    for hh in range(XATTN_HEADS):
        lo, hi = hh * XATTN_HEAD_DIM, (hh + 1) * XATTN_HEAD_DIM
        s = _dot(q[:, lo:hi], kt_ref[0, lo:hi, :])
        e = jnp.exp(s - jnp.max(s, axis=-1, keepdims=True))
        p = e * (1.0 / jnp.sum(e, axis=-1, keepdims=True))
        heads.append(_dot(p.astype(BF16), vb_ref[0, :, lo:hi]).astype(BF16))
    o_ref[0] = x + _dot(jnp.concatenate(heads, axis=1), wo_ref[...])


def _xattn_prompt(x, kt, vb, lw, *, tt=512):
    bsz, seq, _ = x.shape
    g = lw['norm_xattn'].reshape(1, -1)
    return pl.pallas_call(
        _xattn_prompt_kernel,
        grid=(bsz, seq // tt),
        in_specs=[pl.BlockSpec((1, tt, D_MODEL), lambda b, t: (b, t, 0)), _const_spec(g.shape),
                  _const_spec(lw['w_xq'].shape),
                  pl.BlockSpec((1, D_MODEL, MEM_LEN), lambda b, t: (b, 0, 0)),
                  pl.BlockSpec((1, MEM_LEN, D_MODEL), lambda b, t: (b, 0, 0)),
                  _const_spec(lw['w_xo'].shape)],
        out_specs=pl.BlockSpec((1, tt, D_MODEL), lambda b, t: (b, t, 0)),
        out_shape=jax.ShapeDtypeStruct(x.shape, F32),
        compiler_params=pltpu.CompilerParams(dimension_semantics=("arbitrary", "arbitrary"),
                                             vmem_limit_bytes=VMEM_LIMIT),
        name="xattn_prompt",
    )(x, g, lw['w_xq'], kt, vb, lw['w_xo'])


def _norm_proj_kernel(x_ref, g_ref, w_ref, o_ref, *, scale):
    o_ref[...] = _dot(_rms(x_ref[...], g_ref[...]).astype(BF16), w_ref[...]) * scale


def _norm_proj(x, g, w, scale):
    g = g.reshape(1, -1)
    out_shape = (x.shape[0], w.shape[1])
    return pl.pallas_call(
        functools.partial(_norm_proj_kernel, scale=scale),
        grid=(1,),
        in_specs=[_const_spec(x.shape), _const_spec(g.shape), _const_spec(w.shape)],
        out_specs=pl.BlockSpec(out_shape, lambda i: (0, 0)),
        out_shape=jax.ShapeDtypeStruct(out_shape, F32),
        compiler_params=pltpu.CompilerParams(dimension_semantics=("arbitrary",), vmem_limit_bytes=VMEM_LIMIT),
        name="norm_proj",
    )(x, g, w)


def _proj_residual_kernel(x_ref, a_ref, w_ref, o_ref):
    o_ref[...] = x_ref[...] + _dot(a_ref[...].astype(BF16), w_ref[...])


def _proj_residual(x, a, w):
    return pl.pallas_call(
        _proj_residual_kernel,
        grid=(1,),
        in_specs=[_const_spec(x.shape), _const_spec(a.shape), _const_spec(w.shape)],
        out_specs=pl.BlockSpec(x.shape, lambda i: (0, 0)),
        out_shape=jax.ShapeDtypeStruct(x.shape, F32),
        compiler_params=pltpu.CompilerParams(dimension_semantics=("arbitrary",), vmem_limit_bytes=VMEM_LIMIT),
        name="proj_residual",
    )(x, a, w)


def _attend_sample_kernel(q_ref, k_ref, v_ref, o_ref, *, rows):
    for r in range(rows):
        s = jnp.sum(k_ref[r] * q_ref[r][None], axis=-1, keepdims=True)
        e = jnp.exp(s - jnp.max(s, axis=0, keepdims=True))
        p = e * (1.0 / jnp.sum(e, axis=0, keepdims=True))
        o_ref[r] = jnp.sum(p * v_ref[r], axis=0)


def _attend_sample(q, cache_k, cache_v, layer, *, rows=2):
    n = q.shape[0]
    q_spec = pl.BlockSpec((rows, XATTN_HEADS, XATTN_HEAD_DIM), lambda i: (i, 0, 0))
    kv_spec = pl.BlockSpec((None, rows, MEM_LEN, XATTN_HEADS, XATTN_HEAD_DIM), lambda i: (layer, i, 0, 0, 0))
    return pl.pallas_call(
        functools.partial(_attend_sample_kernel, rows=rows),
        grid=(n // rows,),
        in_specs=[q_spec, kv_spec, kv_spec],
        out_specs=q_spec,
        out_shape=jax.ShapeDtypeStruct(q.shape, F32),
        compiler_params=pltpu.CompilerParams(dimension_semantics=("arbitrary",), vmem_limit_bytes=VMEM_LIMIT),
        name="attend_sample",
    )(q, cache_k, cache_v)


def _xattn_sample(xs, cache_k, cache_v, layer, lw):
    n = xs.shape[0]
    q = _norm_proj(xs, lw['norm_xattn'], lw['w_xq'], XATTN_HEAD_DIM ** -0.5)
    o = _attend_sample(q.reshape(n, XATTN_HEADS, XATTN_HEAD_DIM), cache_k, cache_v, layer)
    return _proj_residual(xs, o.reshape(n, D_MODEL), lw['w_xo'])


def _ffn_kernel(x_ref, g_ref, w1_ref, w2_ref, gf_ref, o_ref, *, final_norm):
    x = x_ref[...]
    h = _rms(x, g_ref[...]).astype(BF16)
    y = x
    for c in range(0, D_FF, FF_CHUNK):
        a = jnp.maximum(_dot(h, w1_ref[:, c:c + FF_CHUNK]), 0.0)
        y = y + _dot((a * a).astype(BF16), w2_ref[c:c + FF_CHUNK, :])
    o_ref[...] = _rms(y, gf_ref[...]) if final_norm else y


def _ffn(x2d, lw, norm_final, *, final_norm, tm=512):
    n = x2d.shape[0]
    tm = min(tm, n)
    g = lw['norm_ffn'].reshape(1, -1)
    gf = norm_final.reshape(1, -1)
    return pl.pallas_call(
        functools.partial(_ffn_kernel, final_norm=final_norm),
        grid=(n // tm,),
        in_specs=[pl.BlockSpec((tm, D_MODEL), lambda i: (i, 0)), _const_spec(g.shape),
                  _const_spec(lw['w_ff1'].shape), _const_spec(lw['w_ff2'].shape), _const_spec(gf.shape)],
        out_specs=pl.BlockSpec((tm, D_MODEL), lambda i: (i, 0)),
        out_shape=jax.ShapeDtypeStruct(x2d.shape, F32),
        compiler_params=pltpu.CompilerParams(dimension_semantics=("arbitrary",), vmem_limit_bytes=VMEM_LIMIT),
        name="ffn",
    )(x2d, g, lw['w_ff1'], lw['w_ff2'], gf)


def _layer_weights(l, p):
    tril = jnp.tril(jnp.ones((CHUNK, CHUNK), dtype=bool))
    ws = jnp.where(tril[None], p['gmlp_ws'][l], 0.0)
    pool_bd = jax.scipy.linalg.block_diag(*[p['pool_w'][l, g] for g in range(len(POOL_WINDOWS))])
    lw = {k: p[k][l] for k in ('norm_mix', 'gmlp_ln_g', 'gmlp_ln_b', 'conf_dw', 'conf_dw_b', 'conf_ln_g',
                               'conf_ln_b', 'sc_dw', 'pool_scale', 'mix_out_g', 'norm_xattn', 'norm_mem',
                               'norm_ffn')}
    lw.update(
        w_in=p['w_in'][l].astype(BF16), w_out=p['w_out'][l].astype(BF16),
        gmlp_wcat=jnp.concatenate([ws[h] for h in range(GMLP_HEADS)], axis=1).astype(BF16),
        gmlp_bias_tile=jnp.repeat(p['gmlp_bs'][l].T, GMLP_HEAD_DIM, axis=1),
        gmlp_w00=jnp.repeat(ws[:, 0, 0], GMLP_HEAD_DIM), gmlp_b0=jnp.repeat(p['gmlp_bs'][l][:, 0], GMLP_HEAD_DIM),
        pool_w_bd=pool_bd.astype(BF16),
        w_xq=p['w_xq'][l].astype(BF16), w_xo=p['w_xo'][l].astype(BF16),
        w_xkv=jnp.concatenate([p['w_xk'][l], p['w_xv'][l]], axis=1).astype(BF16),
        w_ff1=p['w_ff1'][l].astype(BF16), w_ff2=p['w_ff2'][l].astype(BF16))
    return lw


def kernel(x_prompt, x_sample, mem_prompt, cache_mem_k, cache_mem_v, state_conv_glu, state_conv_short, state_pool, norm_mix, w_in, gmlp_ln_g, gmlp_ln_b, gmlp_ws, gmlp_bs, conf_dw, conf_dw_b, conf_ln_g, conf_ln_b, sc_dw, pool_w, pool_scale, mix_out_g, w_out, norm_xattn, norm_mem, w_xq, w_xk, w_xv, w_xo, norm_ffn, w_ff1, w_ff2, norm_final):
    params = dict(norm_mix=norm_mix, w_in=w_in, gmlp_ln_g=gmlp_ln_g, gmlp_ln_b=gmlp_ln_b, gmlp_ws=gmlp_ws,
                  gmlp_bs=gmlp_bs, conf_dw=conf_dw, conf_dw_b=conf_dw_b, conf_ln_g=conf_ln_g, conf_ln_b=conf_ln_b,
                  sc_dw=sc_dw, pool_w=pool_w, pool_scale=pool_scale, mix_out_g=mix_out_g, w_out=w_out,
                  norm_xattn=norm_xattn, norm_mem=norm_mem, w_xq=w_xq, w_xk=w_xk, w_xv=w_xv, w_xo=w_xo,
                  norm_ffn=norm_ffn, w_ff1=w_ff1, w_ff2=w_ff2)
    depth = w_in.shape[0]
    bsz, seq, _ = x_prompt.shape
    n_s = x_sample.shape[0]
    xp = x_prompt
    xs = x_sample.reshape(n_s, D_MODEL)
    outs = {k: [] for k in ('mk', 'mv', 'glu_p', 'glu_s', 'sh_p', 'sh_s', 'pl_p', 'pl_s', 'v_s')}
    for l in range(depth):
        lw = _layer_weights(l, params)
        last = l == depth - 1
        xp, glu_p, sh_p, pool_p = _mixer_prompt(xp, lw)
        k_p, v_p, kt, vb = _mem_kv(mem_prompt, lw)
        xp = _xattn_prompt(xp, kt, vb, lw)
        xp = _ffn(xp.reshape(bsz * seq, D_MODEL), lw, norm_final, final_norm=last).reshape(bsz, seq, D_MODEL)
        outs['mk'].append(k_p.reshape(bsz, MEM_LEN, XATTN_HEADS, XATTN_HEAD_DIM))
        outs['mv'].append(v_p.reshape(bsz, MEM_LEN, XATTN_HEADS, XATTN_HEAD_DIM))
        outs['glu_p'].append(glu_p); outs['sh_p'].append(sh_p); outs['pl_p'].append(pool_p)
        xs, vn_s, glu_s, sh_s, pool_s = _mixer_sample(xs, state_conv_glu[l], state_conv_short[l], state_pool[l], lw)
        xs = _xattn_sample(xs, cache_mem_k, cache_mem_v, l, lw)
        xs = _ffn(xs, lw, norm_final, final_norm=last)
        outs['glu_s'].append(glu_s); outs['sh_s'].append(sh_s); outs['pl_s'].append(pool_s)
        outs['v_s'].append(vn_s.reshape(n_s, 1, D_GROUP))
    st = lambda k: jnp.stack(outs[k], axis=0)
    return (xp, xs.reshape(n_s, 1, D_MODEL), st('mk'), st('mv'), st('glu_p'), st('glu_s'), st('sh_p'), st('sh_s'),
            st('pl_p'), st('pl_s'), st('v_s'))
```

```python
import functools

import jax
import jax.numpy as jnp
from jax import lax
from jax.experimental import pallas as pl
from jax.experimental.pallas import tpu as pltpu

F32 = jnp.float32
BF16 = jnp.bfloat16

D_MODEL = 1024
D_GROUP = 256
N_GROUPS = 4
D_IN_PROJ = 8 * D_GROUP
GMLP_HEADS = 4
GMLP_HEAD_DIM = D_GROUP // GMLP_HEADS
CHUNK = 128
CONF_WIDTH = 31
SC_WIDTH = 3
POOL_WINDOWS = (2, 4, 8, 16)
POOL_GROUP_DIM = D_GROUP // len(POOL_WINDOWS)
POOL_BUF = max(POOL_WINDOWS) - 1
MEM_LEN = 256
XATTN_HEADS = 4
XATTN_HEAD_DIM = D_MODEL // XATTN_HEADS
D_FF = 4 * D_MODEL
PAST_LEN = 16384
EPS = 1e-6

GLU_HALO = 32
SC_HALO = 8
POOL_HALO = 32
LANES = 128
SUBLANES = 8
LANE_SLABS = D_GROUP // LANES
CONV_STRIDE = 4
CONV_ROWS = 64
FF_CHUNK = 1024
VMEM_LIMIT = 56 * 1024 * 1024


def _rms(x, g):
    return x * lax.rsqrt(jnp.mean(x * x, axis=-1, keepdims=True) + EPS) * g


def _layer_norm(x, g, b):
    xc = x - jnp.mean(x, axis=-1, keepdims=True)
    return xc * lax.rsqrt(jnp.mean(xc * xc, axis=-1, keepdims=True) + EPS) * g + b


def _sigmoid(x):
    return 0.5 * jnp.tanh(0.5 * x) + 0.5


def _dot(a, b):
    return jnp.dot(a, b, preferred_element_type=F32)


def _const_spec(shape):
    zeros = (0,) * len(shape)
    return pl.BlockSpec(shape, lambda *_: zeros, pipeline_mode=pl.Buffered(1))


def _pool_window_lanes(shape):
    grp = lax.broadcasted_iota(jnp.int32, shape, len(shape) - 1) // POOL_GROUP_DIM
    win = jnp.full(shape, float(POOL_WINDOWS[-1]), F32)
    for g in range(len(POOL_WINDOWS) - 2, -1, -1):
        win = jnp.where(grp == g, float(POOL_WINDOWS[g]), win)
    return grp, win


def _select_by_group(grp, vals):
    out = vals[-1]
    for g in range(len(vals) - 2, -1, -1):
        out = jnp.where(grp == g, vals[g], out)
    return out


def _mix_out(x, y_groups, mog_ref, wout_ref):
    yn = [_rms(y, mog_ref[:, g * D_GROUP:(g + 1) * D_GROUP]).astype(BF16) for g, y in enumerate(y_groups)]
    return x + _dot(jnp.concatenate(yn, axis=1), wout_ref[...])


def _mixer_prompt_kernel(x_ref, g_ref, win_ref, lng_ref, lnb_ref, wcat_ref, bst_ref, cw_ref, cb_ref, clg_ref,
                         clb_ref, sw_ref, pw_ref, ps_ref, mog_ref, wout_ref,
                         o_ref, oglu_ref, osh_ref, opool_ref,
                         glu_buf0, glu_buf1, yb_buf0, yb_buf1, sc_buf, p_buf, s2_buf, s4_buf, s8_buf, *, tt, n_t, sub):
    t = pl.program_id(1)
    glu_buf = (glu_buf0, glu_buf1)
    yb_buf = (yb_buf0, yb_buf1)

    @pl.when(t == 0)
    def _():
        for s in range(LANE_SLABS):
            glu_buf[s][0:GLU_HALO, :] = jnp.zeros((GLU_HALO, LANES), F32)
        sc_buf[0:SC_HALO, :] = jnp.zeros((SC_HALO, D_GROUP), F32)
        p_buf[0:POOL_HALO, :] = jnp.zeros((POOL_HALO, D_GROUP), F32)

    for r0 in range(0, tt, sub):
        _mix_rows(t, r0, sub, x_ref, g_ref, win_ref, lng_ref, lnb_ref, wcat_ref, bst_ref, cw_ref, cb_ref, clg_ref,
                  clb_ref, sw_ref, pw_ref, ps_ref, mog_ref, wout_ref, o_ref,
                  glu_buf, yb_buf, sc_buf, p_buf, s2_buf, s4_buf, s8_buf, tt=tt)

    @pl.when(t == n_t - 1)
    def _():
        oglu_ref[0] = jnp.concatenate(
            [glu_buf[s][GLU_HALO + tt - (CONF_WIDTH - 1):GLU_HALO + tt, :] for s in range(LANE_SLABS)], axis=1)
        osh_ref[0] = sc_buf[SC_HALO + tt - (SC_WIDTH - 1):SC_HALO + tt, :]
        opool_ref[0] = p_buf[POOL_HALO + tt - POOL_BUF:POOL_HALO + tt, :]

    for s in range(LANE_SLABS):
        glu_buf[s][0:GLU_HALO, :] = glu_buf[s][tt:tt + GLU_HALO, :]
    sc_buf[0:SC_HALO, :] = sc_buf[tt:tt + SC_HALO, :]
    p_buf[0:POOL_HALO, :] = p_buf[tt:tt + POOL_HALO, :]


def _mix_rows(t, r0, sub, x_ref, g_ref, win_ref, lng_ref, lnb_ref, wcat_ref, bst_ref, cw_ref, cb_ref, clg_ref,
              clb_ref, sw_ref, pw_ref, ps_ref, mog_ref, wout_ref, o_ref,
              glu_buf, yb_buf, sc_buf, p_buf, s2_buf, s4_buf, s8_buf, *, tt):
    x = x_ref[0, r0:r0 + sub, :]
    z = _dot(_rms(x, g_ref[...]).astype(BF16), win_ref[...])
    zcol = lambda i: z[:, i * D_GROUP:(i + 1) * D_GROUP]
    group_norm = lambda g, y: _rms(y, mog_ref[:, g * D_GROUP:(g + 1) * D_GROUP]).astype(BF16)
    yn = [None] * N_GROUPS

    vn = _layer_norm(zcol(1), lng_ref[...], lnb_ref[...])
    head = lax.broadcasted_iota(jnp.int32, (CHUNK, D_GROUP), 1) // GMLP_HEAD_DIM
    gates = []
    for c in range(sub // CHUNK):
        vc = vn[c * CHUNK:(c + 1) * CHUNK]
        stack = jnp.concatenate([jnp.where(head == hh, vc, 0.0) for hh in range(GMLP_HEADS)], axis=0)
        gates.append(_dot(wcat_ref[...], stack.astype(BF16)) + bst_ref[...])
    yn[0] = group_norm(0, zcol(0) * jnp.concatenate(gates, axis=0))

    first = GLU_HALO - (CONF_WIDTH - 1)
    span = SUBLANES * CONV_STRIDE
    glu = zcol(2) * _sigmoid(zcol(3))
    for s in range(LANE_SLABS):
        glu_buf[s][GLU_HALO + r0:GLU_HALO + r0 + sub, :] = glu[:, s * LANES:(s + 1) * LANES]
    for c0 in range(r0, r0 + sub, CONV_ROWS):
        starts = [c0 + (i // CONV_STRIDE) * span + i % CONV_STRIDE for i in range(CONV_ROWS // SUBLANES)]
        accs = [[None] * len(starts) for _ in range(LANE_SLABS)]
        for k in range(CONF_WIDTH):
            for s in range(LANE_SLABS):
                wk = jnp.broadcast_to(cw_ref[k:k + 1, s * LANES:(s + 1) * LANES], (SUBLANES, LANES))
                for i, t0 in enumerate(starts):
                    term = wk * glu_buf[s][pl.ds(first + t0 + k, SUBLANES, stride=CONV_STRIDE), :]
                    accs[s][i] = term if k == 0 else accs[s][i] + term
        conv = jnp.concatenate([jnp.concatenate(a, axis=0) for a in accs], axis=1)
        ln = _layer_norm(conv + cb_ref[...], clg_ref[...], clb_ref[...])
        yb = ln * _sigmoid(ln)
        for s in range(LANE_SLABS):
            for i, t0 in enumerate(starts):
                yb_buf[s][pl.ds(t0, SUBLANES, stride=CONV_STRIDE), :] = (
                    yb[i * SUBLANES:(i + 1) * SUBLANES, s * LANES:(s + 1) * LANES])
    yn[1] = group_norm(1, jnp.concatenate([yb_buf[s][r0:r0 + sub, :] for s in range(LANE_SLABS)], axis=1))

    sxc = zcol(5) * zcol(6)
    lo = SC_HALO + r0
    sc_buf[lo:lo + sub, :] = sxc
    conv_c = (sw_ref[0:1, :] * sc_buf[lo - 2:lo - 2 + sub, :]
              + sw_ref[1:2, :] * sc_buf[lo - 1:lo - 1 + sub, :]
              + sw_ref[2:3, :] * sxc)
    yn[2] = group_norm(2, zcol(4) * conv_c)

    pool_x = zcol(7)
    lo = POOL_HALO + r0
    end = lo + sub
    p_buf[lo:end, :] = pool_x
    lo2, lo4, lo8 = (8, 16, 24) if r0 == 0 else (lo, lo, lo)
    s2_buf[lo2:end, :] = p_buf[lo2:end, :] + p_buf[lo2 - 1:end - 1, :]
    s4_buf[lo4:end, :] = s2_buf[lo4:end, :] + s2_buf[lo4 - 2:end - 2, :]
    s8_buf[lo8:end, :] = s4_buf[lo8:end, :] + s4_buf[lo8 - 4:end - 4, :]
    s16 = s8_buf[lo:end, :] + s8_buf[lo - 8:end - 8, :]
    grp = _pool_window_lanes((sub, D_GROUP))[0]
    ssum = _select_by_group(grp, [s2_buf[lo:end, :], s4_buf[lo:end, :], s8_buf[lo:end, :], s16])
    head_rows = POOL_BUF + 1 if r0 == 0 else 0
    inv_win = _select_by_group(_pool_window_lanes((sub - head_rows, D_GROUP))[0], [1.0 / w for w in POOL_WINDOWS])
    mean = ssum[head_rows:] * inv_win
    if head_rows:
        pos = t * tt + lax.broadcasted_iota(jnp.int32, (head_rows, D_GROUP), 0)
        cnt = jnp.minimum(_pool_window_lanes((head_rows, D_GROUP))[1], (pos + 1).astype(F32))
        mean = jnp.concatenate([ssum[:head_rows] / cnt, mean], axis=0)
    pooled = mean - pool_x
    yn[3] = group_norm(3, _dot(pooled.astype(BF16), pw_ref[...]) * ps_ref[...])

    o_ref[0, r0:r0 + sub, :] = x + _dot(jnp.concatenate(yn, axis=1), wout_ref[...])


def _mixer_prompt(x, lw, *, tt=512, sub=512):
    bsz, seq, _ = x.shape
    n_t = seq // tt
    row = lambda a: a.reshape(1, -1)
    small = [row(lw['norm_mix']), lw['w_in'], row(lw['gmlp_ln_g']), row(lw['gmlp_ln_b']), lw['gmlp_wcat'],
             lw['gmlp_bias_tile'], lw['conf_dw'], row(lw['conf_dw_b']), row(lw['conf_ln_g']), row(lw['conf_ln_b']),
             lw['sc_dw'], lw['pool_w_bd'], row(lw['pool_scale']), row(lw['mix_out_g']), lw['w_out']]
    state_spec = lambda rows: pl.BlockSpec((1, rows, D_GROUP), lambda b, t: (b, 0, 0))
    return pl.pallas_call(
        functools.partial(_mixer_prompt_kernel, tt=tt, n_t=n_t, sub=sub),
        grid=(bsz, n_t),
        in_specs=[pl.BlockSpec((1, tt, D_MODEL), lambda b, t: (b, t, 0))] + [_const_spec(a.shape) for a in small],
        out_specs=[pl.BlockSpec((1, tt, D_MODEL), lambda b, t: (b, t, 0)),
                   state_spec(CONF_WIDTH - 1), state_spec(SC_WIDTH - 1), state_spec(POOL_BUF)],
        out_shape=[jax.ShapeDtypeStruct(x.shape, F32),
                   jax.ShapeDtypeStruct((bsz, CONF_WIDTH - 1, D_GROUP), F32),
                   jax.ShapeDtypeStruct((bsz, SC_WIDTH - 1, D_GROUP), F32),
                   jax.ShapeDtypeStruct((bsz, POOL_BUF, D_GROUP), F32)],
        scratch_shapes=[pltpu.VMEM((GLU_HALO + tt, LANES), F32)] * LANE_SLABS + [pltpu.VMEM((tt, LANES), F32)] * LANE_SLABS
        + [pltpu.VMEM((SC_HALO + tt, D_GROUP), F32)]
        + [pltpu.VMEM((POOL_HALO + tt, D_GROUP), F32)] * 4,
        compiler_params=pltpu.CompilerParams(dimension_semantics=("arbitrary", "arbitrary"),
                                             vmem_limit_bytes=VMEM_LIMIT),
        name="mixer_prompt",
    )(x, *small)


def _mixer_sample_kernel(x_ref, stg_ref, sts_ref, stp_ref, g_ref, win_ref, lng_ref, lnb_ref, w00_ref, b0_ref,
                         cw_ref, cb_ref, clg_ref, clb_ref, sw_ref, pw_ref, ps_ref, mog_ref, wout_ref,
                         o_ref, ovn_ref, oglu_ref, osh_ref, opool_ref):
    x = x_ref[...]
    h = _rms(x, g_ref[...]).astype(BF16)
    z = _dot(h, win_ref[...])
    u, v, glu_a, glu_g, sc_b, sc_c, sc_x, pool_x = [z[:, i * D_GROUP:(i + 1) * D_GROUP] for i in range(8)]
    hist = lambda ref, k: ref[:, k * D_GROUP:(k + 1) * D_GROUP]

    vn = _layer_norm(v, lng_ref[...], lnb_ref[...])
    ovn_ref[...] = vn
    y_a = u * (w00_ref[...] * vn + b0_ref[...])

    glu = glu_a * _sigmoid(glu_g)
    n_hist = CONF_WIDTH - 1
    acc = cw_ref[n_hist:n_hist + 1, :] * glu
    for k in range(n_hist):
        acc = acc + cw_ref[k:k + 1, :] * hist(stg_ref, k)
    ln = _layer_norm(acc + cb_ref[...], clg_ref[...], clb_ref[...])
    y_b = ln * _sigmoid(ln)
    oglu_ref[:, 0:(n_hist - 1) * D_GROUP] = stg_ref[:, D_GROUP:]
    oglu_ref[:, (n_hist - 1) * D_GROUP:] = glu

    sxc = sc_c * sc_x
    y_c = sc_b * (sw_ref[0:1, :] * hist(sts_ref, 0) + sw_ref[1:2, :] * hist(sts_ref, 1) + sw_ref[2:3, :] * sxc)
    osh_ref[:, 0:D_GROUP] = hist(sts_ref, 1)
    osh_ref[:, D_GROUP:] = sxc

    run = pool_x
    sums = []
    back = 0
    for w in POOL_WINDOWS:
        while back < w - 1:
            run = run + hist(stp_ref, POOL_BUF - 1 - back)
            back += 1
        sums.append(run)
    grp, win = _pool_window_lanes(pool_x.shape)
    cnt = jnp.minimum(win, float(PAST_LEN + 1))
    pooled = _select_by_group(grp, sums) / cnt - pool_x
    y_d = _dot(pooled.astype(BF16), pw_ref[...]) * ps_ref[...]
    opool_ref[:, 0:(POOL_BUF - 1) * D_GROUP] = stp_ref[:, D_GROUP:]
    opool_ref[:, (POOL_BUF - 1) * D_GROUP:] = pool_x

    o_ref[...] = _mix_out(x, [y_a, y_b, y_c, y_d], mog_ref, wout_ref)


def _mixer_sample(xs, st_glu, st_sh, st_pool, lw):
    n = xs.shape[0]
    row = lambda a: a.reshape(1, -1)
    args = [xs, st_glu.reshape(n, -1), st_sh.reshape(n, -1), st_pool.reshape(n, -1),
            row(lw['norm_mix']), lw['w_in'], row(lw['gmlp_ln_g']), row(lw['gmlp_ln_b']), row(lw['gmlp_w00']),
            row(lw['gmlp_b0']), lw['conf_dw'], row(lw['conf_dw_b']), row(lw['conf_ln_g']), row(lw['conf_ln_b']),
            lw['sc_dw'], lw['pool_w_bd'], row(lw['pool_scale']), row(lw['mix_out_g']), lw['w_out']]
    out_shapes = [(n, D_MODEL), (n, D_GROUP), (n, (CONF_WIDTH - 1) * D_GROUP), (n, (SC_WIDTH - 1) * D_GROUP),
                  (n, POOL_BUF * D_GROUP)]
    xo, vn, nglu, nsh, npool = pl.pallas_call(
        _mixer_sample_kernel,
        grid=(1,),
        in_specs=[_const_spec(a.shape) for a in args],
        out_specs=[pl.BlockSpec(s, lambda i: (0, 0)) for s in out_shapes],
        out_shape=[jax.ShapeDtypeStruct(s, F32) for s in out_shapes],
        compiler_params=pltpu.CompilerParams(dimension_semantics=("arbitrary",), vmem_limit_bytes=VMEM_LIMIT),
        name="mixer_sample",
    )(*args)
    return (xo, vn, nglu.reshape(n, CONF_WIDTH - 1, D_GROUP), nsh.reshape(n, SC_WIDTH - 1, D_GROUP),
            npool.reshape(n, POOL_BUF, D_GROUP))


def _mem_kv_kernel(m_ref, g_ref, wk_ref, wv_ref, ok_ref, ov_ref, okt_ref, ovb_ref):
    m = _rms(m_ref[0], g_ref[...]).astype(BF16)
    k = _dot(m, wk_ref[...])
    v = _dot(m, wv_ref[...])
    ok_ref[0] = k
    ov_ref[0] = v
    okt_ref[0] = k.T.astype(BF16)
    ovb_ref[0] = v.astype(BF16)


def _mem_kv(mem, lw):
    bsz = mem.shape[0]
    blk = lambda r, c: pl.BlockSpec((1, r, c), lambda b: (b, 0, 0))
    g = lw['norm_mem'].reshape(1, -1)
    return pl.pallas_call(
        _mem_kv_kernel,
        grid=(bsz,),
        in_specs=[blk(MEM_LEN, D_MODEL), _const_spec(g.shape), _const_spec(lw['w_xk'].shape),
                  _const_spec(lw['w_xv'].shape)],
        out_specs=[blk(MEM_LEN, D_MODEL), blk(MEM_LEN, D_MODEL), blk(D_MODEL, MEM_LEN), blk(MEM_LEN, D_MODEL)],
        out_shape=[jax.ShapeDtypeStruct((bsz, MEM_LEN, D_MODEL), F32)] * 2
        + [jax.ShapeDtypeStruct((bsz, D_MODEL, MEM_LEN), BF16), jax.ShapeDtypeStruct((bsz, MEM_LEN, D_MODEL), BF16)],
        compiler_params=pltpu.CompilerParams(dimension_semantics=("arbitrary",), vmem_limit_bytes=VMEM_LIMIT),
        name="mem_kv",
    )(mem, g, lw['w_xk'], lw['w_xv'])


def _xattn_prompt_kernel(x_ref, g_ref, wq_ref, kt_ref, vb_ref, wo_ref, o_ref):
    x = x_ref[0]
    h = _rms(x, g_ref[...]).astype(BF16)
    q = (_dot(h, wq_ref[...]) * (XATTN_HEAD_DIM ** -0.5)).astype(BF16)
    heads = []
    for hh in range(XATTN_HEADS):
        lo, hi = hh * XATTN_HEAD_DIM, (hh + 1) * XATTN_HEAD_DIM
        s = _dot(q[:, lo:hi], kt_ref[0, lo:hi, :])
        e = jnp.exp(s - jnp.max(s, axis=-1, keepdims=True))
        p = e * (1.0 / jnp.sum(e, axis=-1, keepdims=True))
        heads.append(_dot(p.astype(BF16), vb_ref[0, :, lo:hi]).astype(BF16))
    o_ref[0] = x + _dot(jnp.concatenate(heads, axis=1), wo_ref[...])


def _xattn_prompt(x, kt, vb, lw, *, tt=512):
    bsz, seq, _ = x.shape
    g = lw['norm_xattn'].reshape(1, -1)
    return pl.pallas_call(
        _xattn_prompt_kernel,
        grid=(bsz, seq // tt),
        in_specs=[pl.BlockSpec((1, tt, D_MODEL), lambda b, t: (b, t, 0)), _const_spec(g.shape),
                  _const_spec(lw['w_xq'].shape),
                  pl.BlockSpec((1, D_MODEL, MEM_LEN), lambda b, t: (b, 0, 0)),
                  pl.BlockSpec((1, MEM_LEN, D_MODEL), lambda b, t: (b, 0, 0)),
                  _const_spec(lw['w_xo'].shape)],
        out_specs=pl.BlockSpec((1, tt, D_MODEL), lambda b, t: (b, t, 0)),
        out_shape=jax.ShapeDtypeStruct(x.shape, F32),
        compiler_params=pltpu.CompilerParams(dimension_semantics=("arbitrary", "arbitrary"),
                                             vmem_limit_bytes=VMEM_LIMIT),
        name="xattn_prompt",
    )(x, g, lw['w_xq'], kt, vb, lw['w_xo'])


def _norm_proj_kernel(x_ref, g_ref, w_ref, o_ref, *, scale):
    o_ref[...] = _dot(_rms(x_ref[...], g_ref[...]).astype(BF16), w_ref[...]) * scale


def _norm_proj(x, g, w, scale):
    g = g.reshape(1, -1)
    out_shape = (x.shape[0], w.shape[1])
    return pl.pallas_call(
        functools.partial(_norm_proj_kernel, scale=scale),
        grid=(1,),
        in_specs=[_const_spec(x.shape), _const_spec(g.shape), _const_spec(w.shape)],
        out_specs=pl.BlockSpec(out_shape, lambda i: (0, 0)),
        out_shape=jax.ShapeDtypeStruct(out_shape, F32),
        compiler_params=pltpu.CompilerParams(dimension_semantics=("arbitrary",), vmem_limit_bytes=VMEM_LIMIT),
        name="norm_proj",
    )(x, g, w)


def _proj_residual_kernel(x_ref, a_ref, w_ref, o_ref):
    o_ref[...] = x_ref[...] + _dot(a_ref[...].astype(BF16), w_ref[...])


def _proj_residual(x, a, w):
    return pl.pallas_call(
        _proj_residual_kernel,
        grid=(1,),
        in_specs=[_const_spec(x.shape), _const_spec(a.shape), _const_spec(w.shape)],
        out_specs=pl.BlockSpec(x.shape, lambda i: (0, 0)),
        out_shape=jax.ShapeDtypeStruct(x.shape, F32),
        compiler_params=pltpu.CompilerParams(dimension_semantics=("arbitrary",), vmem_limit_bytes=VMEM_LIMIT),
        name="proj_residual",
    )(x, a, w)


def _attend_rows(q_ref, k_ref, v_ref, o_ref, rows):
    for r in rows:
        s = jnp.sum(k_ref[r] * q_ref[r][None], axis=-1, keepdims=True)
        e = jnp.exp(s - jnp.max(s, axis=0, keepdims=True))
        p = e * (1.0 / jnp.sum(e, axis=0, keepdims=True))
        o_ref[r] = jnp.sum(p * v_ref[r], axis=0)


def _ffn_kernel(x_ref, g_ref, w1_ref, w2_ref, gf_ref, o_ref, *, final_norm, between_chunks=None):
    x = x_ref[...]
    h = _rms(x, g_ref[...]).astype(BF16)
    y = x
    for i, c in enumerate(range(0, D_FF, FF_CHUNK)):
        if between_chunks is not None:
            between_chunks(i)
        a = jnp.maximum(_dot(h, w1_ref[:, c:c + FF_CHUNK]), 0.0)
        y = y + _dot((a * a).astype(BF16), w2_ref[c:c + FF_CHUNK, :])
    o_ref[...] = _rms(y, gf_ref[...]) if final_norm else y


def _ffn_attend_kernel(x_ref, g_ref, w1_ref, w2_ref, gf_ref, q_ref, k_ref, v_ref, o_ref, oa_ref, *, final_norm, rows):
    n_chunks = D_FF // FF_CHUNK
    assert rows % n_chunks == 0
    per_chunk = rows // n_chunks
    attend = lambda i: _attend_rows(q_ref, k_ref, v_ref, oa_ref, range(i * per_chunk, (i + 1) * per_chunk))
    _ffn_kernel(x_ref, g_ref, w1_ref, w2_ref, gf_ref, o_ref, final_norm=final_norm, between_chunks=attend)


def _ffn(x2d, lw, norm_final, *, final_norm, tm=512, attend=None):
    n = x2d.shape[0]
    tm = min(tm, n)
    g = lw['norm_ffn'].reshape(1, -1)
    gf = norm_final.reshape(1, -1)
    in_specs = [pl.BlockSpec((tm, D_MODEL), lambda i: (i, 0)), _const_spec(g.shape),
                _const_spec(lw['w_ff1'].shape), _const_spec(lw['w_ff2'].shape), _const_spec(gf.shape)]
    out_spec = pl.BlockSpec((tm, D_MODEL), lambda i: (i, 0))
    out_shape = jax.ShapeDtypeStruct(x2d.shape, F32)
    params = pltpu.CompilerParams(dimension_semantics=("arbitrary",), vmem_limit_bytes=VMEM_LIMIT)
    if attend is None:
        return pl.pallas_call(
            functools.partial(_ffn_kernel, final_norm=final_norm), grid=(n // tm,), in_specs=in_specs,
            out_specs=out_spec, out_shape=out_shape, compiler_params=params, name="ffn",
        )(x2d, g, lw['w_ff1'], lw['w_ff2'], gf)
    q, cache_k, cache_v, layer = attend
    rows = q.shape[0] // (n // tm)
    assert rows * (n // tm) == q.shape[0]
    q_spec = pl.BlockSpec((rows, XATTN_HEADS, XATTN_HEAD_DIM), lambda i: (i, 0, 0))
    kv_spec = pl.BlockSpec((None, rows, MEM_LEN, XATTN_HEADS, XATTN_HEAD_DIM), lambda i: (layer, i, 0, 0, 0))
    return pl.pallas_call(
        functools.partial(_ffn_attend_kernel, final_norm=final_norm, rows=rows), grid=(n // tm,),
        in_specs=in_specs + [q_spec, kv_spec, kv_spec], out_specs=[out_spec, q_spec],
        out_shape=[out_shape, jax.ShapeDtypeStruct(q.shape, F32)], compiler_params=params, name="ffn_attend",
    )(x2d, g, lw['w_ff1'], lw['w_ff2'], gf, q, cache_k, cache_v)


def _layer_weights(l, p):
    tril = jnp.tril(jnp.ones((CHUNK, CHUNK), dtype=bool))
    ws = jnp.where(tril[None], p['gmlp_ws'][l], 0.0)
    pool_bd = jax.scipy.linalg.block_diag(*[p['pool_w'][l, g] for g in range(len(POOL_WINDOWS))])
    lw = {k: p[k][l] for k in ('norm_mix', 'gmlp_ln_g', 'gmlp_ln_b', 'conf_dw', 'conf_dw_b', 'conf_ln_g',
                               'conf_ln_b', 'sc_dw', 'pool_scale', 'mix_out_g', 'norm_xattn', 'norm_mem',
                               'norm_ffn')}
    lw.update(
        w_in=p['w_in'][l].astype(BF16), w_out=p['w_out'][l].astype(BF16),
        gmlp_wcat=jnp.concatenate([ws[h] for h in range(GMLP_HEADS)], axis=1).astype(BF16),
        gmlp_bias_tile=jnp.repeat(p['gmlp_bs'][l].T, GMLP_HEAD_DIM, axis=1),
        gmlp_w00=jnp.repeat(ws[:, 0, 0], GMLP_HEAD_DIM), gmlp_b0=jnp.repeat(p['gmlp_bs'][l][:, 0], GMLP_HEAD_DIM),
        pool_w_bd=pool_bd.astype(BF16),
        w_xq=p['w_xq'][l].astype(BF16), w_xo=p['w_xo'][l].astype(BF16),
        w_xk=p['w_xk'][l].astype(BF16), w_xv=p['w_xv'][l].astype(BF16),
        w_ff1=p['w_ff1'][l].astype(BF16), w_ff2=p['w_ff2'][l].astype(BF16))
    return lw


def kernel(x_prompt, x_sample, mem_prompt, cache_mem_k, cache_mem_v, state_conv_glu, state_conv_short, state_pool, norm_mix, w_in, gmlp_ln_g, gmlp_ln_b, gmlp_ws, gmlp_bs, conf_dw, conf_dw_b, conf_ln_g, conf_ln_b, sc_dw, pool_w, pool_scale, mix_out_g, w_out, norm_xattn, norm_mem, w_xq, w_xk, w_xv, w_xo, norm_ffn, w_ff1, w_ff2, norm_final):
    params = dict(norm_mix=norm_mix, w_in=w_in, gmlp_ln_g=gmlp_ln_g, gmlp_ln_b=gmlp_ln_b, gmlp_ws=gmlp_ws,
                  gmlp_bs=gmlp_bs, conf_dw=conf_dw, conf_dw_b=conf_dw_b, conf_ln_g=conf_ln_g, conf_ln_b=conf_ln_b,
                  sc_dw=sc_dw, pool_w=pool_w, pool_scale=pool_scale, mix_out_g=mix_out_g, w_out=w_out,
                  norm_xattn=norm_xattn, norm_mem=norm_mem, w_xq=w_xq, w_xk=w_xk, w_xv=w_xv, w_xo=w_xo,
                  norm_ffn=norm_ffn, w_ff1=w_ff1, w_ff2=w_ff2)
    depth = w_in.shape[0]
    bsz, seq, _ = x_prompt.shape
    n_s = x_sample.shape[0]
    xp = x_prompt
    xs = x_sample.reshape(n_s, D_MODEL)
    outs = {k: [] for k in ('mk', 'mv', 'glu_p', 'glu_s', 'sh_p', 'sh_s', 'pl_p', 'pl_s', 'v_s')}
    for l in range(depth):
        lw = _layer_weights(l, params)
        last = l == depth - 1
        xp, glu_p, sh_p, pool_p = _mixer_prompt(xp, lw)
        k_p, v_p, kt, vb = _mem_kv(mem_prompt, lw)
        xp = _xattn_prompt(xp, kt, vb, lw)
        outs['mk'].append(k_p.reshape(bsz, MEM_LEN, XATTN_HEADS, XATTN_HEAD_DIM))
        outs['mv'].append(v_p.reshape(bsz, MEM_LEN, XATTN_HEADS, XATTN_HEAD_DIM))
        outs['glu_p'].append(glu_p); outs['sh_p'].append(sh_p); outs['pl_p'].append(pool_p)
        xs, vn_s, glu_s, sh_s, pool_s = _mixer_sample(xs, state_conv_glu[l], state_conv_short[l], state_pool[l], lw)
        q_s = _norm_proj(xs, lw['norm_xattn'], lw['w_xq'], XATTN_HEAD_DIM ** -0.5)
        xp, o_s = _ffn(xp.reshape(bsz * seq, D_MODEL), lw, norm_final, final_norm=last,
                       attend=(q_s.reshape(n_s, XATTN_HEADS, XATTN_HEAD_DIM), cache_mem_k, cache_mem_v, l))
        xp = xp.reshape(bsz, seq, D_MODEL)
        xs = _proj_residual(xs, o_s.reshape(n_s, D_MODEL), lw['w_xo'])
        xs = _ffn(xs, lw, norm_final, final_norm=last)
        outs['glu_s'].append(glu_s); outs['sh_s'].append(sh_s); outs['pl_s'].append(pool_s)
        outs['v_s'].append(vn_s.reshape(n_s, 1, D_GROUP))
    st = lambda k: jnp.stack(outs[k], axis=0)
    return (xp, xs.reshape(n_s, 1, D_MODEL), st('mk'), st('mv'), st('glu_p'), st('glu_s'), st('sh_p'), st('sh_s'),
            st('pl_p'), st('pl_s'), st('v_s'))
```

```python
import functools

import jax
import jax.numpy as jnp
from jax import lax
from jax.experimental import pallas as pl
from jax.experimental.pallas import tpu as pltpu

F32 = jnp.float32
BF16 = jnp.bfloat16

D_MODEL = 1024
D_GROUP = 256
N_GROUPS = 4
D_IN_PROJ = 8 * D_GROUP
GMLP_HEADS = 4
GMLP_HEAD_DIM = D_GROUP // GMLP_HEADS
CHUNK = 128
CONF_WIDTH = 31
SC_WIDTH = 3
POOL_WINDOWS = (2, 4, 8, 16)
POOL_GROUP_DIM = D_GROUP // len(POOL_WINDOWS)
POOL_BUF = max(POOL_WINDOWS) - 1
MEM_LEN = 256
XATTN_HEADS = 4
XATTN_HEAD_DIM = D_MODEL // XATTN_HEADS
D_FF = 4 * D_MODEL
PAST_LEN = 16384
EPS = 1e-6

GLU_HALO = 32
SC_HALO = 8
POOL_HALO = 32
LANES = 128
SUBLANES = 8
LANE_SLABS = D_GROUP // LANES
CONV_STRIDE = 4
CONV_ROWS = 64
FF_CHUNK = 1024
VMEM_LIMIT = 56 * 1024 * 1024


def _rms(x, g):
    return x * lax.rsqrt(jnp.mean(x * x, axis=-1, keepdims=True) + EPS) * g


def _layer_norm(x, g, b):
    xc = x - jnp.mean(x, axis=-1, keepdims=True)
    return xc * lax.rsqrt(jnp.mean(xc * xc, axis=-1, keepdims=True) + EPS) * g + b


def _sigmoid(x):
    return 0.5 * jnp.tanh(0.5 * x) + 0.5


def _dot(a, b):
    return jnp.dot(a, b, preferred_element_type=F32)


def _const_spec(shape):
    zeros = (0,) * len(shape)
    return pl.BlockSpec(shape, lambda *_: zeros, pipeline_mode=pl.Buffered(1))


def _pool_window_lanes(shape):
    grp = lax.broadcasted_iota(jnp.int32, shape, len(shape) - 1) // POOL_GROUP_DIM
    win = jnp.full(shape, float(POOL_WINDOWS[-1]), F32)
    for g in range(len(POOL_WINDOWS) - 2, -1, -1):
        win = jnp.where(grp == g, float(POOL_WINDOWS[g]), win)
    return grp, win


def _select_by_group(grp, vals):
    out = vals[-1]
    for g in range(len(vals) - 2, -1, -1):
        out = jnp.where(grp == g, vals[g], out)
    return out


def _mix_out(x, y_groups, mog_ref, wout_ref):
    yn = [_rms(y, mog_ref[:, g * D_GROUP:(g + 1) * D_GROUP]).astype(BF16) for g, y in enumerate(y_groups)]
    return x + _dot(jnp.concatenate(yn, axis=1), wout_ref[...])


def _mixer_prompt_kernel(x_ref, g_ref, win_ref, lng_ref, lnb_ref, wcat_ref, bst_ref, cw_ref, cb_ref, clg_ref,
                         clb_ref, sw_ref, pw_ref, ps_ref, mog_ref, wout_ref,
                         o_ref, oglu_ref, osh_ref, opool_ref,
                         glu_buf0, glu_buf1, yb_buf0, yb_buf1, sc_buf, p_buf, s2_buf, s4_buf, s8_buf, *, tt, n_t, sub):
    t = pl.program_id(1)
    glu_buf = (glu_buf0, glu_buf1)
    yb_buf = (yb_buf0, yb_buf1)

    @pl.when(t == 0)
    def _():
        for s in range(LANE_SLABS):
            glu_buf[s][0:GLU_HALO, :] = jnp.zeros((GLU_HALO, LANES), F32)
        sc_buf[0:SC_HALO, :] = jnp.zeros((SC_HALO, D_GROUP), F32)
        p_buf[0:POOL_HALO, :] = jnp.zeros((POOL_HALO, D_GROUP), F32)

    for r0 in range(0, tt, sub):
        _mix_rows(t, r0, sub, x_ref, g_ref, win_ref, lng_ref, lnb_ref, wcat_ref, bst_ref, cw_ref, cb_ref, clg_ref,
                  clb_ref, sw_ref, pw_ref, ps_ref, mog_ref, wout_ref, o_ref,
                  glu_buf, yb_buf, sc_buf, p_buf, s2_buf, s4_buf, s8_buf, tt=tt)

    @pl.when(t == n_t - 1)
    def _():
        oglu_ref[0] = jnp.concatenate(
            [glu_buf[s][GLU_HALO + tt - (CONF_WIDTH - 1):GLU_HALO + tt, :] for s in range(LANE_SLABS)], axis=1)
        osh_ref[0] = sc_buf[SC_HALO + tt - (SC_WIDTH - 1):SC_HALO + tt, :]
        opool_ref[0] = p_buf[POOL_HALO + tt - POOL_BUF:POOL_HALO + tt, :]

    for s in range(LANE_SLABS):
        glu_buf[s][0:GLU_HALO, :] = glu_buf[s][tt:tt + GLU_HALO, :]
    sc_buf[0:SC_HALO, :] = sc_buf[tt:tt + SC_HALO, :]
    p_buf[0:POOL_HALO, :] = p_buf[tt:tt + POOL_HALO, :]


def _mix_rows(t, r0, sub, x_ref, g_ref, win_ref, lng_ref, lnb_ref, wcat_ref, bst_ref, cw_ref, cb_ref, clg_ref,
              clb_ref, sw_ref, pw_ref, ps_ref, mog_ref, wout_ref, o_ref,
              glu_buf, yb_buf, sc_buf, p_buf, s2_buf, s4_buf, s8_buf, *, tt):
    x = x_ref[0, r0:r0 + sub, :]
    z = _dot(_rms(x, g_ref[...]).astype(BF16), win_ref[...])
    zcol = lambda i: z[:, i * D_GROUP:(i + 1) * D_GROUP]
    group_norm = lambda g, y: _rms(y, mog_ref[:, g * D_GROUP:(g + 1) * D_GROUP]).astype(BF16)
    yn = [None] * N_GROUPS

    vn = _layer_norm(zcol(1), lng_ref[...], lnb_ref[...])
    head = lax.broadcasted_iota(jnp.int32, (CHUNK, D_GROUP), 1) // GMLP_HEAD_DIM
    gates = []
    for c in range(sub // CHUNK):
        vc = vn[c * CHUNK:(c + 1) * CHUNK]
        stack = jnp.concatenate([jnp.where(head == hh, vc, 0.0) for hh in range(GMLP_HEADS)], axis=0)
        gates.append(_dot(wcat_ref[...], stack.astype(BF16)) + bst_ref[...])
    yn[0] = group_norm(0, zcol(0) * jnp.concatenate(gates, axis=0))

    first = GLU_HALO - (CONF_WIDTH - 1)
    span = SUBLANES * CONV_STRIDE
    glu = zcol(2) * _sigmoid(zcol(3))
    for s in range(LANE_SLABS):
        glu_buf[s][GLU_HALO + r0:GLU_HALO + r0 + sub, :] = glu[:, s * LANES:(s + 1) * LANES]
    for c0 in range(r0, r0 + sub, CONV_ROWS):
        starts = [c0 + (i // CONV_STRIDE) * span + i % CONV_STRIDE for i in range(CONV_ROWS // SUBLANES)]
        accs = [[None] * len(starts) for _ in range(LANE_SLABS)]
        for k in range(CONF_WIDTH):
            for s in range(LANE_SLABS):
                wk = jnp.broadcast_to(cw_ref[k:k + 1, s * LANES:(s + 1) * LANES], (SUBLANES, LANES))
                for i, t0 in enumerate(starts):
                    term = wk * glu_buf[s][pl.ds(first + t0 + k, SUBLANES, stride=CONV_STRIDE), :]
                    accs[s][i] = term if k == 0 else accs[s][i] + term
        conv = jnp.concatenate([jnp.concatenate(a, axis=0) for a in accs], axis=1)
        ln = _layer_norm(conv + cb_ref[...], clg_ref[...], clb_ref[...])
        yb = ln * _sigmoid(ln)
        for s in range(LANE_SLABS):
            for i, t0 in enumerate(starts):
                yb_buf[s][pl.ds(t0, SUBLANES, stride=CONV_STRIDE), :] = (
                    yb[i * SUBLANES:(i + 1) * SUBLANES, s * LANES:(s + 1) * LANES])
    yn[1] = group_norm(1, jnp.concatenate([yb_buf[s][r0:r0 + sub, :] for s in range(LANE_SLABS)], axis=1))

    sxc = zcol(5) * zcol(6)
    lo = SC_HALO + r0
    sc_buf[lo:lo + sub, :] = sxc
    conv_c = (sw_ref[0:1, :] * sc_buf[lo - 2:lo - 2 + sub, :]
              + sw_ref[1:2, :] * sc_buf[lo - 1:lo - 1 + sub, :]
              + sw_ref[2:3, :] * sxc)
    yn[2] = group_norm(2, zcol(4) * conv_c)

    pool_x = zcol(7)
    lo = POOL_HALO + r0
    end = lo + sub
    p_buf[lo:end, :] = pool_x
    lo2, lo4, lo8 = (8, 16, 24) if r0 == 0 else (lo, lo, lo)
    s2_buf[lo2:end, :] = p_buf[lo2:end, :] + p_buf[lo2 - 1:end - 1, :]
    s4_buf[lo4:end, :] = s2_buf[lo4:end, :] + s2_buf[lo4 - 2:end - 2, :]
    s8_buf[lo8:end, :] = s4_buf[lo8:end, :] + s4_buf[lo8 - 4:end - 4, :]
    s16 = s8_buf[lo:end, :] + s8_buf[lo - 8:end - 8, :]
    grp = _pool_window_lanes((sub, D_GROUP))[0]
    ssum = _select_by_group(grp, [s2_buf[lo:end, :], s4_buf[lo:end, :], s8_buf[lo:end, :], s16])
    head_rows = POOL_BUF + 1 if r0 == 0 else 0
    inv_win = _select_by_group(_pool_window_lanes((sub - head_rows, D_GROUP))[0], [1.0 / w for w in POOL_WINDOWS])
    mean = ssum[head_rows:] * inv_win
    if head_rows:
        pos = t * tt + lax.broadcasted_iota(jnp.int32, (head_rows, D_GROUP), 0)
        cnt = jnp.minimum(_pool_window_lanes((head_rows, D_GROUP))[1], (pos + 1).astype(F32))
        mean = jnp.concatenate([ssum[:head_rows] / cnt, mean], axis=0)
    pooled = mean - pool_x
    yn[3] = group_norm(3, _dot(pooled.astype(BF16), pw_ref[...]) * ps_ref[...])

    o_ref[0, r0:r0 + sub, :] = x + _dot(jnp.concatenate(yn, axis=1), wout_ref[...])


def _mixer_prompt(x, lw, *, tt=512, sub=512):
    bsz, seq, _ = x.shape
    n_t = seq // tt
    row = lambda a: a.reshape(1, -1)
    small = [row(lw['norm_mix']), lw['w_in'], row(lw['gmlp_ln_g']), row(lw['gmlp_ln_b']), lw['gmlp_wcat'],
             lw['gmlp_bias_tile'], lw['conf_dw'], row(lw['conf_dw_b']), row(lw['conf_ln_g']), row(lw['conf_ln_b']),
             lw['sc_dw'], lw['pool_w_bd'], row(lw['pool_scale']), row(lw['mix_out_g']), lw['w_out']]
    state_spec = lambda rows: pl.BlockSpec((1, rows, D_GROUP), lambda b, t: (b, 0, 0))
    return pl.pallas_call(
        functools.partial(_mixer_prompt_kernel, tt=tt, n_t=n_t, sub=sub),
        grid=(bsz, n_t),
        in_specs=[pl.BlockSpec((1, tt, D_MODEL), lambda b, t: (b, t, 0))] + [_const_spec(a.shape) for a in small],
        out_specs=[pl.BlockSpec((1, tt, D_MODEL), lambda b, t: (b, t, 0)),
                   state_spec(CONF_WIDTH - 1), state_spec(SC_WIDTH - 1), state_spec(POOL_BUF)],
        out_shape=[jax.ShapeDtypeStruct(x.shape, F32),
                   jax.ShapeDtypeStruct((bsz, CONF_WIDTH - 1, D_GROUP), F32),
                   jax.ShapeDtypeStruct((bsz, SC_WIDTH - 1, D_GROUP), F32),
                   jax.ShapeDtypeStruct((bsz, POOL_BUF, D_GROUP), F32)],
        scratch_shapes=[pltpu.VMEM((GLU_HALO + tt, LANES), F32)] * LANE_SLABS + [pltpu.VMEM((tt, LANES), F32)] * LANE_SLABS
        + [pltpu.VMEM((SC_HALO + tt, D_GROUP), F32)]
        + [pltpu.VMEM((POOL_HALO + tt, D_GROUP), F32)] * 4,
        compiler_params=pltpu.CompilerParams(dimension_semantics=("arbitrary", "arbitrary"),
                                             vmem_limit_bytes=VMEM_LIMIT),
        name="mixer_prompt",
    )(x, *small)


def _mixer_sample_kernel(x_ref, stg_ref, sts_ref, stp_ref, g_ref, win_ref, lng_ref, lnb_ref, w00_ref, b0_ref,
                         cw_ref, cb_ref, clg_ref, clb_ref, sw_ref, pw_ref, ps_ref, mog_ref, wout_ref,
                         o_ref, ovn_ref, oglu_ref, osh_ref, opool_ref):
    x = x_ref[...]
    h = _rms(x, g_ref[...]).astype(BF16)
    z = _dot(h, win_ref[...])
    u, v, glu_a, glu_g, sc_b, sc_c, sc_x, pool_x = [z[:, i * D_GROUP:(i + 1) * D_GROUP] for i in range(8)]
    hist = lambda ref, k: ref[:, k * D_GROUP:(k + 1) * D_GROUP]

    vn = _layer_norm(v, lng_ref[...], lnb_ref[...])
    ovn_ref[...] = vn
    y_a = u * (w00_ref[...] * vn + b0_ref[...])

    glu = glu_a * _sigmoid(glu_g)
    n_hist = CONF_WIDTH - 1
    acc = cw_ref[n_hist:n_hist + 1, :] * glu
    for k in range(n_hist):
        acc = acc + cw_ref[k:k + 1, :] * hist(stg_ref, k)
    ln = _layer_norm(acc + cb_ref[...], clg_ref[...], clb_ref[...])
    y_b = ln * _sigmoid(ln)
    oglu_ref[:, 0:(n_hist - 1) * D_GROUP] = stg_ref[:, D_GROUP:]
    oglu_ref[:, (n_hist - 1) * D_GROUP:] = glu

    sxc = sc_c * sc_x
    y_c = sc_b * (sw_ref[0:1, :] * hist(sts_ref, 0) + sw_ref[1:2, :] * hist(sts_ref, 1) + sw_ref[2:3, :] * sxc)
    osh_ref[:, 0:D_GROUP] = hist(sts_ref, 1)
    osh_ref[:, D_GROUP:] = sxc

    run = pool_x
    sums = []
    back = 0
    for w in POOL_WINDOWS:
        while back < w - 1:
            run = run + hist(stp_ref, POOL_BUF - 1 - back)
            back += 1
        sums.append(run)
    grp, win = _pool_window_lanes(pool_x.shape)
    cnt = jnp.minimum(win, float(PAST_LEN + 1))
    pooled = _select_by_group(grp, sums) / cnt - pool_x
    y_d = _dot(pooled.astype(BF16), pw_ref[...]) * ps_ref[...]
    opool_ref[:, 0:(POOL_BUF - 1) * D_GROUP] = stp_ref[:, D_GROUP:]
    opool_ref[:, (POOL_BUF - 1) * D_GROUP:] = pool_x

    o_ref[...] = _mix_out(x, [y_a, y_b, y_c, y_d], mog_ref, wout_ref)


def _mixer_sample(xs, st_glu, st_sh, st_pool, lw):
    n = xs.shape[0]
    row = lambda a: a.reshape(1, -1)
    args = [xs, st_glu.reshape(n, -1), st_sh.reshape(n, -1), st_pool.reshape(n, -1),
            row(lw['norm_mix']), lw['w_in'], row(lw['gmlp_ln_g']), row(lw['gmlp_ln_b']), row(lw['gmlp_w00']),
            row(lw['gmlp_b0']), lw['conf_dw'], row(lw['conf_dw_b']), row(lw['conf_ln_g']), row(lw['conf_ln_b']),
            lw['sc_dw'], lw['pool_w_bd'], row(lw['pool_scale']), row(lw['mix_out_g']), lw['w_out']]
    out_shapes = [(n, D_MODEL), (n, D_GROUP), (n, (CONF_WIDTH - 1) * D_GROUP), (n, (SC_WIDTH - 1) * D_GROUP),
                  (n, POOL_BUF * D_GROUP)]
    xo, vn, nglu, nsh, npool = pl.pallas_call(
        _mixer_sample_kernel,
        grid=(1,),
        in_specs=[_const_spec(a.shape) for a in args],
        out_specs=[pl.BlockSpec(s, lambda i: (0, 0)) for s in out_shapes],
        out_shape=[jax.ShapeDtypeStruct(s, F32) for s in out_shapes],
        compiler_params=pltpu.CompilerParams(dimension_semantics=("arbitrary",), vmem_limit_bytes=VMEM_LIMIT),
        name="mixer_sample",
    )(*args)
    return (xo, vn, nglu.reshape(n, CONF_WIDTH - 1, D_GROUP), nsh.reshape(n, SC_WIDTH - 1, D_GROUP),
            npool.reshape(n, POOL_BUF, D_GROUP))


def _mem_kv_kernel(m_ref, g_ref, wk_ref, wv_ref, ok_ref, ov_ref, okt_ref, ovb_ref):
    m = _rms(m_ref[0], g_ref[...]).astype(BF16)
    k = _dot(m, wk_ref[...])
    v = _dot(m, wv_ref[...])
    ok_ref[0] = k
    ov_ref[0] = v
    okt_ref[0] = k.T.astype(BF16)
    ovb_ref[0] = v.astype(BF16)


def _mem_kv(mem, lw):
    bsz = mem.shape[0]
    blk = lambda r, c: pl.BlockSpec((1, r, c), lambda b: (b, 0, 0))
    g = lw['norm_mem'].reshape(1, -1)
    return pl.pallas_call(
        _mem_kv_kernel,
        grid=(bsz,),
        in_specs=[blk(MEM_LEN, D_MODEL), _const_spec(g.shape), _const_spec(lw['w_xk'].shape),
                  _const_spec(lw['w_xv'].shape)],
        out_specs=[blk(MEM_LEN, D_MODEL), blk(MEM_LEN, D_MODEL), blk(D_MODEL, MEM_LEN), blk(MEM_LEN, D_MODEL)],
        out_shape=[jax.ShapeDtypeStruct((bsz, MEM_LEN, D_MODEL), F32)] * 2
        + [jax.ShapeDtypeStruct((bsz, D_MODEL, MEM_LEN), BF16), jax.ShapeDtypeStruct((bsz, MEM_LEN, D_MODEL), BF16)],
        compiler_params=pltpu.CompilerParams(dimension_semantics=("arbitrary",), vmem_limit_bytes=VMEM_LIMIT),
        name="mem_kv",
    )(mem, g, lw['w_xk'], lw['w_xv'])


def _xattn_prompt_kernel(x_ref, g_ref, wq_ref, kt_ref, vb_ref, wo_ref, o_ref):
    x = x_ref[0]
    h = _rms(x, g_ref[...]).astype(BF16)
    q = (_dot(h, wq_ref[...]) * (XATTN_HEAD_DIM ** -0.5)).astype(BF16)
    heads = []
    for hh in range(XATTN_HEADS):
        lo, hi = hh * XATTN_HEAD_DIM, (hh + 1) * XATTN_HEAD_DIM
        s = _dot(q[:, lo:hi], kt_ref[0, lo:hi, :])
        e = jnp.exp(s - jnp.max(s, axis=-1, keepdims=True))
        p = e * (1.0 / jnp.sum(e, axis=-1, keepdims=True))
        heads.append(_dot(p.astype(BF16), vb_ref[0, :, lo:hi]).astype(BF16))
    o_ref[0] = x + _dot(jnp.concatenate(heads, axis=1), wo_ref[...])


def _xattn_prompt(x, kt, vb, lw, *, tt=512):
    bsz, seq, _ = x.shape
    g = lw['norm_xattn'].reshape(1, -1)
    return pl.pallas_call(
        _xattn_prompt_kernel,
        grid=(bsz, seq // tt),
        in_specs=[pl.BlockSpec((1, tt, D_MODEL), lambda b, t: (b, t, 0)), _const_spec(g.shape),
                  _const_spec(lw['w_xq'].shape),
                  pl.BlockSpec((1, D_MODEL, MEM_LEN), lambda b, t: (b, 0, 0)),
                  pl.BlockSpec((1, MEM_LEN, D_MODEL), lambda b, t: (b, 0, 0)),
                  _const_spec(lw['w_xo'].shape)],
        out_specs=pl.BlockSpec((1, tt, D_MODEL), lambda b, t: (b, t, 0)),
        out_shape=jax.ShapeDtypeStruct(x.shape, F32),
        compiler_params=pltpu.CompilerParams(dimension_semantics=("arbitrary", "arbitrary"),
                                             vmem_limit_bytes=VMEM_LIMIT),
        name="xattn_prompt",
    )(x, g, lw['w_xq'], kt, vb, lw['w_xo'])


def _norm_proj_kernel(x_ref, g_ref, w_ref, o_ref, *, scale):
    o_ref[...] = _dot(_rms(x_ref[...], g_ref[...]).astype(BF16), w_ref[...]) * scale


def _norm_proj(x, g, w, scale):
    g = g.reshape(1, -1)
    out_shape = (x.shape[0], w.shape[1])
    return pl.pallas_call(
        functools.partial(_norm_proj_kernel, scale=scale),
        grid=(1,),
        in_specs=[_const_spec(x.shape), _const_spec(g.shape), _const_spec(w.shape)],
        out_specs=pl.BlockSpec(out_shape, lambda i: (0, 0)),
        out_shape=jax.ShapeDtypeStruct(out_shape, F32),
        compiler_params=pltpu.CompilerParams(dimension_semantics=("arbitrary",), vmem_limit_bytes=VMEM_LIMIT),
        name="norm_proj",
    )(x, g, w)


def _proj_residual_kernel(x_ref, a_ref, w_ref, o_ref):
    o_ref[...] = x_ref[...] + _dot(a_ref[...].astype(BF16), w_ref[...])


def _proj_residual(x, a, w):
    return pl.pallas_call(
        _proj_residual_kernel,
        grid=(1,),
        in_specs=[_const_spec(x.shape), _const_spec(a.shape), _const_spec(w.shape)],
        out_specs=pl.BlockSpec(x.shape, lambda i: (0, 0)),
        out_shape=jax.ShapeDtypeStruct(x.shape, F32),
        compiler_params=pltpu.CompilerParams(dimension_semantics=("arbitrary",), vmem_limit_bytes=VMEM_LIMIT),
        name="proj_residual",
    )(x, a, w)


def _split_head_dim(a):
    lead = a.shape[:-2]
    a = a.reshape(*lead, XATTN_HEADS, 2, XATTN_HEAD_DIM // 2)
    return jnp.swapaxes(a, -3, -2).reshape(*lead, 2 * XATTN_HEADS, XATTN_HEAD_DIM // 2)


def _merge_head_dim(a):
    lead = a.shape[:-2]
    a = a.reshape(*lead, 2, XATTN_HEADS, XATTN_HEAD_DIM // 2)
    return jnp.swapaxes(a, -3, -2).reshape(*lead, XATTN_HEADS, XATTN_HEAD_DIM)


def _attend_rows(q_ref, k_ref, v_ref, o_ref, rows):
    for r in rows:
        part = jnp.sum(k_ref[r] * q_ref[r][None], axis=-1, keepdims=True)
        s = part + pltpu.roll(part, XATTN_HEADS, axis=1)
        e = jnp.exp(s - jnp.max(s, axis=0, keepdims=True))
        p = e * (1.0 / jnp.sum(e, axis=0, keepdims=True))
        o_ref[r] = jnp.sum(p * v_ref[r], axis=0)


def _ffn_kernel(x_ref, g_ref, w1_ref, w2_ref, gf_ref, o_ref, *, final_norm, between_chunks=None):
    x = x_ref[...]
    h = _rms(x, g_ref[...]).astype(BF16)
    y = x
    for i, c in enumerate(range(0, D_FF, FF_CHUNK)):
        if between_chunks is not None:
            between_chunks(i)
        a = jnp.maximum(_dot(h, w1_ref[:, c:c + FF_CHUNK]), 0.0)
        y = y + _dot((a * a).astype(BF16), w2_ref[c:c + FF_CHUNK, :])
    o_ref[...] = _rms(y, gf_ref[...]) if final_norm else y


def _ffn_attend_kernel(x_ref, g_ref, w1_ref, w2_ref, gf_ref, q_ref, k_ref, v_ref, o_ref, oa_ref, *, final_norm, rows):
    n_chunks = D_FF // FF_CHUNK
    assert rows % n_chunks == 0
    per_chunk = rows // n_chunks
    attend = lambda i: _attend_rows(q_ref, k_ref, v_ref, oa_ref, range(i * per_chunk, (i + 1) * per_chunk))
    _ffn_kernel(x_ref, g_ref, w1_ref, w2_ref, gf_ref, o_ref, final_norm=final_norm, between_chunks=attend)


def _ffn(x2d, lw, norm_final, *, final_norm, tm=512, attend=None):
    n = x2d.shape[0]
    tm = min(tm, n)
    g = lw['norm_ffn'].reshape(1, -1)
    gf = norm_final.reshape(1, -1)
    in_specs = [pl.BlockSpec((tm, D_MODEL), lambda i: (i, 0)), _const_spec(g.shape),
                _const_spec(lw['w_ff1'].shape), _const_spec(lw['w_ff2'].shape), _const_spec(gf.shape)]
    out_spec = pl.BlockSpec((tm, D_MODEL), lambda i: (i, 0))
    out_shape = jax.ShapeDtypeStruct(x2d.shape, F32)
    params = pltpu.CompilerParams(dimension_semantics=("arbitrary",), vmem_limit_bytes=VMEM_LIMIT)
    if attend is None:
        return pl.pallas_call(
            functools.partial(_ffn_kernel, final_norm=final_norm), grid=(n // tm,), in_specs=in_specs,
            out_specs=out_spec, out_shape=out_shape, compiler_params=params, name="ffn",
        )(x2d, g, lw['w_ff1'], lw['w_ff2'], gf)
    q, cache_k, cache_v, layer = attend
    rows = q.shape[0] // (n // tm)
    assert rows * (n // tm) == q.shape[0]
    q_spec = pl.BlockSpec((rows,) + q.shape[1:], lambda i: (i, 0, 0))
    kv_spec = pl.BlockSpec((None, rows) + cache_k.shape[2:], lambda i: (layer, i, 0, 0, 0))
    return pl.pallas_call(
        functools.partial(_ffn_attend_kernel, final_norm=final_norm, rows=rows), grid=(n // tm,),
        in_specs=in_specs + [q_spec, kv_spec, kv_spec], out_specs=[out_spec, q_spec],
        out_shape=[out_shape, jax.ShapeDtypeStruct(q.shape, F32)], compiler_params=params, name="ffn_attend",
    )(x2d, g, lw['w_ff1'], lw['w_ff2'], gf, q, cache_k, cache_v)


def _layer_weights(l, p):
    tril = jnp.tril(jnp.ones((CHUNK, CHUNK), dtype=bool))
    ws = jnp.where(tril[None], p['gmlp_ws'][l], 0.0)
    pool_bd = jax.scipy.linalg.block_diag(*[p['pool_w'][l, g] for g in range(len(POOL_WINDOWS))])
    lw = {k: p[k][l] for k in ('norm_mix', 'gmlp_ln_g', 'gmlp_ln_b', 'conf_dw', 'conf_dw_b', 'conf_ln_g',
                               'conf_ln_b', 'sc_dw', 'pool_scale', 'mix_out_g', 'norm_xattn', 'norm_mem',
                               'norm_ffn')}
    lw.update(
        w_in=p['w_in'][l].astype(BF16), w_out=p['w_out'][l].astype(BF16),
        gmlp_wcat=jnp.concatenate([ws[h] for h in range(GMLP_HEADS)], axis=1).astype(BF16),
        gmlp_bias_tile=jnp.repeat(p['gmlp_bs'][l].T, GMLP_HEAD_DIM, axis=1),
        gmlp_w00=jnp.repeat(ws[:, 0, 0], GMLP_HEAD_DIM), gmlp_b0=jnp.repeat(p['gmlp_bs'][l][:, 0], GMLP_HEAD_DIM),
        pool_w_bd=pool_bd.astype(BF16),
        w_xq=p['w_xq'][l].astype(BF16), w_xo=p['w_xo'][l].astype(BF16),
        w_xk=p['w_xk'][l].astype(BF16), w_xv=p['w_xv'][l].astype(BF16),
        w_ff1=p['w_ff1'][l].astype(BF16), w_ff2=p['w_ff2'][l].astype(BF16))
    return lw


def kernel(x_prompt, x_sample, mem_prompt, cache_mem_k, cache_mem_v, state_conv_glu, state_conv_short, state_pool, norm_mix, w_in, gmlp_ln_g, gmlp_ln_b, gmlp_ws, gmlp_bs, conf_dw, conf_dw_b, conf_ln_g, conf_ln_b, sc_dw, pool_w, pool_scale, mix_out_g, w_out, norm_xattn, norm_mem, w_xq, w_xk, w_xv, w_xo, norm_ffn, w_ff1, w_ff2, norm_final):
    params = dict(norm_mix=norm_mix, w_in=w_in, gmlp_ln_g=gmlp_ln_g, gmlp_ln_b=gmlp_ln_b, gmlp_ws=gmlp_ws,
                  gmlp_bs=gmlp_bs, conf_dw=conf_dw, conf_dw_b=conf_dw_b, conf_ln_g=conf_ln_g, conf_ln_b=conf_ln_b,
                  sc_dw=sc_dw, pool_w=pool_w, pool_scale=pool_scale, mix_out_g=mix_out_g, w_out=w_out,
                  norm_xattn=norm_xattn, norm_mem=norm_mem, w_xq=w_xq, w_xk=w_xk, w_xv=w_xv, w_xo=w_xo,
                  norm_ffn=norm_ffn, w_ff1=w_ff1, w_ff2=w_ff2)
    depth = w_in.shape[0]
    bsz, seq, _ = x_prompt.shape
    n_s = x_sample.shape[0]
    xp = x_prompt
    xs = x_sample.reshape(n_s, D_MODEL)
    outs = {k: [] for k in ('mk', 'mv', 'glu_p', 'glu_s', 'sh_p', 'sh_s', 'pl_p', 'pl_s', 'v_s')}
    cache_k_split = _split_head_dim(cache_mem_k)
    cache_v_split = _split_head_dim(cache_mem_v)
    for l in range(depth):
        lw = _layer_weights(l, params)
        last = l == depth - 1
        xp, glu_p, sh_p, pool_p = _mixer_prompt(xp, lw)
        k_p, v_p, kt, vb = _mem_kv(mem_prompt, lw)
        xp = _xattn_prompt(xp, kt, vb, lw)
        outs['mk'].append(k_p.reshape(bsz, MEM_LEN, XATTN_HEADS, XATTN_HEAD_DIM))
        outs['mv'].append(v_p.reshape(bsz, MEM_LEN, XATTN_HEADS, XATTN_HEAD_DIM))
        outs['glu_p'].append(glu_p); outs['sh_p'].append(sh_p); outs['pl_p'].append(pool_p)
        xs, vn_s, glu_s, sh_s, pool_s = _mixer_sample(xs, state_conv_glu[l], state_conv_short[l], state_pool[l], lw)
        q_s = _norm_proj(xs, lw['norm_xattn'], lw['w_xq'], XATTN_HEAD_DIM ** -0.5)
        q_s = _split_head_dim(q_s.reshape(n_s, XATTN_HEADS, XATTN_HEAD_DIM))
        xp, o_s = _ffn(xp.reshape(bsz * seq, D_MODEL), lw, norm_final, final_norm=last,
                       attend=(q_s, cache_k_split, cache_v_split, l))
        xp = xp.reshape(bsz, seq, D_MODEL)
        xs = _proj_residual(xs, _merge_head_dim(o_s).reshape(n_s, D_MODEL), lw['w_xo'])
        xs = _ffn(xs, lw, norm_final, final_norm=last)
        outs['glu_s'].append(glu_s); outs['sh_s'].append(sh_s); outs['pl_s'].append(pool_s)
        outs['v_s'].append(vn_s.reshape(n_s, 1, D_GROUP))
    st = lambda k: jnp.stack(outs[k], axis=0)
    return (xp, xs.reshape(n_s, 1, D_MODEL), st('mk'), st('mv'), st('glu_p'), st('glu_s'), st('sh_p'), st('sh_s'),
            st('pl_p'), st('pl_s'), st('v_s'))
```

```python
import functools

import jax
import jax.numpy as jnp
from jax import lax
from jax.experimental import pallas as pl
from jax.experimental.pallas import tpu as pltpu

F32 = jnp.float32
BF16 = jnp.bfloat16

D_MODEL = 1024
D_GROUP = 256
N_GROUPS = 4
D_IN_PROJ = 8 * D_GROUP
GMLP_HEADS = 4
GMLP_HEAD_DIM = D_GROUP // GMLP_HEADS
CHUNK = 128
CONF_WIDTH = 31
SC_WIDTH = 3
POOL_WINDOWS = (2, 4, 8, 16)
POOL_GROUP_DIM = D_GROUP // len(POOL_WINDOWS)
POOL_BUF = max(POOL_WINDOWS) - 1
MEM_LEN = 256
XATTN_HEADS = 4
XATTN_HEAD_DIM = D_MODEL // XATTN_HEADS
D_FF = 4 * D_MODEL
PAST_LEN = 16384
EPS = 1e-6

GLU_HALO = 32
SC_HALO = 8
POOL_HALO = 32
LANES = 128
SUBLANES = 8
BF16_SUBLANES = 16
LANE_SLABS = D_GROUP // LANES
CONV_STRIDE = 4
CONV_ROWS = 64
FF_CHUNK = 1024
VMEM_LIMIT = 56 * 1024 * 1024


def _rms(x, g):
    return x * lax.rsqrt(jnp.mean(x * x, axis=-1, keepdims=True) + EPS) * g


def _layer_norm(x, g, b):
    xc = x - jnp.mean(x, axis=-1, keepdims=True)
    return xc * lax.rsqrt(jnp.mean(xc * xc, axis=-1, keepdims=True) + EPS) * g + b


def _sigmoid(x):
    return 0.5 * jnp.tanh(0.5 * x) + 0.5


def _dot(a, b):
    return jnp.dot(a, b, preferred_element_type=F32)


def _const_spec(shape):
    zeros = (0,) * len(shape)
    return pl.BlockSpec(shape, lambda *_: zeros, pipeline_mode=pl.Buffered(1))


def _cast_job(stacked, layer, steps, step_of):
    _, r, c = stacked.shape
    blk = r // steps
    assert blk * steps == r and blk % BF16_SUBLANES == 0
    return (pl.BlockSpec((None, blk, c), lambda *ids: (layer, step_of(*ids), 0)),
            pl.BlockSpec((blk, c), lambda *ids: (step_of(*ids), 0)),
            jax.ShapeDtypeStruct((r, c), BF16))


def _run_cast_jobs(src_refs, dst_refs):
    for src, dst in zip(src_refs, dst_refs):
        dst[...] = src[...].astype(BF16)


def _pool_window_lanes(shape):
    grp = lax.broadcasted_iota(jnp.int32, shape, len(shape) - 1) // POOL_GROUP_DIM
    win = jnp.full(shape, float(POOL_WINDOWS[-1]), F32)
    for g in range(len(POOL_WINDOWS) - 2, -1, -1):
        win = jnp.where(grp == g, float(POOL_WINDOWS[g]), win)
    return grp, win


def _select_by_group(grp, vals):
    out = vals[-1]
    for g in range(len(vals) - 2, -1, -1):
        out = jnp.where(grp == g, vals[g], out)
    return out


def _mix_out(x, y_groups, mog_ref, wout_ref):
    yn = [_rms(y, mog_ref[:, g * D_GROUP:(g + 1) * D_GROUP]).astype(BF16) for g, y in enumerate(y_groups)]
    return x + _dot(jnp.concatenate(yn, axis=1), wout_ref[...])


MIXER_INPUTS = 16
MIXER_OUTPUTS = 4


def _mixer_prompt_kernel(*refs, tt, n_t, sub, n_cast):
    (x_ref, g_ref, win_ref, lng_ref, lnb_ref, wcat_ref, bst_ref, cw_ref, cb_ref, clg_ref,
     clb_ref, sw_ref, pw_ref, ps_ref, mog_ref, wout_ref) = refs[:MIXER_INPUTS]
    cast_src = refs[MIXER_INPUTS:MIXER_INPUTS + n_cast]
    outs = refs[MIXER_INPUTS + n_cast:]
    o_ref, oglu_ref, osh_ref, opool_ref = outs[:MIXER_OUTPUTS]
    cast_dst = outs[MIXER_OUTPUTS:MIXER_OUTPUTS + n_cast]
    glu_buf0, glu_buf1, yb_buf0, yb_buf1, sc_buf, p_buf, s2_buf, s4_buf, s8_buf = outs[MIXER_OUTPUTS + n_cast:]
    _run_cast_jobs(cast_src, cast_dst)
    t = pl.program_id(1)
    glu_buf = (glu_buf0, glu_buf1)
    yb_buf = (yb_buf0, yb_buf1)

    @pl.when(t == 0)
    def _():
        for s in range(LANE_SLABS):
            glu_buf[s][0:GLU_HALO, :] = jnp.zeros((GLU_HALO, LANES), F32)
        sc_buf[0:SC_HALO, :] = jnp.zeros((SC_HALO, D_GROUP), F32)
        p_buf[0:POOL_HALO, :] = jnp.zeros((POOL_HALO, D_GROUP), F32)

    for r0 in range(0, tt, sub):
        _mix_rows(t, r0, sub, x_ref, g_ref, win_ref, lng_ref, lnb_ref, wcat_ref, bst_ref, cw_ref, cb_ref, clg_ref,
                  clb_ref, sw_ref, pw_ref, ps_ref, mog_ref, wout_ref, o_ref,
                  glu_buf, yb_buf, sc_buf, p_buf, s2_buf, s4_buf, s8_buf, tt=tt)

    @pl.when(t == n_t - 1)
    def _():
        oglu_ref[0] = jnp.concatenate(
            [glu_buf[s][GLU_HALO + tt - (CONF_WIDTH - 1):GLU_HALO + tt, :] for s in range(LANE_SLABS)], axis=1)
        osh_ref[0] = sc_buf[SC_HALO + tt - (SC_WIDTH - 1):SC_HALO + tt, :]
        opool_ref[0] = p_buf[POOL_HALO + tt - POOL_BUF:POOL_HALO + tt, :]

    for s in range(LANE_SLABS):
        glu_buf[s][0:GLU_HALO, :] = glu_buf[s][tt:tt + GLU_HALO, :]
    sc_buf[0:SC_HALO, :] = sc_buf[tt:tt + SC_HALO, :]
    p_buf[0:POOL_HALO, :] = p_buf[tt:tt + POOL_HALO, :]


def _mix_rows(t, r0, sub, x_ref, g_ref, win_ref, lng_ref, lnb_ref, wcat_ref, bst_ref, cw_ref, cb_ref, clg_ref,
              clb_ref, sw_ref, pw_ref, ps_ref, mog_ref, wout_ref, o_ref,
              glu_buf, yb_buf, sc_buf, p_buf, s2_buf, s4_buf, s8_buf, *, tt):
    x = x_ref[0, r0:r0 + sub, :]
    z = _dot(_rms(x, g_ref[...]).astype(BF16), win_ref[...])
    zcol = lambda i: z[:, i * D_GROUP:(i + 1) * D_GROUP]
    group_norm = lambda g, y: _rms(y, mog_ref[:, g * D_GROUP:(g + 1) * D_GROUP]).astype(BF16)
    yn = [None] * N_GROUPS

    vn = _layer_norm(zcol(1), lng_ref[...], lnb_ref[...])
    head = lax.broadcasted_iota(jnp.int32, (CHUNK, D_GROUP), 1) // GMLP_HEAD_DIM
    gates = []
    for c in range(sub // CHUNK):
        vc = vn[c * CHUNK:(c + 1) * CHUNK]
        stack = jnp.concatenate([jnp.where(head == hh, vc, 0.0) for hh in range(GMLP_HEADS)], axis=0)
        gates.append(_dot(wcat_ref[...], stack.astype(BF16)) + bst_ref[...])
    yn[0] = group_norm(0, zcol(0) * jnp.concatenate(gates, axis=0))

    first = GLU_HALO - (CONF_WIDTH - 1)
    span = SUBLANES * CONV_STRIDE
    glu = zcol(2) * _sigmoid(zcol(3))
    for s in range(LANE_SLABS):
        glu_buf[s][GLU_HALO + r0:GLU_HALO + r0 + sub, :] = glu[:, s * LANES:(s + 1) * LANES]
    for c0 in range(r0, r0 + sub, CONV_ROWS):
        starts = [c0 + (i // CONV_STRIDE) * span + i % CONV_STRIDE for i in range(CONV_ROWS // SUBLANES)]
        accs = [[None] * len(starts) for _ in range(LANE_SLABS)]
        for k in range(CONF_WIDTH):
            for s in range(LANE_SLABS):
                wk = jnp.broadcast_to(cw_ref[k:k + 1, s * LANES:(s + 1) * LANES], (SUBLANES, LANES))
                for i, t0 in enumerate(starts):
                    term = wk * glu_buf[s][pl.ds(first + t0 + k, SUBLANES, stride=CONV_STRIDE), :]
                    accs[s][i] = term if k == 0 else accs[s][i] + term
        conv = jnp.concatenate([jnp.concatenate(a, axis=0) for a in accs], axis=1)
        ln = _layer_norm(conv + cb_ref[...], clg_ref[...], clb_ref[...])
        yb = ln * _sigmoid(ln)
        for s in range(LANE_SLABS):
            for i, t0 in enumerate(starts):
                yb_buf[s][pl.ds(t0, SUBLANES, stride=CONV_STRIDE), :] = (
                    yb[i * SUBLANES:(i + 1) * SUBLANES, s * LANES:(s + 1) * LANES])
    yn[1] = group_norm(1, jnp.concatenate([yb_buf[s][r0:r0 + sub, :] for s in range(LANE_SLABS)], axis=1))

    sxc = zcol(5) * zcol(6)
    lo = SC_HALO + r0
    sc_buf[lo:lo + sub, :] = sxc
    conv_c = (sw_ref[0:1, :] * sc_buf[lo - 2:lo - 2 + sub, :]
              + sw_ref[1:2, :] * sc_buf[lo - 1:lo - 1 + sub, :]
              + sw_ref[2:3, :] * sxc)
    yn[2] = group_norm(2, zcol(4) * conv_c)

    pool_x = zcol(7)
    lo = POOL_HALO + r0
    end = lo + sub
    p_buf[lo:end, :] = pool_x
    lo2, lo4, lo8 = (8, 16, 24) if r0 == 0 else (lo, lo, lo)
    s2_buf[lo2:end, :] = p_buf[lo2:end, :] + p_buf[lo2 - 1:end - 1, :]
    s4_buf[lo4:end, :] = s2_buf[lo4:end, :] + s2_buf[lo4 - 2:end - 2, :]
    s8_buf[lo8:end, :] = s4_buf[lo8:end, :] + s4_buf[lo8 - 4:end - 4, :]
    s16 = s8_buf[lo:end, :] + s8_buf[lo - 8:end - 8, :]
    grp = _pool_window_lanes((sub, D_GROUP))[0]
    ssum = _select_by_group(grp, [s2_buf[lo:end, :], s4_buf[lo:end, :], s8_buf[lo:end, :], s16])
    head_rows = POOL_BUF + 1 if r0 == 0 else 0
    inv_win = _select_by_group(_pool_window_lanes((sub - head_rows, D_GROUP))[0], [1.0 / w for w in POOL_WINDOWS])
    mean = ssum[head_rows:] * inv_win
    if head_rows:
        pos = t * tt + lax.broadcasted_iota(jnp.int32, (head_rows, D_GROUP), 0)
        cnt = jnp.minimum(_pool_window_lanes((head_rows, D_GROUP))[1], (pos + 1).astype(F32))
        mean = jnp.concatenate([ssum[:head_rows] / cnt, mean], axis=0)
    pooled = mean - pool_x
    yn[3] = group_norm(3, _dot(pooled.astype(BF16), pw_ref[...]) * ps_ref[...])

    o_ref[0, r0:r0 + sub, :] = x + _dot(jnp.concatenate(yn, axis=1), wout_ref[...])


def _mixer_prompt(x, lw, *, tt=512, sub=512, cast=()):
    bsz, seq, _ = x.shape
    n_t = seq // tt
    row = lambda a: a.reshape(1, -1)
    small = [row(lw['norm_mix']), lw['w_in'], row(lw['gmlp_ln_g']), row(lw['gmlp_ln_b']), lw['gmlp_wcat'],
             lw['gmlp_bias_tile'], lw['conf_dw'], row(lw['conf_dw_b']), row(lw['conf_ln_g']), row(lw['conf_ln_b']),
             lw['sc_dw'], lw['pool_w_bd'], row(lw['pool_scale']), row(lw['mix_out_g']), lw['w_out']]
    assert 1 + len(small) == MIXER_INPUTS
    jobs = [_cast_job(w, layer, bsz * n_t, lambda b, t: b * n_t + t) for w, layer in cast]
    state_spec = lambda rows: pl.BlockSpec((1, rows, D_GROUP), lambda b, t: (b, 0, 0))
    outs = pl.pallas_call(
        functools.partial(_mixer_prompt_kernel, tt=tt, n_t=n_t, sub=sub, n_cast=len(jobs)),
        grid=(bsz, n_t),
        in_specs=[pl.BlockSpec((1, tt, D_MODEL), lambda b, t: (b, t, 0))] + [_const_spec(a.shape) for a in small]
        + [j[0] for j in jobs],
        out_specs=[pl.BlockSpec((1, tt, D_MODEL), lambda b, t: (b, t, 0)),
                   state_spec(CONF_WIDTH - 1), state_spec(SC_WIDTH - 1), state_spec(POOL_BUF)] + [j[1] for j in jobs],
        out_shape=[jax.ShapeDtypeStruct(x.shape, F32),
                   jax.ShapeDtypeStruct((bsz, CONF_WIDTH - 1, D_GROUP), F32),
                   jax.ShapeDtypeStruct((bsz, SC_WIDTH - 1, D_GROUP), F32),
                   jax.ShapeDtypeStruct((bsz, POOL_BUF, D_GROUP), F32)] + [j[2] for j in jobs],
        scratch_shapes=[pltpu.VMEM((GLU_HALO + tt, LANES), F32)] * LANE_SLABS + [pltpu.VMEM((tt, LANES), F32)] * LANE_SLABS
        + [pltpu.VMEM((SC_HALO + tt, D_GROUP), F32)]
        + [pltpu.VMEM((POOL_HALO + tt, D_GROUP), F32)] * 4,
        compiler_params=pltpu.CompilerParams(dimension_semantics=("arbitrary", "arbitrary"),
                                             vmem_limit_bytes=VMEM_LIMIT),
        name="mixer_prompt",
    )(x, *small, *[w for w, _ in cast])
    return (*outs[:MIXER_OUTPUTS], list(outs[MIXER_OUTPUTS:]))


def _mixer_sample_kernel(x_ref, stg_ref, sts_ref, stp_ref, g_ref, win_ref, lng_ref, lnb_ref, w00_ref, b0_ref,
                         cw_ref, cb_ref, clg_ref, clb_ref, sw_ref, pw_ref, ps_ref, mog_ref, wout_ref,
                         o_ref, ovn_ref, oglu_ref, osh_ref, opool_ref):
    x = x_ref[...]
    h = _rms(x, g_ref[...]).astype(BF16)
    z = _dot(h, win_ref[...])
    u, v, glu_a, glu_g, sc_b, sc_c, sc_x, pool_x = [z[:, i * D_GROUP:(i + 1) * D_GROUP] for i in range(8)]
    hist = lambda ref, k: ref[:, k * D_GROUP:(k + 1) * D_GROUP]

    vn = _layer_norm(v, lng_ref[...], lnb_ref[...])
    ovn_ref[...] = vn
    y_a = u * (w00_ref[...] * vn + b0_ref[...])

    glu = glu_a * _sigmoid(glu_g)
    n_hist = CONF_WIDTH - 1
    acc = cw_ref[n_hist:n_hist + 1, :] * glu
    for k in range(n_hist):
        acc = acc + cw_ref[k:k + 1, :] * hist(stg_ref, k)
    ln = _layer_norm(acc + cb_ref[...], clg_ref[...], clb_ref[...])
    y_b = ln * _sigmoid(ln)
    oglu_ref[:, 0:(n_hist - 1) * D_GROUP] = stg_ref[:, D_GROUP:]
    oglu_ref[:, (n_hist - 1) * D_GROUP:] = glu

    sxc = sc_c * sc_x
    y_c = sc_b * (sw_ref[0:1, :] * hist(sts_ref, 0) + sw_ref[1:2, :] * hist(sts_ref, 1) + sw_ref[2:3, :] * sxc)
    osh_ref[:, 0:D_GROUP] = hist(sts_ref, 1)
    osh_ref[:, D_GROUP:] = sxc

    run = pool_x
    sums = []
    back = 0
    for w in POOL_WINDOWS:
        while back < w - 1:
            run = run + hist(stp_ref, POOL_BUF - 1 - back)
            back += 1
        sums.append(run)
    grp, win = _pool_window_lanes(pool_x.shape)
    cnt = jnp.minimum(win, float(PAST_LEN + 1))
    pooled = _select_by_group(grp, sums) / cnt - pool_x
    y_d = _dot(pooled.astype(BF16), pw_ref[...]) * ps_ref[...]
    opool_ref[:, 0:(POOL_BUF - 1) * D_GROUP] = stp_ref[:, D_GROUP:]
    opool_ref[:, (POOL_BUF - 1) * D_GROUP:] = pool_x

    o_ref[...] = _mix_out(x, [y_a, y_b, y_c, y_d], mog_ref, wout_ref)


def _mixer_sample(xs, st_glu, st_sh, st_pool, lw):
    n = xs.shape[0]
    row = lambda a: a.reshape(1, -1)
    args = [xs, st_glu.reshape(n, -1), st_sh.reshape(n, -1), st_pool.reshape(n, -1),
            row(lw['norm_mix']), lw['w_in'], row(lw['gmlp_ln_g']), row(lw['gmlp_ln_b']), row(lw['gmlp_w00']),
            row(lw['gmlp_b0']), lw['conf_dw'], row(lw['conf_dw_b']), row(lw['conf_ln_g']), row(lw['conf_ln_b']),
            lw['sc_dw'], lw['pool_w_bd'], row(lw['pool_scale']), row(lw['mix_out_g']), lw['w_out']]
    out_shapes = [(n, D_MODEL), (n, D_GROUP), (n, (CONF_WIDTH - 1) * D_GROUP), (n, (SC_WIDTH - 1) * D_GROUP),
                  (n, POOL_BUF * D_GROUP)]
    xo, vn, nglu, nsh, npool = pl.pallas_call(
        _mixer_sample_kernel,
        grid=(1,),
        in_specs=[_const_spec(a.shape) for a in args],
        out_specs=[pl.BlockSpec(s, lambda i: (0, 0)) for s in out_shapes],
        out_shape=[jax.ShapeDtypeStruct(s, F32) for s in out_shapes],
        compiler_params=pltpu.CompilerParams(dimension_semantics=("arbitrary",), vmem_limit_bytes=VMEM_LIMIT),
        name="mixer_sample",
    )(*args)
    return (xo, vn, nglu.reshape(n, CONF_WIDTH - 1, D_GROUP), nsh.reshape(n, SC_WIDTH - 1, D_GROUP),
            npool.reshape(n, POOL_BUF, D_GROUP))


def _mem_kv_kernel(m_ref, g_ref, wk_ref, wv_ref, ok_ref, ov_ref, okt_ref, ovb_ref):
    m = _rms(m_ref[0], g_ref[...]).astype(BF16)
    k = _dot(m, wk_ref[...])
    v = _dot(m, wv_ref[...])
    ok_ref[0] = k
    ov_ref[0] = v
    okt_ref[0] = k.T.astype(BF16)
    ovb_ref[0] = v.astype(BF16)


def _mem_kv(mem, lw):
    bsz = mem.shape[0]
    blk = lambda r, c: pl.BlockSpec((1, r, c), lambda b: (b, 0, 0))
    g = lw['norm_mem'].reshape(1, -1)
    return pl.pallas_call(
        _mem_kv_kernel,
        grid=(bsz,),
        in_specs=[blk(MEM_LEN, D_MODEL), _const_spec(g.shape), _const_spec(lw['w_xk'].shape),
                  _const_spec(lw['w_xv'].shape)],
        out_specs=[blk(MEM_LEN, D_MODEL), blk(MEM_LEN, D_MODEL), blk(D_MODEL, MEM_LEN), blk(MEM_LEN, D_MODEL)],
        out_shape=[jax.ShapeDtypeStruct((bsz, MEM_LEN, D_MODEL), F32)] * 2
        + [jax.ShapeDtypeStruct((bsz, D_MODEL, MEM_LEN), BF16), jax.ShapeDtypeStruct((bsz, MEM_LEN, D_MODEL), BF16)],
        compiler_params=pltpu.CompilerParams(dimension_semantics=("arbitrary",), vmem_limit_bytes=VMEM_LIMIT),
        name="mem_kv",
    )(mem, g, lw['w_xk'], lw['w_xv'])


def _xattn_prompt_kernel(x_ref, g_ref, wq_ref, kt_ref, vb_ref, wo_ref, o_ref):
    x = x_ref[0]
    h = _rms(x, g_ref[...]).astype(BF16)
    q = (_dot(h, wq_ref[...]) * (XATTN_HEAD_DIM ** -0.5)).astype(BF16)
    heads = []
    for hh in range(XATTN_HEADS):
        lo, hi = hh * XATTN_HEAD_DIM, (hh + 1) * XATTN_HEAD_DIM
        s = _dot(q[:, lo:hi], kt_ref[0, lo:hi, :])
        e = jnp.exp(s - jnp.max(s, axis=-1, keepdims=True))
        p = e * (1.0 / jnp.sum(e, axis=-1, keepdims=True))
        heads.append(_dot(p.astype(BF16), vb_ref[0, :, lo:hi]).astype(BF16))
    o_ref[0] = x + _dot(jnp.concatenate(heads, axis=1), wo_ref[...])


def _xattn_prompt(x, kt, vb, lw, *, tt=512):
    bsz, seq, _ = x.shape
    g = lw['norm_xattn'].reshape(1, -1)
    return pl.pallas_call(
        _xattn_prompt_kernel,
        grid=(bsz, seq // tt),
        in_specs=[pl.BlockSpec((1, tt, D_MODEL), lambda b, t: (b, t, 0)), _const_spec(g.shape),
                  _const_spec(lw['w_xq'].shape),
                  pl.BlockSpec((1, D_MODEL, MEM_LEN), lambda b, t: (b, 0, 0)),
                  pl.BlockSpec((1, MEM_LEN, D_MODEL), lambda b, t: (b, 0, 0)),
                  _const_spec(lw['w_xo'].shape)],
        out_specs=pl.BlockSpec((1, tt, D_MODEL), lambda b, t: (b, t, 0)),
        out_shape=jax.ShapeDtypeStruct(x.shape, F32),
        compiler_params=pltpu.CompilerParams(dimension_semantics=("arbitrary", "arbitrary"),
                                             vmem_limit_bytes=VMEM_LIMIT),
        name="xattn_prompt",
    )(x, g, lw['w_xq'], kt, vb, lw['w_xo'])


def _norm_proj_kernel(x_ref, g_ref, w_ref, o_ref, *, scale):
    o_ref[...] = _dot(_rms(x_ref[...], g_ref[...]).astype(BF16), w_ref[...]) * scale


def _norm_proj(x, g, w, scale):
    g = g.reshape(1, -1)
    out_shape = (x.shape[0], w.shape[1])
    return pl.pallas_call(
        functools.partial(_norm_proj_kernel, scale=scale),
        grid=(1,),
        in_specs=[_const_spec(x.shape), _const_spec(g.shape), _const_spec(w.shape)],
        out_specs=pl.BlockSpec(out_shape, lambda i: (0, 0)),
        out_shape=jax.ShapeDtypeStruct(out_shape, F32),
        compiler_params=pltpu.CompilerParams(dimension_semantics=("arbitrary",), vmem_limit_bytes=VMEM_LIMIT),
        name="norm_proj",
    )(x, g, w)


def _proj_residual_kernel(x_ref, a_ref, w_ref, o_ref):
    o_ref[...] = x_ref[...] + _dot(a_ref[...].astype(BF16), w_ref[...])


def _proj_residual(x, a, w):
    return pl.pallas_call(
        _proj_residual_kernel,
        grid=(1,),
        in_specs=[_const_spec(x.shape), _const_spec(a.shape), _const_spec(w.shape)],
        out_specs=pl.BlockSpec(x.shape, lambda i: (0, 0)),
        out_shape=jax.ShapeDtypeStruct(x.shape, F32),
        compiler_params=pltpu.CompilerParams(dimension_semantics=("arbitrary",), vmem_limit_bytes=VMEM_LIMIT),
        name="proj_residual",
    )(x, a, w)


def _split_head_dim(a):
    lead = a.shape[:-2]
    a = a.reshape(*lead, XATTN_HEADS, 2, XATTN_HEAD_DIM // 2)
    return jnp.swapaxes(a, -3, -2).reshape(*lead, 2 * XATTN_HEADS, XATTN_HEAD_DIM // 2)


def _merge_head_dim(a):
    lead = a.shape[:-2]
    a = a.reshape(*lead, 2, XATTN_HEADS, XATTN_HEAD_DIM // 2)
    return jnp.swapaxes(a, -3, -2).reshape(*lead, XATTN_HEADS, XATTN_HEAD_DIM)


def _attend_rows(q_ref, k_ref, v_ref, o_ref, rows):
    for r in rows:
        part = jnp.sum(k_ref[r] * q_ref[r][None], axis=-1, keepdims=True)
        s = part + pltpu.roll(part, XATTN_HEADS, axis=1)
        e = jnp.exp(s - jnp.max(s, axis=0, keepdims=True))
        p = e * (1.0 / jnp.sum(e, axis=0, keepdims=True))
        o_ref[r] = jnp.sum(p * v_ref[r], axis=0)


def _ffn_kernel(x_ref, g_ref, w1_ref, w2_ref, gf_ref, o_ref, *, final_norm, between_chunks=None):
    x = x_ref[...]
    h = _rms(x, g_ref[...]).astype(BF16)
    y = x
    for i, c in enumerate(range(0, D_FF, FF_CHUNK)):
        if between_chunks is not None:
            between_chunks(i)
        a = jnp.maximum(_dot(h, w1_ref[:, c:c + FF_CHUNK]), 0.0)
        y = y + _dot((a * a).astype(BF16), w2_ref[c:c + FF_CHUNK, :])
    o_ref[...] = _rms(y, gf_ref[...]) if final_norm else y


FFN_ATTEND_INPUTS = 8
FFN_ATTEND_OUTPUTS = 2


def _ffn_attend_kernel(*refs, final_norm, rows, n_cast):
    x_ref, g_ref, w1_ref, w2_ref, gf_ref, q_ref, k_ref, v_ref = refs[:FFN_ATTEND_INPUTS]
    cast_src = refs[FFN_ATTEND_INPUTS:FFN_ATTEND_INPUTS + n_cast]
    o_ref, oa_ref = refs[FFN_ATTEND_INPUTS + n_cast:FFN_ATTEND_INPUTS + n_cast + FFN_ATTEND_OUTPUTS]
    cast_dst = refs[FFN_ATTEND_INPUTS + n_cast + FFN_ATTEND_OUTPUTS:]
    _run_cast_jobs(cast_src, cast_dst)
    n_chunks = D_FF // FF_CHUNK
    assert rows % n_chunks == 0
    per_chunk = rows // n_chunks
    attend = lambda i: _attend_rows(q_ref, k_ref, v_ref, oa_ref, range(i * per_chunk, (i + 1) * per_chunk))
    _ffn_kernel(x_ref, g_ref, w1_ref, w2_ref, gf_ref, o_ref, final_norm=final_norm, between_chunks=attend)


def _ffn(x2d, lw, norm_final, *, final_norm, tm=512, attend=None, cast=()):
    n = x2d.shape[0]
    tm = min(tm, n)
    g = lw['norm_ffn'].reshape(1, -1)
    gf = norm_final.reshape(1, -1)
    in_specs = [pl.BlockSpec((tm, D_MODEL), lambda i: (i, 0)), _const_spec(g.shape),
                _const_spec(lw['w_ff1'].shape), _const_spec(lw['w_ff2'].shape), _const_spec(gf.shape)]
    out_spec = pl.BlockSpec((tm, D_MODEL), lambda i: (i, 0))
    out_shape = jax.ShapeDtypeStruct(x2d.shape, F32)
    params = pltpu.CompilerParams(dimension_semantics=("arbitrary",), vmem_limit_bytes=VMEM_LIMIT)
    if attend is None:
        return pl.pallas_call(
            functools.partial(_ffn_kernel, final_norm=final_norm), grid=(n // tm,), in_specs=in_specs,
            out_specs=out_spec, out_shape=out_shape, compiler_params=params, name="ffn",
        )(x2d, g, lw['w_ff1'], lw['w_ff2'], gf)
    q, cache_k, cache_v, layer = attend
    rows = q.shape[0] // (n // tm)
    assert rows * (n // tm) == q.shape[0]
    q_spec = pl.BlockSpec((rows,) + q.shape[1:], lambda i: (i, 0, 0))
    kv_spec = pl.BlockSpec((None, rows) + cache_k.shape[2:], lambda i: (layer, i, 0, 0, 0))
    jobs = [_cast_job(w, wl, n // tm, lambda i: i) for w, wl in cast]
    outs = pl.pallas_call(
        functools.partial(_ffn_attend_kernel, final_norm=final_norm, rows=rows, n_cast=len(jobs)), grid=(n // tm,),
        in_specs=in_specs + [q_spec, kv_spec, kv_spec] + [j[0] for j in jobs],
        out_specs=[out_spec, q_spec] + [j[1] for j in jobs],
        out_shape=[out_shape, jax.ShapeDtypeStruct(q.shape, F32)] + [j[2] for j in jobs],
        compiler_params=params, name="ffn_attend",
    )(x2d, g, lw['w_ff1'], lw['w_ff2'], gf, q, cache_k, cache_v, *[w for w, _ in cast])
    return outs[0], outs[1], list(outs[FFN_ATTEND_OUTPUTS:])


PROJ_WEIGHTS = ('w_in', 'w_out', 'w_xq', 'w_xk', 'w_xv', 'w_xo')


def _layer_weights(l, p, proj_bf16):
    tril = jnp.tril(jnp.ones((CHUNK, CHUNK), dtype=bool))
    ws = jnp.where(tril[None], p['gmlp_ws'][l], 0.0)
    pool_bd = jax.scipy.linalg.block_diag(*[p['pool_w'][l, g] for g in range(len(POOL_WINDOWS))])
    lw = {k: p[k][l] for k in ('norm_mix', 'gmlp_ln_g', 'gmlp_ln_b', 'conf_dw', 'conf_dw_b', 'conf_ln_g',
                               'conf_ln_b', 'sc_dw', 'pool_scale', 'mix_out_g', 'norm_xattn', 'norm_mem',
                               'norm_ffn')}
    lw.update(proj_bf16)
    lw.update(
        gmlp_wcat=jnp.concatenate([ws[h] for h in range(GMLP_HEADS)], axis=1).astype(BF16),
        gmlp_bias_tile=jnp.repeat(p['gmlp_bs'][l].T, GMLP_HEAD_DIM, axis=1),
        gmlp_w00=jnp.repeat(ws[:, 0, 0], GMLP_HEAD_DIM), gmlp_b0=jnp.repeat(p['gmlp_bs'][l][:, 0], GMLP_HEAD_DIM),
        pool_w_bd=pool_bd.astype(BF16))
    return lw


def kernel(x_prompt, x_sample, mem_prompt, cache_mem_k, cache_mem_v, state_conv_glu, state_conv_short, state_pool, norm_mix, w_in, gmlp_ln_g, gmlp_ln_b, gmlp_ws, gmlp_bs, conf_dw, conf_dw_b, conf_ln_g, conf_ln_b, sc_dw, pool_w, pool_scale, mix_out_g, w_out, norm_xattn, norm_mem, w_xq, w_xk, w_xv, w_xo, norm_ffn, w_ff1, w_ff2, norm_final):
    params = dict(norm_mix=norm_mix, w_in=w_in, gmlp_ln_g=gmlp_ln_g, gmlp_ln_b=gmlp_ln_b, gmlp_ws=gmlp_ws,
                  gmlp_bs=gmlp_bs, conf_dw=conf_dw, conf_dw_b=conf_dw_b, conf_ln_g=conf_ln_g, conf_ln_b=conf_ln_b,
                  sc_dw=sc_dw, pool_w=pool_w, pool_scale=pool_scale, mix_out_g=mix_out_g, w_out=w_out,
                  norm_xattn=norm_xattn, norm_mem=norm_mem, w_xq=w_xq, w_xk=w_xk, w_xv=w_xv, w_xo=w_xo,
                  norm_ffn=norm_ffn, w_ff1=w_ff1, w_ff2=w_ff2)
    depth = w_in.shape[0]
    bsz, seq, _ = x_prompt.shape
    n_s = x_sample.shape[0]
    xp = x_prompt
    xs = x_sample.reshape(n_s, D_MODEL)
    outs = {k: [] for k in ('mk', 'mv', 'glu_p', 'glu_s', 'sh_p', 'sh_s', 'pl_p', 'pl_s', 'v_s')}
    cache_k_split = _split_head_dim(cache_mem_k)
    cache_v_split = _split_head_dim(cache_mem_v)
    proj_bf16 = {k: params[k][0].astype(BF16) for k in PROJ_WEIGHTS}
    for l in range(depth):
        lw = _layer_weights(l, params, proj_bf16)
        last = l == depth - 1
        xp, glu_p, sh_p, pool_p, (lw['w_ff1'], lw['w_ff2']) = _mixer_prompt(
            xp, lw, cast=[(params['w_ff1'], l), (params['w_ff2'], l)])
        k_p, v_p, kt, vb = _mem_kv(mem_prompt, lw)
        xp = _xattn_prompt(xp, kt, vb, lw)
        outs['mk'].append(k_p.reshape(bsz, MEM_LEN, XATTN_HEADS, XATTN_HEAD_DIM))
        outs['mv'].append(v_p.reshape(bsz, MEM_LEN, XATTN_HEADS, XATTN_HEAD_DIM))
        outs['glu_p'].append(glu_p); outs['sh_p'].append(sh_p); outs['pl_p'].append(pool_p)
        xs, vn_s, glu_s, sh_s, pool_s = _mixer_sample(xs, state_conv_glu[l], state_conv_short[l], state_pool[l], lw)
        q_s = _norm_proj(xs, lw['norm_xattn'], lw['w_xq'], XATTN_HEAD_DIM ** -0.5)
        q_s = _split_head_dim(q_s.reshape(n_s, XATTN_HEADS, XATTN_HEAD_DIM))
        xp, o_s, nxt = _ffn(xp.reshape(bsz * seq, D_MODEL), lw, norm_final, final_norm=last,
                            attend=(q_s, cache_k_split, cache_v_split, l),
                            cast=[] if last else [(params[k], l + 1) for k in PROJ_WEIGHTS])
        proj_bf16 = dict(zip(PROJ_WEIGHTS, nxt))
        xp = xp.reshape(bsz, seq, D_MODEL)
        xs = _proj_residual(xs, _merge_head_dim(o_s).reshape(n_s, D_MODEL), lw['w_xo'])
        xs = _ffn(xs, lw, norm_final, final_norm=last)
        outs['glu_s'].append(glu_s); outs['sh_s'].append(sh_s); outs['pl_s'].append(pool_s)
        outs['v_s'].append(vn_s.reshape(n_s, 1, D_GROUP))
    st = lambda k: jnp.stack(outs[k], axis=0)
    return (xp, xs.reshape(n_s, 1, D_MODEL), st('mk'), st('mv'), st('glu_p'), st('glu_s'), st('sh_p'), st('sh_s'),
            st('pl_p'), st('pl_s'), st('v_s'))
```

```python
import functools

import jax
import jax.numpy as jnp
from jax import lax
from jax.experimental import pallas as pl
from jax.experimental.pallas import tpu as pltpu

F32 = jnp.float32
BF16 = jnp.bfloat16

D_MODEL = 1024
D_GROUP = 256
N_GROUPS = 4
D_IN_PROJ = 8 * D_GROUP
GMLP_HEADS = 4
GMLP_HEAD_DIM = D_GROUP // GMLP_HEADS
CHUNK = 128
CONF_WIDTH = 31
SC_WIDTH = 3
POOL_WINDOWS = (2, 4, 8, 16)
POOL_GROUP_DIM = D_GROUP // len(POOL_WINDOWS)
POOL_BUF = max(POOL_WINDOWS) - 1
MEM_LEN = 256
XATTN_HEADS = 4
XATTN_HEAD_DIM = D_MODEL // XATTN_HEADS
D_FF = 4 * D_MODEL
PAST_LEN = 16384
EPS = 1e-6

GLU_HALO = 32
SC_HALO = 8
POOL_HALO = 32
LANES = 128
SUBLANES = 8
BF16_SUBLANES = 16
LANE_SLABS = D_GROUP // LANES
CONV_STRIDE = 4
CONV_ROWS = 64
FF_CHUNK = 1024
VMEM_LIMIT = 56 * 1024 * 1024


def _rms(x, g):
    return x * lax.rsqrt(jnp.mean(x * x, axis=-1, keepdims=True) + EPS) * g


def _layer_norm(x, g, b):
    xc = x - jnp.mean(x, axis=-1, keepdims=True)
    return xc * lax.rsqrt(jnp.mean(xc * xc, axis=-1, keepdims=True) + EPS) * g + b


def _sigmoid(x):
    return 0.5 * jnp.tanh(0.5 * x) + 0.5


def _dot(a, b):
    return jnp.dot(a, b, preferred_element_type=F32)


def _const_spec(shape):
    zeros = (0,) * len(shape)
    return pl.BlockSpec(shape, lambda *_: zeros, pipeline_mode=pl.Buffered(1))


def _cast_job(stacked, layer, steps, step_of):
    _, r, c = stacked.shape
    blk = r // steps
    assert blk * steps == r and blk % BF16_SUBLANES == 0
    return (pl.BlockSpec((None, blk, c), lambda *ids: (layer, step_of(*ids), 0)),
            pl.BlockSpec((blk, c), lambda *ids: (step_of(*ids), 0)),
            jax.ShapeDtypeStruct((r, c), BF16))


def _run_cast_jobs(src_refs, dst_refs):
    for src, dst in zip(src_refs, dst_refs):
        dst[...] = src[...].astype(BF16)


def _pool_window_lanes(shape):
    grp = lax.broadcasted_iota(jnp.int32, shape, len(shape) - 1) // POOL_GROUP_DIM
    win = jnp.full(shape, float(POOL_WINDOWS[-1]), F32)
    for g in range(len(POOL_WINDOWS) - 2, -1, -1):
        win = jnp.where(grp == g, float(POOL_WINDOWS[g]), win)
    return grp, win


def _select_by_group(grp, vals):
    out = vals[-1]
    for g in range(len(vals) - 2, -1, -1):
        out = jnp.where(grp == g, vals[g], out)
    return out


def _mix_out(x, y_groups, mog_ref, wout_ref):
    yn = [_rms(y, mog_ref[:, g * D_GROUP:(g + 1) * D_GROUP]).astype(BF16) for g, y in enumerate(y_groups)]
    return x + _dot(jnp.concatenate(yn, axis=1), wout_ref[...])


MIXER_INPUTS = 16
MIXER_OUTPUTS = 4


def _mixer_prompt_kernel(*refs, tt, n_t, sub, n_cast):
    (x_ref, g_ref, win_ref, lng_ref, lnb_ref, wcat_ref, bst_ref, cw_ref, cb_ref, clg_ref,
     clb_ref, sw_ref, pw_ref, ps_ref, mog_ref, wout_ref) = refs[:MIXER_INPUTS]
    cast_src = refs[MIXER_INPUTS:MIXER_INPUTS + n_cast]
    outs = refs[MIXER_INPUTS + n_cast:]
    o_ref, oglu_ref, osh_ref, opool_ref = outs[:MIXER_OUTPUTS]
    cast_dst = outs[MIXER_OUTPUTS:MIXER_OUTPUTS + n_cast]
    glu_buf0, glu_buf1, yb_buf0, yb_buf1, sc_buf, p_buf, s2_buf, s4_buf, s8_buf = outs[MIXER_OUTPUTS + n_cast:]
    _run_cast_jobs(cast_src, cast_dst)
    t = pl.program_id(1)
    glu_buf = (glu_buf0, glu_buf1)
    yb_buf = (yb_buf0, yb_buf1)

    @pl.when(t == 0)
    def _():
        for s in range(LANE_SLABS):
            glu_buf[s][0:GLU_HALO, :] = jnp.zeros((GLU_HALO, LANES), F32)
        sc_buf[0:SC_HALO, :] = jnp.zeros((SC_HALO, D_GROUP), F32)
        p_buf[0:POOL_HALO, :] = jnp.zeros((POOL_HALO, D_GROUP), F32)

    for r0 in range(0, tt, sub):
        _mix_rows(t, r0, sub, x_ref, g_ref, win_ref, lng_ref, lnb_ref, wcat_ref, bst_ref, cw_ref, cb_ref, clg_ref,
                  clb_ref, sw_ref, pw_ref, ps_ref, mog_ref, wout_ref, o_ref,
                  glu_buf, yb_buf, sc_buf, p_buf, s2_buf, s4_buf, s8_buf, tt=tt)

    @pl.when(t == n_t - 1)
    def _():
        oglu_ref[0] = jnp.concatenate(
            [glu_buf[s][GLU_HALO + tt - (CONF_WIDTH - 1):GLU_HALO + tt, :] for s in range(LANE_SLABS)], axis=1)
        osh_ref[0] = sc_buf[SC_HALO + tt - (SC_WIDTH - 1):SC_HALO + tt, :]
        opool_ref[0] = p_buf[POOL_HALO + tt - POOL_BUF:POOL_HALO + tt, :]

    for s in range(LANE_SLABS):
        glu_buf[s][0:GLU_HALO, :] = glu_buf[s][tt:tt + GLU_HALO, :]
    sc_buf[0:SC_HALO, :] = sc_buf[tt:tt + SC_HALO, :]
    p_buf[0:POOL_HALO, :] = p_buf[tt:tt + POOL_HALO, :]


def _mix_rows(t, r0, sub, x_ref, g_ref, win_ref, lng_ref, lnb_ref, wcat_ref, bst_ref, cw_ref, cb_ref, clg_ref,
              clb_ref, sw_ref, pw_ref, ps_ref, mog_ref, wout_ref, o_ref,
              glu_buf, yb_buf, sc_buf, p_buf, s2_buf, s4_buf, s8_buf, *, tt):
    x = x_ref[0, r0:r0 + sub, :]
    z = _dot(_rms(x, g_ref[...]).astype(BF16), win_ref[...])
    zcol = lambda i: z[:, i * D_GROUP:(i + 1) * D_GROUP]
    group_norm = lambda g, y: _rms(y, mog_ref[:, g * D_GROUP:(g + 1) * D_GROUP]).astype(BF16)
    yn = [None] * N_GROUPS

    vn = _layer_norm(zcol(1), lng_ref[...], lnb_ref[...])
    head = lax.broadcasted_iota(jnp.int32, (CHUNK, D_GROUP), 1) // GMLP_HEAD_DIM
    gates = []
    for c in range(sub // CHUNK):
        vc = vn[c * CHUNK:(c + 1) * CHUNK]
        stack = jnp.concatenate([jnp.where(head == hh, vc, 0.0) for hh in range(GMLP_HEADS)], axis=0)
        gates.append(_dot(wcat_ref[...], stack.astype(BF16)) + bst_ref[...])
    yn[0] = group_norm(0, zcol(0) * jnp.concatenate(gates, axis=0))

    first = GLU_HALO - (CONF_WIDTH - 1)
    span = SUBLANES * CONV_STRIDE
    glu = zcol(2) * _sigmoid(zcol(3))
    for s in range(LANE_SLABS):
        glu_buf[s][GLU_HALO + r0:GLU_HALO + r0 + sub, :] = glu[:, s * LANES:(s + 1) * LANES]
    for c0 in range(r0, r0 + sub, CONV_ROWS):
        starts = [c0 + (i // CONV_STRIDE) * span + i % CONV_STRIDE for i in range(CONV_ROWS // SUBLANES)]
        accs = [[None] * len(starts) for _ in range(LANE_SLABS)]
        for k in range(CONF_WIDTH):
            for s in range(LANE_SLABS):
                wk = jnp.broadcast_to(cw_ref[k:k + 1, s * LANES:(s + 1) * LANES], (SUBLANES, LANES))
                for i, t0 in enumerate(starts):
                    term = wk * glu_buf[s][pl.ds(first + t0 + k, SUBLANES, stride=CONV_STRIDE), :]
                    accs[s][i] = term if k == 0 else accs[s][i] + term
        conv = jnp.concatenate([jnp.concatenate(a, axis=0) for a in accs], axis=1)
        ln = _layer_norm(conv + cb_ref[...], clg_ref[...], clb_ref[...])
        yb = ln * _sigmoid(ln)
        for s in range(LANE_SLABS):
            for i, t0 in enumerate(starts):
                yb_buf[s][pl.ds(t0, SUBLANES, stride=CONV_STRIDE), :] = (
                    yb[i * SUBLANES:(i + 1) * SUBLANES, s * LANES:(s + 1) * LANES])
    yn[1] = group_norm(1, jnp.concatenate([yb_buf[s][r0:r0 + sub, :] for s in range(LANE_SLABS)], axis=1))

    sxc = zcol(5) * zcol(6)
    lo = SC_HALO + r0
    sc_buf[lo:lo + sub, :] = sxc
    conv_c = (sw_ref[0:1, :] * sc_buf[lo - 2:lo - 2 + sub, :]
              + sw_ref[1:2, :] * sc_buf[lo - 1:lo - 1 + sub, :]
              + sw_ref[2:3, :] * sxc)
    yn[2] = group_norm(2, zcol(4) * conv_c)

    pool_x = zcol(7)
    lo = POOL_HALO + r0
    end = lo + sub
    p_buf[lo:end, :] = pool_x
    lo2, lo4, lo8 = (8, 16, 24) if r0 == 0 else (lo, lo, lo)
    s2_buf[lo2:end, :] = p_buf[lo2:end, :] + p_buf[lo2 - 1:end - 1, :]
    s4_buf[lo4:end, :] = s2_buf[lo4:end, :] + s2_buf[lo4 - 2:end - 2, :]
    s8_buf[lo8:end, :] = s4_buf[lo8:end, :] + s4_buf[lo8 - 4:end - 4, :]
    s16 = s8_buf[lo:end, :] + s8_buf[lo - 8:end - 8, :]
    grp = _pool_window_lanes((sub, D_GROUP))[0]
    ssum = _select_by_group(grp, [s2_buf[lo:end, :], s4_buf[lo:end, :], s8_buf[lo:end, :], s16])
    head_rows = POOL_BUF + 1 if r0 == 0 else 0
    inv_win = _select_by_group(_pool_window_lanes((sub - head_rows, D_GROUP))[0], [1.0 / w for w in POOL_WINDOWS])
    mean = ssum[head_rows:] * inv_win
    if head_rows:
        pos = t * tt + lax.broadcasted_iota(jnp.int32, (head_rows, D_GROUP), 0)
        cnt = jnp.minimum(_pool_window_lanes((head_rows, D_GROUP))[1], (pos + 1).astype(F32))
        mean = jnp.concatenate([ssum[:head_rows] / cnt, mean], axis=0)
    pooled = mean - pool_x
    yn[3] = group_norm(3, _dot(pooled.astype(BF16), pw_ref[...]) * ps_ref[...])

    o_ref[0, r0:r0 + sub, :] = x + _dot(jnp.concatenate(yn, axis=1), wout_ref[...])


def _mixer_prompt(x, lw, *, tt=512, sub=512, cast=()):
    bsz, seq, _ = x.shape
    n_t = seq // tt
    row = lambda a: a.reshape(1, -1)
    small = [row(lw['norm_mix']), lw['w_in'], row(lw['gmlp_ln_g']), row(lw['gmlp_ln_b']), lw['gmlp_wcat'],
             lw['gmlp_bias_tile'], lw['conf_dw'], row(lw['conf_dw_b']), row(lw['conf_ln_g']), row(lw['conf_ln_b']),
             lw['sc_dw'], lw['pool_w_bd'], row(lw['pool_scale']), row(lw['mix_out_g']), lw['w_out']]
    assert 1 + len(small) == MIXER_INPUTS
    jobs = [_cast_job(w, layer, bsz * n_t, lambda b, t: b * n_t + t) for w, layer in cast]
    state_spec = lambda rows: pl.BlockSpec((1, rows, D_GROUP), lambda b, t: (b, 0, 0))
    outs = pl.pallas_call(
        functools.partial(_mixer_prompt_kernel, tt=tt, n_t=n_t, sub=sub, n_cast=len(jobs)),
        grid=(bsz, n_t),
        in_specs=[pl.BlockSpec((1, tt, D_MODEL), lambda b, t: (b, t, 0))] + [_const_spec(a.shape) for a in small]
        + [j[0] for j in jobs],
        out_specs=[pl.BlockSpec((1, tt, D_MODEL), lambda b, t: (b, t, 0)),
                   state_spec(CONF_WIDTH - 1), state_spec(SC_WIDTH - 1), state_spec(POOL_BUF)] + [j[1] for j in jobs],
        out_shape=[jax.ShapeDtypeStruct(x.shape, F32),
                   jax.ShapeDtypeStruct((bsz, CONF_WIDTH - 1, D_GROUP), F32),
                   jax.ShapeDtypeStruct((bsz, SC_WIDTH - 1, D_GROUP), F32),
                   jax.ShapeDtypeStruct((bsz, POOL_BUF, D_GROUP), F32)] + [j[2] for j in jobs],
        scratch_shapes=[pltpu.VMEM((GLU_HALO + tt, LANES), F32)] * LANE_SLABS + [pltpu.VMEM((tt, LANES), F32)] * LANE_SLABS
        + [pltpu.VMEM((SC_HALO + tt, D_GROUP), F32)]
        + [pltpu.VMEM((POOL_HALO + tt, D_GROUP), F32)] * 4,
        compiler_params=pltpu.CompilerParams(dimension_semantics=("arbitrary", "arbitrary"),
                                             vmem_limit_bytes=VMEM_LIMIT),
        name="mixer_prompt",
    )(x, *small, *[w for w, _ in cast])
    return (*outs[:MIXER_OUTPUTS], list(outs[MIXER_OUTPUTS:]))


def _mixer_sample_kernel(x_ref, stg_ref, sts_ref, stp_ref, g_ref, win_ref, lng_ref, lnb_ref, w00_ref, b0_ref,
                         cw_ref, cb_ref, clg_ref, clb_ref, sw_ref, pw_ref, ps_ref, mog_ref, wout_ref,
                         o_ref, ovn_ref, oglu_ref, osh_ref, opool_ref):
    x = x_ref[...]
    h = _rms(x, g_ref[...]).astype(BF16)
    z = _dot(h, win_ref[...])
    u, v, glu_a, glu_g, sc_b, sc_c, sc_x, pool_x = [z[:, i * D_GROUP:(i + 1) * D_GROUP] for i in range(8)]
    hist = lambda ref, k: ref[:, k, :]

    def push(new_ref, old_ref, row):
        n_hist = old_ref.shape[1]
        new_ref[:, 0:n_hist - 1, :] = old_ref[:, 1:n_hist, :]
        new_ref[:, n_hist - 1, :] = row

    vn = _layer_norm(v, lng_ref[...], lnb_ref[...])
    ovn_ref[...] = vn
    y_a = u * (w00_ref[...] * vn + b0_ref[...])

    glu = glu_a * _sigmoid(glu_g)
    n_hist = CONF_WIDTH - 1
    acc = cw_ref[n_hist:n_hist + 1, :] * glu
    for k in range(n_hist):
        acc = acc + cw_ref[k:k + 1, :] * hist(stg_ref, k)
    ln = _layer_norm(acc + cb_ref[...], clg_ref[...], clb_ref[...])
    y_b = ln * _sigmoid(ln)
    push(oglu_ref, stg_ref, glu)

    sxc = sc_c * sc_x
    y_c = sc_b * (sw_ref[0:1, :] * hist(sts_ref, 0) + sw_ref[1:2, :] * hist(sts_ref, 1) + sw_ref[2:3, :] * sxc)
    push(osh_ref, sts_ref, sxc)

    run = pool_x
    sums = []
    back = 0
    for w in POOL_WINDOWS:
        while back < w - 1:
            run = run + hist(stp_ref, POOL_BUF - 1 - back)
            back += 1
        sums.append(run)
    grp, win = _pool_window_lanes(pool_x.shape)
    cnt = jnp.minimum(win, float(PAST_LEN + 1))
    pooled = _select_by_group(grp, sums) / cnt - pool_x
    y_d = _dot(pooled.astype(BF16), pw_ref[...]) * ps_ref[...]
    push(opool_ref, stp_ref, pool_x)

    o_ref[...] = _mix_out(x, [y_a, y_b, y_c, y_d], mog_ref, wout_ref)


def _mixer_sample(xs, st_glu, st_sh, st_pool, layer, lw):
    n = xs.shape[0]
    row = lambda a: a.reshape(1, -1)
    states = [st_glu, st_sh, st_pool]
    consts = [row(lw['norm_mix']), lw['w_in'], row(lw['gmlp_ln_g']), row(lw['gmlp_ln_b']), row(lw['gmlp_w00']),
              row(lw['gmlp_b0']), lw['conf_dw'], row(lw['conf_dw_b']), row(lw['conf_ln_g']), row(lw['conf_ln_b']),
              lw['sc_dw'], lw['pool_w_bd'], row(lw['pool_scale']), row(lw['mix_out_g']), lw['w_out']]
    state_spec = lambda s: pl.BlockSpec((None,) + s.shape[1:], lambda i: (layer, 0, 0, 0), pipeline_mode=pl.Buffered(1))
    out_shapes = [(n, D_MODEL), (n, D_GROUP)] + [s.shape[1:] for s in states]
    return pl.pallas_call(
        _mixer_sample_kernel,
        grid=(1,),
        in_specs=[_const_spec(xs.shape)] + [state_spec(s) for s in states] + [_const_spec(a.shape) for a in consts],
        out_specs=[pl.BlockSpec(s, lambda i, nd=len(s): (0,) * nd) for s in out_shapes],
        out_shape=[jax.ShapeDtypeStruct(s, F32) for s in out_shapes],
        compiler_params=pltpu.CompilerParams(dimension_semantics=("arbitrary",), vmem_limit_bytes=VMEM_LIMIT),
        name="mixer_sample",
    )(xs, *states, *consts)


def _mem_kv_kernel(m_ref, g_ref, wk_ref, wv_ref, ok_ref, ov_ref, okt_ref, ovb_ref, wk_bf16, wv_bf16):
    @pl.when(pl.program_id(1) == 0)
    def _():
        wk_bf16[...] = wk_ref[...].astype(BF16)
        wv_bf16[...] = wv_ref[...].astype(BF16)

    m = _rms(m_ref[0], g_ref[...]).astype(BF16)
    k = _dot(m, wk_bf16[...])
    v = _dot(m, wv_bf16[...])
    for h in range(XATTN_HEADS):
        ok_ref[0, :, h, :] = k[:, h * XATTN_HEAD_DIM:(h + 1) * XATTN_HEAD_DIM]
        ov_ref[0, :, h, :] = v[:, h * XATTN_HEAD_DIM:(h + 1) * XATTN_HEAD_DIM]
    okt_ref[0] = k.T.astype(BF16)
    ovb_ref[0] = v.astype(BF16)


def _mem_kv(mem, norm_mem, w_xk, w_xv):
    depth = w_xk.shape[0]
    bsz = mem.shape[0]
    g = norm_mem.reshape(depth, 1, D_MODEL)
    per_layer = lambda *shape: pl.BlockSpec((None,) + shape, lambda l, b: (l,) + (0,) * len(shape))
    out_blk = lambda *shape: pl.BlockSpec((None, 1) + shape, lambda l, b: (l, b) + (0,) * len(shape))
    return pl.pallas_call(
        _mem_kv_kernel,
        grid=(depth, bsz),
        in_specs=[pl.BlockSpec((1, MEM_LEN, D_MODEL), lambda l, b: (b, 0, 0)), per_layer(1, D_MODEL),
                  per_layer(D_MODEL, D_MODEL), per_layer(D_MODEL, D_MODEL)],
        out_specs=[out_blk(MEM_LEN, XATTN_HEADS, XATTN_HEAD_DIM), out_blk(MEM_LEN, XATTN_HEADS, XATTN_HEAD_DIM),
                   out_blk(D_MODEL, MEM_LEN), out_blk(MEM_LEN, D_MODEL)],
        out_shape=[jax.ShapeDtypeStruct((depth, bsz, MEM_LEN, XATTN_HEADS, XATTN_HEAD_DIM), F32)] * 2
        + [jax.ShapeDtypeStruct((depth, bsz, D_MODEL, MEM_LEN), BF16),
           jax.ShapeDtypeStruct((depth, bsz, MEM_LEN, D_MODEL), BF16)],
        scratch_shapes=[pltpu.VMEM((D_MODEL, D_MODEL), BF16)] * 2,
        compiler_params=pltpu.CompilerParams(dimension_semantics=("arbitrary", "arbitrary"),
                                             vmem_limit_bytes=VMEM_LIMIT),
        name="mem_kv",
    )(mem, g, w_xk, w_xv)


def _xattn_prompt_kernel(x_ref, g_ref, wq_ref, kt_ref, vb_ref, wo_ref, o_ref):
    x = x_ref[0]
    h = _rms(x, g_ref[...]).astype(BF16)
    q = (_dot(h, wq_ref[...]) * (XATTN_HEAD_DIM ** -0.5)).astype(BF16)
    heads = []
    for hh in range(XATTN_HEADS):
        lo, hi = hh * XATTN_HEAD_DIM, (hh + 1) * XATTN_HEAD_DIM
        s = _dot(q[:, lo:hi], kt_ref[0, lo:hi, :])
        e = jnp.exp(s - jnp.max(s, axis=-1, keepdims=True))
        p = e * (1.0 / jnp.sum(e, axis=-1, keepdims=True))
        heads.append(_dot(p.astype(BF16), vb_ref[0, :, lo:hi]).astype(BF16))
    o_ref[0] = x + _dot(jnp.concatenate(heads, axis=1), wo_ref[...])


def _xattn_prompt(x, kt, vb, layer, lw, *, tt=512):
    bsz, seq, _ = x.shape
    g = lw['norm_xattn'].reshape(1, -1)
    return pl.pallas_call(
        _xattn_prompt_kernel,
        grid=(bsz, seq // tt),
        in_specs=[pl.BlockSpec((1, tt, D_MODEL), lambda b, t: (b, t, 0)), _const_spec(g.shape),
                  _const_spec(lw['w_xq'].shape),
                  pl.BlockSpec((None, 1, D_MODEL, MEM_LEN), lambda b, t: (layer, b, 0, 0)),
                  pl.BlockSpec((None, 1, MEM_LEN, D_MODEL), lambda b, t: (layer, b, 0, 0)),
                  _const_spec(lw['w_xo'].shape)],
        out_specs=pl.BlockSpec((1, tt, D_MODEL), lambda b, t: (b, t, 0)),
        out_shape=jax.ShapeDtypeStruct(x.shape, F32),
        compiler_params=pltpu.CompilerParams(dimension_semantics=("arbitrary", "arbitrary"),
                                             vmem_limit_bytes=VMEM_LIMIT),
        name="xattn_prompt",
    )(x, g, lw['w_xq'], kt, vb, lw['w_xo'])


def _norm_proj_kernel(x_ref, g_ref, w_ref, o_ref, *, scale):
    o_ref[...] = _dot(_rms(x_ref[...], g_ref[...]).astype(BF16), w_ref[...]) * scale


def _norm_proj(x, g, w, scale):
    g = g.reshape(1, -1)
    out_shape = (x.shape[0], w.shape[1])
    return pl.pallas_call(
        functools.partial(_norm_proj_kernel, scale=scale),
        grid=(1,),
        in_specs=[_const_spec(x.shape), _const_spec(g.shape), _const_spec(w.shape)],
        out_specs=pl.BlockSpec(out_shape, lambda i: (0, 0)),
        out_shape=jax.ShapeDtypeStruct(out_shape, F32),
        compiler_params=pltpu.CompilerParams(dimension_semantics=("arbitrary",), vmem_limit_bytes=VMEM_LIMIT),
        name="norm_proj",
    )(x, g, w)


def _proj_residual_kernel(x_ref, a_ref, w_ref, o_ref):
    o_ref[...] = x_ref[...] + _dot(a_ref[...].astype(BF16), w_ref[...])


def _proj_residual(x, a, w):
    return pl.pallas_call(
        _proj_residual_kernel,
        grid=(1,),
        in_specs=[_const_spec(x.shape), _const_spec(a.shape), _const_spec(w.shape)],
        out_specs=pl.BlockSpec(x.shape, lambda i: (0, 0)),
        out_shape=jax.ShapeDtypeStruct(x.shape, F32),
        compiler_params=pltpu.CompilerParams(dimension_semantics=("arbitrary",), vmem_limit_bytes=VMEM_LIMIT),
        name="proj_residual",
    )(x, a, w)


def _split_head_dim(a):
    lead = a.shape[:-2]
    a = a.reshape(*lead, XATTN_HEADS, 2, XATTN_HEAD_DIM // 2)
    return jnp.swapaxes(a, -3, -2).reshape(*lead, 2 * XATTN_HEADS, XATTN_HEAD_DIM // 2)


def _merge_head_dim(a):
    lead = a.shape[:-2]
    a = a.reshape(*lead, 2, XATTN_HEADS, XATTN_HEAD_DIM // 2)
    return jnp.swapaxes(a, -3, -2).reshape(*lead, XATTN_HEADS, XATTN_HEAD_DIM)


def _attend_rows(q_ref, k_ref, v_ref, o_ref, rows):
    for r in rows:
        part = jnp.sum(k_ref[r] * q_ref[r][None], axis=-1, keepdims=True)
        s = part + pltpu.roll(part, XATTN_HEADS, axis=1)
        e = jnp.exp(s - jnp.max(s, axis=0, keepdims=True))
        p = e * (1.0 / jnp.sum(e, axis=0, keepdims=True))
        o_ref[r] = jnp.sum(p * v_ref[r], axis=0)


def _ffn_kernel(x_ref, g_ref, w1_ref, w2_ref, gf_ref, o_ref, *, final_norm, between_chunks=None):
    x = x_ref[...]
    h = _rms(x, g_ref[...]).astype(BF16)
    y = x
    for i, c in enumerate(range(0, D_FF, FF_CHUNK)):
        if between_chunks is not None:
            between_chunks(i)
        a = jnp.maximum(_dot(h, w1_ref[:, c:c + FF_CHUNK]), 0.0)
        y = y + _dot((a * a).astype(BF16), w2_ref[c:c + FF_CHUNK, :])
    o_ref[...] = _rms(y, gf_ref[...]) if final_norm else y


FFN_ATTEND_INPUTS = 8
FFN_ATTEND_OUTPUTS = 2


def _ffn_attend_kernel(*refs, final_norm, rows, n_cast):
    x_ref, g_ref, w1_ref, w2_ref, gf_ref, q_ref, k_ref, v_ref = refs[:FFN_ATTEND_INPUTS]
    cast_src = refs[FFN_ATTEND_INPUTS:FFN_ATTEND_INPUTS + n_cast]
    o_ref, oa_ref = refs[FFN_ATTEND_INPUTS + n_cast:FFN_ATTEND_INPUTS + n_cast + FFN_ATTEND_OUTPUTS]
    cast_dst = refs[FFN_ATTEND_INPUTS + n_cast + FFN_ATTEND_OUTPUTS:]
    _run_cast_jobs(cast_src, cast_dst)
    n_chunks = D_FF // FF_CHUNK
    assert rows % n_chunks == 0
    per_chunk = rows // n_chunks
    attend = lambda i: _attend_rows(q_ref, k_ref, v_ref, oa_ref, range(i * per_chunk, (i + 1) * per_chunk))
    _ffn_kernel(x_ref, g_ref, w1_ref, w2_ref, gf_ref, o_ref, final_norm=final_norm, between_chunks=attend)


def _ffn(x2d, lw, norm_final, *, final_norm, tm=512, attend=None, cast=()):
    n = x2d.shape[0]
    tm = min(tm, n)
    g = lw['norm_ffn'].reshape(1, -1)
    gf = norm_final.reshape(1, -1)
    in_specs = [pl.BlockSpec((tm, D_MODEL), lambda i: (i, 0)), _const_spec(g.shape),
                _const_spec(lw['w_ff1'].shape), _const_spec(lw['w_ff2'].shape), _const_spec(gf.shape)]
    out_spec = pl.BlockSpec((tm, D_MODEL), lambda i: (i, 0))
    out_shape = jax.ShapeDtypeStruct(x2d.shape, F32)
    params = pltpu.CompilerParams(dimension_semantics=("arbitrary",), vmem_limit_bytes=VMEM_LIMIT)
    if attend is None:
        return pl.pallas_call(
            functools.partial(_ffn_kernel, final_norm=final_norm), grid=(n // tm,), in_specs=in_specs,
            out_specs=out_spec, out_shape=out_shape, compiler_params=params, name="ffn",
        )(x2d, g, lw['w_ff1'], lw['w_ff2'], gf)
    q, cache_k, cache_v, layer = attend
    rows = q.shape[0] // (n // tm)
    assert rows * (n // tm) == q.shape[0]
    q_spec = pl.BlockSpec((rows,) + q.shape[1:], lambda i: (i, 0, 0))
    kv_spec = pl.BlockSpec((None, rows) + cache_k.shape[2:], lambda i: (layer, i, 0, 0, 0))
    jobs = [_cast_job(w, wl, n // tm, lambda i: i) for w, wl in cast]
    outs = pl.pallas_call(
        functools.partial(_ffn_attend_kernel, final_norm=final_norm, rows=rows, n_cast=len(jobs)), grid=(n // tm,),
        in_specs=in_specs + [q_spec, kv_spec, kv_spec] + [j[0] for j in jobs],
        out_specs=[out_spec, q_spec] + [j[1] for j in jobs],
        out_shape=[out_shape, jax.ShapeDtypeStruct(q.shape, F32)] + [j[2] for j in jobs],
        compiler_params=params, name="ffn_attend",
    )(x2d, g, lw['w_ff1'], lw['w_ff2'], gf, q, cache_k, cache_v, *[w for w, _ in cast])
    return outs[0], outs[1], list(outs[FFN_ATTEND_OUTPUTS:])


PROJ_WEIGHTS = ('w_in', 'w_out', 'w_xq', 'w_xo')


def _layer_weights(l, p, proj_bf16):
    tril = jnp.tril(jnp.ones((CHUNK, CHUNK), dtype=bool))
    ws = jnp.where(tril[None], p['gmlp_ws'][l], 0.0)
    pool_bd = jax.scipy.linalg.block_diag(*[p['pool_w'][l, g] for g in range(len(POOL_WINDOWS))])
    lw = {k: p[k][l] for k in ('norm_mix', 'gmlp_ln_g', 'gmlp_ln_b', 'conf_dw', 'conf_dw_b', 'conf_ln_g',
                               'conf_ln_b', 'sc_dw', 'pool_scale', 'mix_out_g', 'norm_xattn', 'norm_ffn')}
    lw.update(proj_bf16)
    lw.update(
        gmlp_wcat=jnp.concatenate([ws[h] for h in range(GMLP_HEADS)], axis=1).astype(BF16),
        gmlp_bias_tile=jnp.repeat(p['gmlp_bs'][l].T, GMLP_HEAD_DIM, axis=1),
        gmlp_w00=jnp.repeat(ws[:, 0, 0], GMLP_HEAD_DIM), gmlp_b0=jnp.repeat(p['gmlp_bs'][l][:, 0], GMLP_HEAD_DIM),
        pool_w_bd=pool_bd.astype(BF16))
    return lw


def kernel(x_prompt, x_sample, mem_prompt, cache_mem_k, cache_mem_v, state_conv_glu, state_conv_short, state_pool, norm_mix, w_in, gmlp_ln_g, gmlp_ln_b, gmlp_ws, gmlp_bs, conf_dw, conf_dw_b, conf_ln_g, conf_ln_b, sc_dw, pool_w, pool_scale, mix_out_g, w_out, norm_xattn, norm_mem, w_xq, w_xk, w_xv, w_xo, norm_ffn, w_ff1, w_ff2, norm_final):
    params = dict(norm_mix=norm_mix, w_in=w_in, gmlp_ln_g=gmlp_ln_g, gmlp_ln_b=gmlp_ln_b, gmlp_ws=gmlp_ws,
                  gmlp_bs=gmlp_bs, conf_dw=conf_dw, conf_dw_b=conf_dw_b, conf_ln_g=conf_ln_g, conf_ln_b=conf_ln_b,
                  sc_dw=sc_dw, pool_w=pool_w, pool_scale=pool_scale, mix_out_g=mix_out_g, w_out=w_out,
                  norm_xattn=norm_xattn, norm_mem=norm_mem, w_xq=w_xq, w_xk=w_xk, w_xv=w_xv, w_xo=w_xo,
                  norm_ffn=norm_ffn, w_ff1=w_ff1, w_ff2=w_ff2)
    depth = w_in.shape[0]
    bsz, seq, _ = x_prompt.shape
    n_s = x_sample.shape[0]
    xp = x_prompt
    xs = x_sample.reshape(n_s, D_MODEL)
    outs = {k: [] for k in ('glu_p', 'glu_s', 'sh_p', 'sh_s', 'pl_p', 'pl_s', 'v_s')}
    mem_k, mem_v, mem_kt, mem_vb = _mem_kv(mem_prompt, norm_mem, w_xk, w_xv)
    cache_k_split = _split_head_dim(cache_mem_k)
    cache_v_split = _split_head_dim(cache_mem_v)
    proj_bf16 = {k: params[k][0].astype(BF16) for k in PROJ_WEIGHTS}
    for l in range(depth):
        lw = _layer_weights(l, params, proj_bf16)
        last = l == depth - 1
        xp, glu_p, sh_p, pool_p, (lw['w_ff1'], lw['w_ff2']) = _mixer_prompt(
            xp, lw, cast=[(params['w_ff1'], l), (params['w_ff2'], l)])
        xp = _xattn_prompt(xp, mem_kt, mem_vb, l, lw)
        outs['glu_p'].append(glu_p); outs['sh_p'].append(sh_p); outs['pl_p'].append(pool_p)
        xs, vn_s, glu_s, sh_s, pool_s = _mixer_sample(xs, state_conv_glu, state_conv_short, state_pool, l, lw)
        q_s = _norm_proj(xs, lw['norm_xattn'], lw['w_xq'], XATTN_HEAD_DIM ** -0.5)
        q_s = _split_head_dim(q_s.reshape(n_s, XATTN_HEADS, XATTN_HEAD_DIM))
        xp, o_s, nxt = _ffn(xp.reshape(bsz * seq, D_MODEL), lw, norm_final, final_norm=last,
                            attend=(q_s, cache_k_split, cache_v_split, l),
                            cast=[] if last else [(params[k], l + 1) for k in PROJ_WEIGHTS])
        proj_bf16 = dict(zip(PROJ_WEIGHTS, nxt))
        xp = xp.reshape(bsz, seq, D_MODEL)
        xs = _proj_residual(xs, _merge_head_dim(o_s).reshape(n_s, D_MODEL), lw['w_xo'])
        xs = _ffn(xs, lw, norm_final, final_norm=last)
        outs['glu_s'].append(glu_s); outs['sh_s'].append(sh_s); outs['pl_s'].append(pool_s)
        outs['v_s'].append(vn_s.reshape(n_s, 1, D_GROUP))
    st = lambda k: jnp.stack(outs[k], axis=0)
    return (xp, xs.reshape(n_s, 1, D_MODEL), mem_k, mem_v, st('glu_p'), st('glu_s'), st('sh_p'), st('sh_s'),
            st('pl_p'), st('pl_s'), st('v_s'))
```

```python
import functools

import jax
import jax.numpy as jnp
from jax import lax
from jax.experimental import pallas as pl
from jax.experimental.pallas import tpu as pltpu

F32 = jnp.float32
BF16 = jnp.bfloat16

D_MODEL = 1024
D_GROUP = 256
N_GROUPS = 4
D_IN_PROJ = 8 * D_GROUP
GMLP_HEADS = 4
GMLP_HEAD_DIM = D_GROUP // GMLP_HEADS
CHUNK = 128
CONF_WIDTH = 31
SC_WIDTH = 3
POOL_WINDOWS = (2, 4, 8, 16)
POOL_GROUP_DIM = D_GROUP // len(POOL_WINDOWS)
POOL_BUF = max(POOL_WINDOWS) - 1
MEM_LEN = 256
XATTN_HEADS = 4
XATTN_HEAD_DIM = D_MODEL // XATTN_HEADS
D_FF = 4 * D_MODEL
PAST_LEN = 16384
EPS = 1e-6

GLU_HALO = 32
SC_HALO = 8
POOL_HALO = 32
LANES = 128
SUBLANES = 8
BF16_SUBLANES = 16
LANE_SLABS = D_GROUP // LANES
CONV_STRIDE = 4
CONV_ROWS = 64
FF_CHUNK = 4096
VMEM_LIMIT = 56 * 1024 * 1024


def _rms(x, g):
    return x * lax.rsqrt(jnp.mean(x * x, axis=-1, keepdims=True) + EPS) * g


def _layer_norm(x, g, b):
    xc = x - jnp.mean(x, axis=-1, keepdims=True)
    return xc * lax.rsqrt(jnp.mean(xc * xc, axis=-1, keepdims=True) + EPS) * g + b


def _sigmoid(x):
    return 0.5 * jnp.tanh(0.5 * x) + 0.5


def _dot(a, b):
    return jnp.dot(a, b, preferred_element_type=F32)


def _const_spec(shape):
    zeros = (0,) * len(shape)
    return pl.BlockSpec(shape, lambda *_: zeros, pipeline_mode=pl.Buffered(1))


def _cast_job(stacked, layer, steps, step_of):
    _, r, c = stacked.shape
    blk = r // steps
    assert blk * steps == r and blk % BF16_SUBLANES == 0
    return (pl.BlockSpec((None, blk, c), lambda *ids: (layer, step_of(*ids), 0)),
            pl.BlockSpec((blk, c), lambda *ids: (step_of(*ids), 0)),
            jax.ShapeDtypeStruct((r, c), BF16))


def _run_cast_jobs(src_refs, dst_refs):
    for src, dst in zip(src_refs, dst_refs):
        dst[...] = src[...].astype(BF16)


def _pool_window_lanes(shape):
    grp = lax.broadcasted_iota(jnp.int32, shape, len(shape) - 1) // POOL_GROUP_DIM
    win = jnp.full(shape, float(POOL_WINDOWS[-1]), F32)
    for g in range(len(POOL_WINDOWS) - 2, -1, -1):
        win = jnp.where(grp == g, float(POOL_WINDOWS[g]), win)
    return grp, win


def _select_by_group(grp, vals):
    out = vals[-1]
    for g in range(len(vals) - 2, -1, -1):
        out = jnp.where(grp == g, vals[g], out)
    return out


def _mix_out(x, y_groups, mog_ref, wout_ref):
    yn = [_rms(y, mog_ref[:, g * D_GROUP:(g + 1) * D_GROUP]).astype(BF16) for g, y in enumerate(y_groups)]
    return x + _dot(jnp.concatenate(yn, axis=1), wout_ref[...])


MIXER_INPUTS = 16
MIXER_OUTPUTS = 4


def _mixer_prompt_kernel(*refs, tt, n_t, sub, n_cast):
    (x_ref, g_ref, win_ref, lng_ref, lnb_ref, wcat_ref, bst_ref, cw_ref, cb_ref, clg_ref,
     clb_ref, sw_ref, pw_ref, ps_ref, mog_ref, wout_ref) = refs[:MIXER_INPUTS]
    cast_src = refs[MIXER_INPUTS:MIXER_INPUTS + n_cast]
    outs = refs[MIXER_INPUTS + n_cast:]
    o_ref, oglu_ref, osh_ref, opool_ref = outs[:MIXER_OUTPUTS]
    cast_dst = outs[MIXER_OUTPUTS:MIXER_OUTPUTS + n_cast]
    glu_buf0, glu_buf1, yb_buf0, yb_buf1, sc_buf, p_buf, s2_buf, s4_buf, s8_buf = outs[MIXER_OUTPUTS + n_cast:]
    _run_cast_jobs(cast_src, cast_dst)
    t = pl.program_id(1)
    glu_buf = (glu_buf0, glu_buf1)
    yb_buf = (yb_buf0, yb_buf1)

    @pl.when(t == 0)
    def _():
        for s in range(LANE_SLABS):
            glu_buf[s][0:GLU_HALO, :] = jnp.zeros((GLU_HALO, LANES), F32)
        sc_buf[0:SC_HALO, :] = jnp.zeros((SC_HALO, D_GROUP), F32)
        p_buf[0:POOL_HALO, :] = jnp.zeros((POOL_HALO, D_GROUP), F32)

    for r0 in range(0, tt, sub):
        _mix_rows(t, r0, sub, x_ref, g_ref, win_ref, lng_ref, lnb_ref, wcat_ref, bst_ref, cw_ref, cb_ref, clg_ref,
                  clb_ref, sw_ref, pw_ref, ps_ref, mog_ref, wout_ref, o_ref,
                  glu_buf, yb_buf, sc_buf, p_buf, s2_buf, s4_buf, s8_buf, tt=tt)

    @pl.when(t == n_t - 1)
    def _():
        oglu_ref[0] = jnp.concatenate(
            [glu_buf[s][GLU_HALO + tt - (CONF_WIDTH - 1):GLU_HALO + tt, :] for s in range(LANE_SLABS)], axis=1)
        osh_ref[0] = sc_buf[SC_HALO + tt - (SC_WIDTH - 1):SC_HALO + tt, :]
        opool_ref[0] = p_buf[POOL_HALO + tt - POOL_BUF:POOL_HALO + tt, :]

    for s in range(LANE_SLABS):
        glu_buf[s][0:GLU_HALO, :] = glu_buf[s][tt:tt + GLU_HALO, :]
    sc_buf[0:SC_HALO, :] = sc_buf[tt:tt + SC_HALO, :]
    p_buf[0:POOL_HALO, :] = p_buf[tt:tt + POOL_HALO, :]


def _mix_rows(t, r0, sub, x_ref, g_ref, win_ref, lng_ref, lnb_ref, wcat_ref, bst_ref, cw_ref, cb_ref, clg_ref,
              clb_ref, sw_ref, pw_ref, ps_ref, mog_ref, wout_ref, o_ref,
              glu_buf, yb_buf, sc_buf, p_buf, s2_buf, s4_buf, s8_buf, *, tt):
    x = x_ref[0, r0:r0 + sub, :]
    z = _dot(_rms(x, g_ref[...]).astype(BF16), win_ref[...])
    zcol = lambda i: z[:, i * D_GROUP:(i + 1) * D_GROUP]
    group_norm = lambda g, y: _rms(y, mog_ref[:, g * D_GROUP:(g + 1) * D_GROUP]).astype(BF16)
    yn = [None] * N_GROUPS

    vn = _layer_norm(zcol(1), lng_ref[...], lnb_ref[...])
    head = lax.broadcasted_iota(jnp.int32, (CHUNK, D_GROUP), 1) // GMLP_HEAD_DIM
    gates = []
    for c in range(sub // CHUNK):
        vc = vn[c * CHUNK:(c + 1) * CHUNK]
        stack = jnp.concatenate([jnp.where(head == hh, vc, 0.0) for hh in range(GMLP_HEADS)], axis=0)
        gates.append(_dot(wcat_ref[...], stack.astype(BF16)) + bst_ref[...])
    yn[0] = group_norm(0, zcol(0) * jnp.concatenate(gates, axis=0))

    first = GLU_HALO - (CONF_WIDTH - 1)
    span = SUBLANES * CONV_STRIDE
    glu = zcol(2) * _sigmoid(zcol(3))
    for s in range(LANE_SLABS):
        glu_buf[s][GLU_HALO + r0:GLU_HALO + r0 + sub, :] = glu[:, s * LANES:(s + 1) * LANES]
    for c0 in range(r0, r0 + sub, CONV_ROWS):
        starts = [c0 + (i // CONV_STRIDE) * span + i % CONV_STRIDE for i in range(CONV_ROWS // SUBLANES)]
        accs = [[None] * len(starts) for _ in range(LANE_SLABS)]
        for k in range(CONF_WIDTH):
            for s in range(LANE_SLABS):
                wk = jnp.broadcast_to(cw_ref[k:k + 1, s * LANES:(s + 1) * LANES], (SUBLANES, LANES))
                for i, t0 in enumerate(starts):
                    term = wk * glu_buf[s][pl.ds(first + t0 + k, SUBLANES, stride=CONV_STRIDE), :]
                    accs[s][i] = term if k == 0 else accs[s][i] + term
        conv = jnp.concatenate([jnp.concatenate(a, axis=0) for a in accs], axis=1)
        ln = _layer_norm(conv + cb_ref[...], clg_ref[...], clb_ref[...])
        yb = ln * _sigmoid(ln)
        for s in range(LANE_SLABS):
            for i, t0 in enumerate(starts):
                yb_buf[s][pl.ds(t0, SUBLANES, stride=CONV_STRIDE), :] = (
                    yb[i * SUBLANES:(i + 1) * SUBLANES, s * LANES:(s + 1) * LANES])
    yn[1] = group_norm(1, jnp.concatenate([yb_buf[s][r0:r0 + sub, :] for s in range(LANE_SLABS)], axis=1))

    sxc = zcol(5) * zcol(6)
    lo = SC_HALO + r0
    sc_buf[lo:lo + sub, :] = sxc
    conv_c = (sw_ref[0:1, :] * sc_buf[lo - 2:lo - 2 + sub, :]
              + sw_ref[1:2, :] * sc_buf[lo - 1:lo - 1 + sub, :]
              + sw_ref[2:3, :] * sxc)
    yn[2] = group_norm(2, zcol(4) * conv_c)

    pool_x = zcol(7)
    lo = POOL_HALO + r0
    end = lo + sub
    p_buf[lo:end, :] = pool_x
    lo2, lo4, lo8 = (8, 16, 24) if r0 == 0 else (lo, lo, lo)
    s2_buf[lo2:end, :] = p_buf[lo2:end, :] + p_buf[lo2 - 1:end - 1, :]
    s4_buf[lo4:end, :] = s2_buf[lo4:end, :] + s2_buf[lo4 - 2:end - 2, :]
    s8_buf[lo8:end, :] = s4_buf[lo8:end, :] + s4_buf[lo8 - 4:end - 4, :]
    s16 = s8_buf[lo:end, :] + s8_buf[lo - 8:end - 8, :]
    grp = _pool_window_lanes((sub, D_GROUP))[0]
    ssum = _select_by_group(grp, [s2_buf[lo:end, :], s4_buf[lo:end, :], s8_buf[lo:end, :], s16])
    head_rows = POOL_BUF + 1 if r0 == 0 else 0
    inv_win = _select_by_group(_pool_window_lanes((sub - head_rows, D_GROUP))[0], [1.0 / w for w in POOL_WINDOWS])
    mean = ssum[head_rows:] * inv_win
    if head_rows:
        pos = t * tt + lax.broadcasted_iota(jnp.int32, (head_rows, D_GROUP), 0)
        cnt = jnp.minimum(_pool_window_lanes((head_rows, D_GROUP))[1], (pos + 1).astype(F32))
        mean = jnp.concatenate([ssum[:head_rows] / cnt, mean], axis=0)
    pooled = mean - pool_x
    yn[3] = group_norm(3, _dot(pooled.astype(BF16), pw_ref[...]) * ps_ref[...])

    o_ref[0, r0:r0 + sub, :] = x + _dot(jnp.concatenate(yn, axis=1), wout_ref[...])


def _mixer_prompt(x, lw, *, tt=1024, sub=1024, cast=()):
    bsz, seq, _ = x.shape
    n_t = seq // tt
    row = lambda a: a.reshape(1, -1)
    small = [row(lw['norm_mix']), lw['w_in'], row(lw['gmlp_ln_g']), row(lw['gmlp_ln_b']), lw['gmlp_wcat'],
             lw['gmlp_bias_tile'], lw['conf_dw'], row(lw['conf_dw_b']), row(lw['conf_ln_g']), row(lw['conf_ln_b']),
             lw['sc_dw'], lw['pool_w_bd'], row(lw['pool_scale']), row(lw['mix_out_g']), lw['w_out']]
    assert 1 + len(small) == MIXER_INPUTS
    jobs = [_cast_job(w, layer, bsz * n_t, lambda b, t: b * n_t + t) for w, layer in cast]
    state_spec = lambda rows: pl.BlockSpec((1, rows, D_GROUP), lambda b, t: (b, 0, 0))
    outs = pl.pallas_call(
        functools.partial(_mixer_prompt_kernel, tt=tt, n_t=n_t, sub=sub, n_cast=len(jobs)),
        grid=(bsz, n_t),
        in_specs=[pl.BlockSpec((1, tt, D_MODEL), lambda b, t: (b, t, 0))] + [_const_spec(a.shape) for a in small]
        + [j[0] for j in jobs],
        out_specs=[pl.BlockSpec((1, tt, D_MODEL), lambda b, t: (b, t, 0)),
                   state_spec(CONF_WIDTH - 1), state_spec(SC_WIDTH - 1), state_spec(POOL_BUF)] + [j[1] for j in jobs],
        out_shape=[jax.ShapeDtypeStruct(x.shape, F32),
                   jax.ShapeDtypeStruct((bsz, CONF_WIDTH - 1, D_GROUP), F32),
                   jax.ShapeDtypeStruct((bsz, SC_WIDTH - 1, D_GROUP), F32),
                   jax.ShapeDtypeStruct((bsz, POOL_BUF, D_GROUP), F32)] + [j[2] for j in jobs],
        scratch_shapes=[pltpu.VMEM((GLU_HALO + tt, LANES), F32)] * LANE_SLABS + [pltpu.VMEM((tt, LANES), F32)] * LANE_SLABS
        + [pltpu.VMEM((SC_HALO + tt, D_GROUP), F32)]
        + [pltpu.VMEM((POOL_HALO + tt, D_GROUP), F32)] * 4,
        compiler_params=pltpu.CompilerParams(dimension_semantics=("arbitrary", "arbitrary"),
                                             vmem_limit_bytes=VMEM_LIMIT),
        name="mixer_prompt",
    )(x, *small, *[w for w, _ in cast])
    return (*outs[:MIXER_OUTPUTS], list(outs[MIXER_OUTPUTS:]))


def _mixer_sample_kernel(x_ref, stg_ref, sts_ref, stp_ref, g_ref, win_ref, lng_ref, lnb_ref, w00_ref, b0_ref,
                         cw_ref, cb_ref, clg_ref, clb_ref, sw_ref, pw_ref, ps_ref, mog_ref, wout_ref,
                         o_ref, ovn_ref, oglu_ref, osh_ref, opool_ref):
    x = x_ref[...]
    h = _rms(x, g_ref[...]).astype(BF16)
    z = _dot(h, win_ref[...])
    u, v, glu_a, glu_g, sc_b, sc_c, sc_x, pool_x = [z[:, i * D_GROUP:(i + 1) * D_GROUP] for i in range(8)]
    hist = lambda ref, k: ref[:, k, :]

    def push(new_ref, old_ref, row):
        n_hist = old_ref.shape[1]
        new_ref[:, 0:n_hist - 1, :] = old_ref[:, 1:n_hist, :]
        new_ref[:, n_hist - 1, :] = row

    vn = _layer_norm(v, lng_ref[...], lnb_ref[...])
    ovn_ref[...] = vn
    y_a = u * (w00_ref[...] * vn + b0_ref[...])

    glu = glu_a * _sigmoid(glu_g)
    n_hist = CONF_WIDTH - 1
    acc = cw_ref[n_hist:n_hist + 1, :] * glu
    for k in range(n_hist):
        acc = acc + cw_ref[k:k + 1, :] * hist(stg_ref, k)
    ln = _layer_norm(acc + cb_ref[...], clg_ref[...], clb_ref[...])
    y_b = ln * _sigmoid(ln)
    push(oglu_ref, stg_ref, glu)

    sxc = sc_c * sc_x
    y_c = sc_b * (sw_ref[0:1, :] * hist(sts_ref, 0) + sw_ref[1:2, :] * hist(sts_ref, 1) + sw_ref[2:3, :] * sxc)
    push(osh_ref, sts_ref, sxc)

    run = pool_x
    sums = []
    back = 0
    for w in POOL_WINDOWS:
        while back < w - 1:
            run = run + hist(stp_ref, POOL_BUF - 1 - back)
            back += 1
        sums.append(run)
    grp, win = _pool_window_lanes(pool_x.shape)
    cnt = jnp.minimum(win, float(PAST_LEN + 1))
    pooled = _select_by_group(grp, sums) / cnt - pool_x
    y_d = _dot(pooled.astype(BF16), pw_ref[...]) * ps_ref[...]
    push(opool_ref, stp_ref, pool_x)

    o_ref[...] = _mix_out(x, [y_a, y_b, y_c, y_d], mog_ref, wout_ref)


def _mixer_sample(xs, st_glu, st_sh, st_pool, layer, lw):
    n = xs.shape[0]
    row = lambda a: a.reshape(1, -1)
    states = [st_glu, st_sh, st_pool]
    consts = [row(lw['norm_mix']), lw['w_in'], row(lw['gmlp_ln_g']), row(lw['gmlp_ln_b']), row(lw['gmlp_w00']),
              row(lw['gmlp_b0']), lw['conf_dw'], row(lw['conf_dw_b']), row(lw['conf_ln_g']), row(lw['conf_ln_b']),
              lw['sc_dw'], lw['pool_w_bd'], row(lw['pool_scale']), row(lw['mix_out_g']), lw['w_out']]
    state_spec = lambda s: pl.BlockSpec((None,) + s.shape[1:], lambda i: (layer, 0, 0, 0), pipeline_mode=pl.Buffered(1))
    out_shapes = [(n, D_MODEL), (n, D_GROUP)] + [s.shape[1:] for s in states]
    return pl.pallas_call(
        _mixer_sample_kernel,
        grid=(1,),
        in_specs=[_const_spec(xs.shape)] + [state_spec(s) for s in states] + [_const_spec(a.shape) for a in consts],
        out_specs=[pl.BlockSpec(s, lambda i, nd=len(s): (0,) * nd) for s in out_shapes],
        out_shape=[jax.ShapeDtypeStruct(s, F32) for s in out_shapes],
        compiler_params=pltpu.CompilerParams(dimension_semantics=("arbitrary",), vmem_limit_bytes=VMEM_LIMIT),
        name="mixer_sample",
    )(xs, *states, *consts)


def _mem_kv_kernel(m_ref, g_ref, wk_ref, wv_ref, ok_ref, ov_ref, okt_ref, ovb_ref, wk_bf16, wv_bf16):
    @pl.when(pl.program_id(1) == 0)
    def _():
        wk_bf16[...] = wk_ref[...].astype(BF16)
        wv_bf16[...] = wv_ref[...].astype(BF16)

    m = _rms(m_ref[0], g_ref[...]).astype(BF16)
    k = _dot(m, wk_bf16[...])
    v = _dot(m, wv_bf16[...])
    for h in range(XATTN_HEADS):
        ok_ref[0, :, h, :] = k[:, h * XATTN_HEAD_DIM:(h + 1) * XATTN_HEAD_DIM]
        ov_ref[0, :, h, :] = v[:, h * XATTN_HEAD_DIM:(h + 1) * XATTN_HEAD_DIM]
    okt_ref[0] = k.T.astype(BF16)
    ovb_ref[0] = v.astype(BF16)


def _mem_kv(mem, norm_mem, w_xk, w_xv):
    depth = w_xk.shape[0]
    bsz = mem.shape[0]
    g = norm_mem.reshape(depth, 1, D_MODEL)
    per_layer = lambda *shape: pl.BlockSpec((None,) + shape, lambda l, b: (l,) + (0,) * len(shape))
    out_blk = lambda *shape: pl.BlockSpec((None, 1) + shape, lambda l, b: (l, b) + (0,) * len(shape))
    return pl.pallas_call(
        _mem_kv_kernel,
        grid=(depth, bsz),
        in_specs=[pl.BlockSpec((1, MEM_LEN, D_MODEL), lambda l, b: (b, 0, 0)), per_layer(1, D_MODEL),
                  per_layer(D_MODEL, D_MODEL), per_layer(D_MODEL, D_MODEL)],
        out_specs=[out_blk(MEM_LEN, XATTN_HEADS, XATTN_HEAD_DIM), out_blk(MEM_LEN, XATTN_HEADS, XATTN_HEAD_DIM),
                   out_blk(D_MODEL, MEM_LEN), out_blk(MEM_LEN, D_MODEL)],
        out_shape=[jax.ShapeDtypeStruct((depth, bsz, MEM_LEN, XATTN_HEADS, XATTN_HEAD_DIM), F32)] * 2
        + [jax.ShapeDtypeStruct((depth, bsz, D_MODEL, MEM_LEN), BF16),
           jax.ShapeDtypeStruct((depth, bsz, MEM_LEN, D_MODEL), BF16)],
        scratch_shapes=[pltpu.VMEM((D_MODEL, D_MODEL), BF16)] * 2,
        compiler_params=pltpu.CompilerParams(dimension_semantics=("arbitrary", "arbitrary"),
                                             vmem_limit_bytes=VMEM_LIMIT),
        name="mem_kv",
    )(mem, g, w_xk, w_xv)


def _xattn_prompt_kernel(x_ref, g_ref, wq_ref, kt_ref, vb_ref, wo_ref, o_ref):
    x = x_ref[0]
    h = _rms(x, g_ref[...]).astype(BF16)
    q = (_dot(h, wq_ref[...]) * (XATTN_HEAD_DIM ** -0.5)).astype(BF16)
    heads = []
    for hh in range(XATTN_HEADS):
        lo, hi = hh * XATTN_HEAD_DIM, (hh + 1) * XATTN_HEAD_DIM
        s = _dot(q[:, lo:hi], kt_ref[0, lo:hi, :])
        e = jnp.exp(s - jnp.max(s, axis=-1, keepdims=True))
        p = e * (1.0 / jnp.sum(e, axis=-1, keepdims=True))
        heads.append(_dot(p.astype(BF16), vb_ref[0, :, lo:hi]).astype(BF16))
    o_ref[0] = x + _dot(jnp.concatenate(heads, axis=1), wo_ref[...])


def _xattn_prompt(x, kt, vb, layer, lw, *, tt=1024):
    bsz, seq, _ = x.shape
    g = lw['norm_xattn'].reshape(1, -1)
    return pl.pallas_call(
        _xattn_prompt_kernel,
        grid=(bsz, seq // tt),
        in_specs=[pl.BlockSpec((1, tt, D_MODEL), lambda b, t: (b, t, 0)), _const_spec(g.shape),
                  _const_spec(lw['w_xq'].shape),
                  pl.BlockSpec((None, 1, D_MODEL, MEM_LEN), lambda b, t: (layer, b, 0, 0)),
                  pl.BlockSpec((None, 1, MEM_LEN, D_MODEL), lambda b, t: (layer, b, 0, 0)),
                  _const_spec(lw['w_xo'].shape)],
        out_specs=pl.BlockSpec((1, tt, D_MODEL), lambda b, t: (b, t, 0)),
        out_shape=jax.ShapeDtypeStruct(x.shape, F32),
        compiler_params=pltpu.CompilerParams(dimension_semantics=("arbitrary", "arbitrary"),
                                             vmem_limit_bytes=VMEM_LIMIT),
        name="xattn_prompt",
    )(x, g, lw['w_xq'], kt, vb, lw['w_xo'])


def _norm_proj_kernel(x_ref, g_ref, w_ref, o_ref, *, scale):
    o_ref[...] = _dot(_rms(x_ref[...], g_ref[...]).astype(BF16), w_ref[...]) * scale


def _norm_proj(x, g, w, scale):
    g = g.reshape(1, -1)
    out_shape = (x.shape[0], w.shape[1])
    return pl.pallas_call(
        functools.partial(_norm_proj_kernel, scale=scale),
        grid=(1,),
        in_specs=[_const_spec(x.shape), _const_spec(g.shape), _const_spec(w.shape)],
        out_specs=pl.BlockSpec(out_shape, lambda i: (0, 0)),
        out_shape=jax.ShapeDtypeStruct(out_shape, F32),
        compiler_params=pltpu.CompilerParams(dimension_semantics=("arbitrary",), vmem_limit_bytes=VMEM_LIMIT),
        name="norm_proj",
    )(x, g, w)


def _proj_residual_kernel(x_ref, a_ref, w_ref, o_ref):
    o_ref[...] = x_ref[...] + _dot(a_ref[...].astype(BF16), w_ref[...])


def _proj_residual(x, a, w):
    return pl.pallas_call(
        _proj_residual_kernel,
        grid=(1,),
        in_specs=[_const_spec(x.shape), _const_spec(a.shape), _const_spec(w.shape)],
        out_specs=pl.BlockSpec(x.shape, lambda i: (0, 0)),
        out_shape=jax.ShapeDtypeStruct(x.shape, F32),
        compiler_params=pltpu.CompilerParams(dimension_semantics=("arbitrary",), vmem_limit_bytes=VMEM_LIMIT),
        name="proj_residual",
    )(x, a, w)


def _split_head_dim(a):
    lead = a.shape[:-2]
    a = a.reshape(*lead, XATTN_HEADS, 2, XATTN_HEAD_DIM // 2)
    return jnp.swapaxes(a, -3, -2).reshape(*lead, 2 * XATTN_HEADS, XATTN_HEAD_DIM // 2)


def _merge_head_dim(a):
    lead = a.shape[:-2]
    a = a.reshape(*lead, 2, XATTN_HEADS, XATTN_HEAD_DIM // 2)
    return jnp.swapaxes(a, -3, -2).reshape(*lead, XATTN_HEADS, XATTN_HEAD_DIM)


def _attend_rows(q_ref, k_ref, v_ref, o_ref, rows):
    for r in rows:
        part = jnp.sum(k_ref[r] * q_ref[r][None], axis=-1, keepdims=True)
        s = part + pltpu.roll(part, XATTN_HEADS, axis=1)
        e = jnp.exp(s - jnp.max(s, axis=0, keepdims=True))
        p = e * (1.0 / jnp.sum(e, axis=0, keepdims=True))
        o_ref[r] = jnp.sum(p * v_ref[r], axis=0)


def _ffn_kernel(x_ref, g_ref, w1_ref, w2_ref, gf_ref, o_ref, *, final_norm, between_chunks=None):
    x = x_ref[...]
    h = _rms(x, g_ref[...]).astype(BF16)
    y = x
    for i, c in enumerate(range(0, D_FF, FF_CHUNK)):
        if between_chunks is not None:
            between_chunks(i)
        a = jnp.maximum(_dot(h, w1_ref[:, c:c + FF_CHUNK]), 0.0)
        y = y + _dot((a * a).astype(BF16), w2_ref[c:c + FF_CHUNK, :])
    o_ref[...] = _rms(y, gf_ref[...]) if final_norm else y


FFN_ATTEND_INPUTS = 8
FFN_ATTEND_OUTPUTS = 2


def _ffn_attend_kernel(*refs, final_norm, rows, n_cast):
    x_ref, g_ref, w1_ref, w2_ref, gf_ref, q_ref, k_ref, v_ref = refs[:FFN_ATTEND_INPUTS]
    cast_src = refs[FFN_ATTEND_INPUTS:FFN_ATTEND_INPUTS + n_cast]
    o_ref, oa_ref = refs[FFN_ATTEND_INPUTS + n_cast:FFN_ATTEND_INPUTS + n_cast + FFN_ATTEND_OUTPUTS]
    cast_dst = refs[FFN_ATTEND_INPUTS + n_cast + FFN_ATTEND_OUTPUTS:]
    _run_cast_jobs(cast_src, cast_dst)
    n_chunks = D_FF // FF_CHUNK
    assert rows % n_chunks == 0
    per_chunk = rows // n_chunks
    attend = lambda i: _attend_rows(q_ref, k_ref, v_ref, oa_ref, range(i * per_chunk, (i + 1) * per_chunk))
    _ffn_kernel(x_ref, g_ref, w1_ref, w2_ref, gf_ref, o_ref, final_norm=final_norm, between_chunks=attend)


def _ffn(x2d, lw, norm_final, *, final_norm, tm=512, attend=None, cast=()):
    n = x2d.shape[0]
    tm = min(tm, n)
    g = lw['norm_ffn'].reshape(1, -1)
    gf = norm_final.reshape(1, -1)
    in_specs = [pl.BlockSpec((tm, D_MODEL), lambda i: (i, 0)), _const_spec(g.shape),
                _const_spec(lw['w_ff1'].shape), _const_spec(lw['w_ff2'].shape), _const_spec(gf.shape)]
    out_spec = pl.BlockSpec((tm, D_MODEL), lambda i: (i, 0))
    out_shape = jax.ShapeDtypeStruct(x2d.shape, F32)
    params = pltpu.CompilerParams(dimension_semantics=("arbitrary",), vmem_limit_bytes=VMEM_LIMIT)
    if attend is None:
        return pl.pallas_call(
            functools.partial(_ffn_kernel, final_norm=final_norm), grid=(n // tm,), in_specs=in_specs,
            out_specs=out_spec, out_shape=out_shape, compiler_params=params, name="ffn",
        )(x2d, g, lw['w_ff1'], lw['w_ff2'], gf)
    q, cache_k, cache_v, layer = attend
    rows = q.shape[0] // (n // tm)
    assert rows * (n // tm) == q.shape[0]
    q_spec = pl.BlockSpec((rows,) + q.shape[1:], lambda i: (i, 0, 0))
    kv_spec = pl.BlockSpec((None, rows) + cache_k.shape[2:], lambda i: (layer, i, 0, 0, 0))
    jobs = [_cast_job(w, wl, n // tm, lambda i: i) for w, wl in cast]
    outs = pl.pallas_call(
        functools.partial(_ffn_attend_kernel, final_norm=final_norm, rows=rows, n_cast=len(jobs)), grid=(n // tm,),
        in_specs=in_specs + [q_spec, kv_spec, kv_spec] + [j[0] for j in jobs],
        out_specs=[out_spec, q_spec] + [j[1] for j in jobs],
        out_shape=[out_shape, jax.ShapeDtypeStruct(q.shape, F32)] + [j[2] for j in jobs],
        compiler_params=params, name="ffn_attend",
    )(x2d, g, lw['w_ff1'], lw['w_ff2'], gf, q, cache_k, cache_v, *[w for w, _ in cast])
    return outs[0], outs[1], list(outs[FFN_ATTEND_OUTPUTS:])


PROJ_WEIGHTS = ('w_in', 'w_out', 'w_xq', 'w_xo')


def _layer_weights(l, p, proj_bf16):
    tril = jnp.tril(jnp.ones((CHUNK, CHUNK), dtype=bool))
    ws = jnp.where(tril[None], p['gmlp_ws'][l], 0.0)
    pool_bd = jax.scipy.linalg.block_diag(*[p['pool_w'][l, g] for g in range(len(POOL_WINDOWS))])
    lw = {k: p[k][l] for k in ('norm_mix', 'gmlp_ln_g', 'gmlp_ln_b', 'conf_dw', 'conf_dw_b', 'conf_ln_g',
                               'conf_ln_b', 'sc_dw', 'pool_scale', 'mix_out_g', 'norm_xattn', 'norm_ffn')}
    lw.update(proj_bf16)
    lw.update(
        gmlp_wcat=jnp.concatenate([ws[h] for h in range(GMLP_HEADS)], axis=1).astype(BF16),
        gmlp_bias_tile=jnp.repeat(p['gmlp_bs'][l].T, GMLP_HEAD_DIM, axis=1),
        gmlp_w00=jnp.repeat(ws[:, 0, 0], GMLP_HEAD_DIM), gmlp_b0=jnp.repeat(p['gmlp_bs'][l][:, 0], GMLP_HEAD_DIM),
        pool_w_bd=pool_bd.astype(BF16))
    return lw


def kernel(x_prompt, x_sample, mem_prompt, cache_mem_k, cache_mem_v, state_conv_glu, state_conv_short, state_pool, norm_mix, w_in, gmlp_ln_g, gmlp_ln_b, gmlp_ws, gmlp_bs, conf_dw, conf_dw_b, conf_ln_g, conf_ln_b, sc_dw, pool_w, pool_scale, mix_out_g, w_out, norm_xattn, norm_mem, w_xq, w_xk, w_xv, w_xo, norm_ffn, w_ff1, w_ff2, norm_final):
    params = dict(norm_mix=norm_mix, w_in=w_in, gmlp_ln_g=gmlp_ln_g, gmlp_ln_b=gmlp_ln_b, gmlp_ws=gmlp_ws,
                  gmlp_bs=gmlp_bs, conf_dw=conf_dw, conf_dw_b=conf_dw_b, conf_ln_g=conf_ln_g, conf_ln_b=conf_ln_b,
                  sc_dw=sc_dw, pool_w=pool_w, pool_scale=pool_scale, mix_out_g=mix_out_g, w_out=w_out,
                  norm_xattn=norm_xattn, norm_mem=norm_mem, w_xq=w_xq, w_xk=w_xk, w_xv=w_xv, w_xo=w_xo,
                  norm_ffn=norm_ffn, w_ff1=w_ff1, w_ff2=w_ff2)
    depth = w_in.shape[0]
    bsz, seq, _ = x_prompt.shape
    n_s = x_sample.shape[0]
    xp = x_prompt
    xs = x_sample.reshape(n_s, D_MODEL)
    outs = {k: [] for k in ('glu_p', 'glu_s', 'sh_p', 'sh_s', 'pl_p', 'pl_s', 'v_s')}
    mem_k, mem_v, mem_kt, mem_vb = _mem_kv(mem_prompt, norm_mem, w_xk, w_xv)
    cache_k_split = _split_head_dim(cache_mem_k)
    cache_v_split = _split_head_dim(cache_mem_v)
    proj_bf16 = {k: params[k][0].astype(BF16) for k in PROJ_WEIGHTS}
    for l in range(depth):
        lw = _layer_weights(l, params, proj_bf16)
        last = l == depth - 1
        xp, glu_p, sh_p, pool_p, (lw['w_ff1'], lw['w_ff2']) = _mixer_prompt(
            xp, lw, cast=[(params['w_ff1'], l), (params['w_ff2'], l)])
        xp = _xattn_prompt(xp, mem_kt, mem_vb, l, lw)
        outs['glu_p'].append(glu_p); outs['sh_p'].append(sh_p); outs['pl_p'].append(pool_p)
        xs, vn_s, glu_s, sh_s, pool_s = _mixer_sample(xs, state_conv_glu, state_conv_short, state_pool, l, lw)
        q_s = _norm_proj(xs, lw['norm_xattn'], lw['w_xq'], XATTN_HEAD_DIM ** -0.5)
        q_s = _split_head_dim(q_s.reshape(n_s, XATTN_HEADS, XATTN_HEAD_DIM))
        xp, o_s, nxt = _ffn(xp.reshape(bsz * seq, D_MODEL), lw, norm_final, final_norm=last,
                            attend=(q_s, cache_k_split, cache_v_split, l),
                            cast=[] if last else [(params[k], l + 1) for k in PROJ_WEIGHTS])
        proj_bf16 = dict(zip(PROJ_WEIGHTS, nxt))
        xp = xp.reshape(bsz, seq, D_MODEL)
        xs = _proj_residual(xs, _merge_head_dim(o_s).reshape(n_s, D_MODEL), lw['w_xo'])
        xs = _ffn(xs, lw, norm_final, final_norm=last)
        outs['glu_s'].append(glu_s); outs['sh_s'].append(sh_s); outs['pl_s'].append(pool_s)
        outs['v_s'].append(vn_s.reshape(n_s, 1, D_GROUP))
    st = lambda k: jnp.stack(outs[k], axis=0)
    return (xp, xs.reshape(n_s, 1, D_MODEL), mem_k, mem_v, st('glu_p'), st('glu_s'), st('sh_p'), st('sh_s'),
            st('pl_p'), st('pl_s'), st('v_s'))
```

```python
import functools

import jax
import jax.numpy as jnp
from jax import lax
from jax.experimental import pallas as pl
from jax.experimental.pallas import tpu as pltpu

F32 = jnp.float32
BF16 = jnp.bfloat16

D_MODEL = 1024
D_GROUP = 256
N_GROUPS = 4
D_IN_PROJ = 8 * D_GROUP
GMLP_HEADS = 4
GMLP_HEAD_DIM = D_GROUP // GMLP_HEADS
CHUNK = 128
CONF_WIDTH = 31
SC_WIDTH = 3
POOL_WINDOWS = (2, 4, 8, 16)
POOL_GROUP_DIM = D_GROUP // len(POOL_WINDOWS)
POOL_BUF = max(POOL_WINDOWS) - 1
MEM_LEN = 256
XATTN_HEADS = 4
XATTN_HEAD_DIM = D_MODEL // XATTN_HEADS
D_FF = 4 * D_MODEL
PAST_LEN = 16384
EPS = 1e-6

GLU_HALO = 32
SC_HALO = 8
POOL_HALO = 32
LANES = 128
SUBLANES = 8
BF16_SUBLANES = 16
LANE_SLABS = D_GROUP // LANES
CONV_STRIDE = 4
CONV_ROWS = 64
FF_CHUNK = 4096
VMEM_LIMIT = 56 * 1024 * 1024


def _rms(x, g):
    return x * lax.rsqrt(jnp.mean(x * x, axis=-1, keepdims=True) + EPS) * g


def _layer_norm(x, g, b):
    xc = x - jnp.mean(x, axis=-1, keepdims=True)
    return xc * lax.rsqrt(jnp.mean(xc * xc, axis=-1, keepdims=True) + EPS) * g + b


def _sigmoid(x):
    return 0.5 * jnp.tanh(0.5 * x) + 0.5


def _dot(a, b):
    return jnp.dot(a, b, preferred_element_type=F32)


def _const_spec(shape):
    zeros = (0,) * len(shape)
    return pl.BlockSpec(shape, lambda *_: zeros, pipeline_mode=pl.Buffered(1))


def _cast_job(stacked, layer, steps, step_of):
    _, r, c = stacked.shape
    blk = r // steps
    assert blk * steps == r and blk % BF16_SUBLANES == 0
    return (pl.BlockSpec((None, blk, c), lambda *ids: (layer, step_of(*ids), 0)),
            pl.BlockSpec((blk, c), lambda *ids: (step_of(*ids), 0)),
            jax.ShapeDtypeStruct((r, c), BF16))


def _run_cast_jobs(src_refs, dst_refs):
    for src, dst in zip(src_refs, dst_refs):
        dst[...] = src[...].astype(BF16)


def _pool_window_lanes(shape):
    grp = lax.broadcasted_iota(jnp.int32, shape, len(shape) - 1) // POOL_GROUP_DIM
    win = jnp.full(shape, float(POOL_WINDOWS[-1]), F32)
    for g in range(len(POOL_WINDOWS) - 2, -1, -1):
        win = jnp.where(grp == g, float(POOL_WINDOWS[g]), win)
    return grp, win


def _select_by_group(grp, vals):
    out = vals[-1]
    for g in range(len(vals) - 2, -1, -1):
        out = jnp.where(grp == g, vals[g], out)
    return out


def _mix_out(x, y_groups, mog_ref, wout_ref):
    yn = [_rms(y, mog_ref[:, g * D_GROUP:(g + 1) * D_GROUP]).astype(BF16) for g, y in enumerate(y_groups)]
    return x + _dot(jnp.concatenate(yn, axis=1), wout_ref[...])


MIXER_INPUTS = 16
MIXER_OUTPUTS = 4


def _mixer_prompt_kernel(*refs, tt, n_t, sub, n_cast):
    (x_ref, g_ref, win_ref, lng_ref, lnb_ref, wcat_ref, bst_ref, cw_ref, cb_ref, clg_ref,
     clb_ref, sw_ref, pw_ref, ps_ref, mog_ref, wout_ref) = refs[:MIXER_INPUTS]
    cast_src = refs[MIXER_INPUTS:MIXER_INPUTS + n_cast]
    outs = refs[MIXER_INPUTS + n_cast:]
    o_ref, oglu_ref, osh_ref, opool_ref = outs[:MIXER_OUTPUTS]
    cast_dst = outs[MIXER_OUTPUTS:MIXER_OUTPUTS + n_cast]
    glu_buf0, glu_buf1, yb_buf0, yb_buf1, sc_buf, p_buf, s2_buf, s4_buf, s8_buf = outs[MIXER_OUTPUTS + n_cast:]
    _run_cast_jobs(cast_src, cast_dst)
    t = pl.program_id(1)
    glu_buf = (glu_buf0, glu_buf1)
    yb_buf = (yb_buf0, yb_buf1)

    @pl.when(t == 0)
    def _():
        for s in range(LANE_SLABS):
            glu_buf[s][0:GLU_HALO, :] = jnp.zeros((GLU_HALO, LANES), F32)
        sc_buf[0:SC_HALO, :] = jnp.zeros((SC_HALO, D_GROUP), F32)
        p_buf[0:POOL_HALO, :] = jnp.zeros((POOL_HALO, D_GROUP), F32)

    for r0 in range(0, tt, sub):
        _mix_rows(t, r0, sub, x_ref, g_ref, win_ref, lng_ref, lnb_ref, wcat_ref, bst_ref, cw_ref, cb_ref, clg_ref,
                  clb_ref, sw_ref, pw_ref, ps_ref, mog_ref, wout_ref, o_ref,
                  glu_buf, yb_buf, sc_buf, p_buf, s2_buf, s4_buf, s8_buf, tt=tt)

    @pl.when(t == n_t - 1)
    def _():
        oglu_ref[0] = jnp.concatenate(
            [glu_buf[s][GLU_HALO + tt - (CONF_WIDTH - 1):GLU_HALO + tt, :] for s in range(LANE_SLABS)], axis=1)
        osh_ref[0] = sc_buf[SC_HALO + tt - (SC_WIDTH - 1):SC_HALO + tt, :]
        opool_ref[0] = p_buf[POOL_HALO + tt - POOL_BUF:POOL_HALO + tt, :]

    for s in range(LANE_SLABS):
        glu_buf[s][0:GLU_HALO, :] = glu_buf[s][tt:tt + GLU_HALO, :]
    sc_buf[0:SC_HALO, :] = sc_buf[tt:tt + SC_HALO, :]
    p_buf[0:POOL_HALO, :] = p_buf[tt:tt + POOL_HALO, :]


def _mix_rows(t, r0, sub, x_ref, g_ref, win_ref, lng_ref, lnb_ref, wcat_ref, bst_ref, cw_ref, cb_ref, clg_ref,
              clb_ref, sw_ref, pw_ref, ps_ref, mog_ref, wout_ref, o_ref,
              glu_buf, yb_buf, sc_buf, p_buf, s2_buf, s4_buf, s8_buf, *, tt):
    x = x_ref[0, r0:r0 + sub, :]
    z = _dot(_rms(x, g_ref[...]).astype(BF16), win_ref[...])
    zcol = lambda i: z[:, i * D_GROUP:(i + 1) * D_GROUP]
    group_norm = lambda g, y: _rms(y, mog_ref[:, g * D_GROUP:(g + 1) * D_GROUP]).astype(BF16)
    yn = [None] * N_GROUPS

    vn = _layer_norm(zcol(1), lng_ref[...], lnb_ref[...])
    head = lax.broadcasted_iota(jnp.int32, (CHUNK, D_GROUP), 1) // GMLP_HEAD_DIM
    gates = []
    for c in range(sub // CHUNK):
        vc = vn[c * CHUNK:(c + 1) * CHUNK]
        stack = jnp.concatenate([jnp.where(head == hh, vc, 0.0) for hh in range(GMLP_HEADS)], axis=0)
        gates.append(_dot(wcat_ref[...], stack.astype(BF16)) + bst_ref[...])
    yn[0] = group_norm(0, zcol(0) * jnp.concatenate(gates, axis=0))

    first = GLU_HALO - (CONF_WIDTH - 1)
    span = SUBLANES * CONV_STRIDE
    glu = zcol(2) * _sigmoid(zcol(3))
    for s in range(LANE_SLABS):
        glu_buf[s][GLU_HALO + r0:GLU_HALO + r0 + sub, :] = glu[:, s * LANES:(s + 1) * LANES]
    for c0 in range(r0, r0 + sub, CONV_ROWS):
        starts = [c0 + (i // CONV_STRIDE) * span + i % CONV_STRIDE for i in range(CONV_ROWS // SUBLANES)]
        accs = [[None] * len(starts) for _ in range(LANE_SLABS)]
        for k in range(CONF_WIDTH):
            for s in range(LANE_SLABS):
                wk = jnp.broadcast_to(cw_ref[k:k + 1, s * LANES:(s + 1) * LANES], (SUBLANES, LANES))
                for i, t0 in enumerate(starts):
                    term = wk * glu_buf[s][pl.ds(first + t0 + k, SUBLANES, stride=CONV_STRIDE), :]
                    accs[s][i] = term if k == 0 else accs[s][i] + term
        conv = jnp.concatenate([jnp.concatenate(a, axis=0) for a in accs], axis=1)
        ln = _layer_norm(conv + cb_ref[...], clg_ref[...], clb_ref[...])
        yb = ln * _sigmoid(ln)
        for s in range(LANE_SLABS):
            for i, t0 in enumerate(starts):
                yb_buf[s][pl.ds(t0, SUBLANES, stride=CONV_STRIDE), :] = (
                    yb[i * SUBLANES:(i + 1) * SUBLANES, s * LANES:(s + 1) * LANES])
    yn[1] = group_norm(1, jnp.concatenate([yb_buf[s][r0:r0 + sub, :] for s in range(LANE_SLABS)], axis=1))

    sxc = zcol(5) * zcol(6)
    lo = SC_HALO + r0
    sc_buf[lo:lo + sub, :] = sxc
    conv_c = (sw_ref[0:1, :] * sc_buf[lo - 2:lo - 2 + sub, :]
              + sw_ref[1:2, :] * sc_buf[lo - 1:lo - 1 + sub, :]
              + sw_ref[2:3, :] * sxc)
    yn[2] = group_norm(2, zcol(4) * conv_c)

    pool_x = zcol(7)
    lo = POOL_HALO + r0
    end = lo + sub
    p_buf[lo:end, :] = pool_x
    lo2, lo4, lo8 = (8, 16, 24) if r0 == 0 else (lo, lo, lo)
    s2_buf[lo2:end, :] = p_buf[lo2:end, :] + p_buf[lo2 - 1:end - 1, :]
    s4_buf[lo4:end, :] = s2_buf[lo4:end, :] + s2_buf[lo4 - 2:end - 2, :]
    s8_buf[lo8:end, :] = s4_buf[lo8:end, :] + s4_buf[lo8 - 4:end - 4, :]
    s16 = s8_buf[lo:end, :] + s8_buf[lo - 8:end - 8, :]
    grp = _pool_window_lanes((sub, D_GROUP))[0]
    ssum = _select_by_group(grp, [s2_buf[lo:end, :], s4_buf[lo:end, :], s8_buf[lo:end, :], s16])
    head_rows = POOL_BUF + 1 if r0 == 0 else 0
    inv_win = _select_by_group(_pool_window_lanes((sub - head_rows, D_GROUP))[0], [1.0 / w for w in POOL_WINDOWS])
    mean = ssum[head_rows:] * inv_win
    if head_rows:
        pos = t * tt + lax.broadcasted_iota(jnp.int32, (head_rows, D_GROUP), 0)
        cnt = jnp.minimum(_pool_window_lanes((head_rows, D_GROUP))[1], (pos + 1).astype(F32))
        mean = jnp.concatenate([ssum[:head_rows] / cnt, mean], axis=0)
    pooled = mean - pool_x
    yn[3] = group_norm(3, _dot(pooled.astype(BF16), pw_ref[...]) * ps_ref[...])

    o_ref[0, r0:r0 + sub, :] = x + _dot(jnp.concatenate(yn, axis=1), wout_ref[...])


def _mixer_prompt(x, lw, *, tt=1024, sub=1024, cast=()):
    bsz, seq, _ = x.shape
    n_t = seq // tt
    row = lambda a: a.reshape(1, -1)
    small = [row(lw['norm_mix']), lw['w_in'], row(lw['gmlp_ln_g']), row(lw['gmlp_ln_b']), lw['gmlp_wcat'],
             lw['gmlp_bias_tile'], lw['conf_dw'], row(lw['conf_dw_b']), row(lw['conf_ln_g']), row(lw['conf_ln_b']),
             lw['sc_dw'], lw['pool_w_bd'], row(lw['pool_scale']), row(lw['mix_out_g']), lw['w_out']]
    assert 1 + len(small) == MIXER_INPUTS
    jobs = [_cast_job(w, layer, bsz * n_t, lambda b, t: b * n_t + t) for w, layer in cast]
    state_spec = lambda rows: pl.BlockSpec((1, rows, D_GROUP), lambda b, t: (b, 0, 0))
    outs = pl.pallas_call(
        functools.partial(_mixer_prompt_kernel, tt=tt, n_t=n_t, sub=sub, n_cast=len(jobs)),
        grid=(bsz, n_t),
        in_specs=[pl.BlockSpec((1, tt, D_MODEL), lambda b, t: (b, t, 0))] + [_const_spec(a.shape) for a in small]
        + [j[0] for j in jobs],
        out_specs=[pl.BlockSpec((1, tt, D_MODEL), lambda b, t: (b, t, 0)),
                   state_spec(CONF_WIDTH - 1), state_spec(SC_WIDTH - 1), state_spec(POOL_BUF)] + [j[1] for j in jobs],
        out_shape=[jax.ShapeDtypeStruct(x.shape, F32),
                   jax.ShapeDtypeStruct((bsz, CONF_WIDTH - 1, D_GROUP), F32),
                   jax.ShapeDtypeStruct((bsz, SC_WIDTH - 1, D_GROUP), F32),
                   jax.ShapeDtypeStruct((bsz, POOL_BUF, D_GROUP), F32)] + [j[2] for j in jobs],
        scratch_shapes=[pltpu.VMEM((GLU_HALO + tt, LANES), F32)] * LANE_SLABS + [pltpu.VMEM((tt, LANES), F32)] * LANE_SLABS
        + [pltpu.VMEM((SC_HALO + tt, D_GROUP), F32)]
        + [pltpu.VMEM((POOL_HALO + tt, D_GROUP), F32)] * 4,
        compiler_params=pltpu.CompilerParams(dimension_semantics=("arbitrary", "arbitrary"),
                                             vmem_limit_bytes=VMEM_LIMIT),
        name="mixer_prompt",
    )(x, *small, *[w for w, _ in cast])
    return (*outs[:MIXER_OUTPUTS], list(outs[MIXER_OUTPUTS:]))


def _mixer_sample_kernel(x_ref, stg_ref, sts_ref, stp_ref, g_ref, win_ref, lng_ref, lnb_ref, w00_ref, b0_ref,
                         cw_ref, cb_ref, clg_ref, clb_ref, sw_ref, pw_ref, ps_ref, mog_ref, wout_ref,
                         o_ref, ovn_ref, oglu_ref, osh_ref, opool_ref):
    x = x_ref[...]
    h = _rms(x, g_ref[...]).astype(BF16)
    z = _dot(h, win_ref[...])
    u, v, glu_a, glu_g, sc_b, sc_c, sc_x, pool_x = [z[:, i * D_GROUP:(i + 1) * D_GROUP] for i in range(8)]
    hist = lambda ref, k: ref[:, k, :]

    def push(new_ref, old_ref, row):
        n_hist = old_ref.shape[1]
        new_ref[:, 0:n_hist - 1, :] = old_ref[:, 1:n_hist, :]
        new_ref[:, n_hist - 1, :] = row

    vn = _layer_norm(v, lng_ref[...], lnb_ref[...])
    ovn_ref[...] = vn
    y_a = u * (w00_ref[...] * vn + b0_ref[...])

    glu = glu_a * _sigmoid(glu_g)
    n_hist = CONF_WIDTH - 1
    acc = cw_ref[n_hist:n_hist + 1, :] * glu
    for k in range(n_hist):
        acc = acc + cw_ref[k:k + 1, :] * hist(stg_ref, k)
    ln = _layer_norm(acc + cb_ref[...], clg_ref[...], clb_ref[...])
    y_b = ln * _sigmoid(ln)
    push(oglu_ref, stg_ref, glu)

    sxc = sc_c * sc_x
    y_c = sc_b * (sw_ref[0:1, :] * hist(sts_ref, 0) + sw_ref[1:2, :] * hist(sts_ref, 1) + sw_ref[2:3, :] * sxc)
    push(osh_ref, sts_ref, sxc)

    run = pool_x
    sums = []
    back = 0
    for w in POOL_WINDOWS:
        while back < w - 1:
            run = run + hist(stp_ref, POOL_BUF - 1 - back)
            back += 1
        sums.append(run)
    grp, win = _pool_window_lanes(pool_x.shape)
    cnt = jnp.minimum(win, float(PAST_LEN + 1))
    pooled = _select_by_group(grp, sums) / cnt - pool_x
    y_d = _dot(pooled.astype(BF16), pw_ref[...]) * ps_ref[...]
    push(opool_ref, stp_ref, pool_x)

    o_ref[...] = _mix_out(x, [y_a, y_b, y_c, y_d], mog_ref, wout_ref)


def _mixer_sample(xs, st_glu, st_sh, st_pool, layer, lw):
    n = xs.shape[0]
    row = lambda a: a.reshape(1, -1)
    states = [st_glu, st_sh, st_pool]
    consts = [row(lw['norm_mix']), lw['w_in'], row(lw['gmlp_ln_g']), row(lw['gmlp_ln_b']), row(lw['gmlp_w00']),
              row(lw['gmlp_b0']), lw['conf_dw'], row(lw['conf_dw_b']), row(lw['conf_ln_g']), row(lw['conf_ln_b']),
              lw['sc_dw'], lw['pool_w_bd'], row(lw['pool_scale']), row(lw['mix_out_g']), lw['w_out']]
    state_spec = lambda s: pl.BlockSpec((None,) + s.shape[1:], lambda i: (layer, 0, 0, 0), pipeline_mode=pl.Buffered(1))
    out_shapes = [(n, D_MODEL), (n, D_GROUP)] + [s.shape[1:] for s in states]
    return pl.pallas_call(
        _mixer_sample_kernel,
        grid=(1,),
        in_specs=[_const_spec(xs.shape)] + [state_spec(s) for s in states] + [_const_spec(a.shape) for a in consts],
        out_specs=[pl.BlockSpec(s, lambda i, nd=len(s): (0,) * nd) for s in out_shapes],
        out_shape=[jax.ShapeDtypeStruct(s, F32) for s in out_shapes],
        compiler_params=pltpu.CompilerParams(dimension_semantics=("arbitrary",), vmem_limit_bytes=VMEM_LIMIT),
        name="mixer_sample",
    )(xs, *states, *consts)


def _mem_kv_kernel(m_ref, g_ref, wk_ref, wv_ref, wq_ref, wo_ref, ok_ref, ov_ref, oqk_ref, ovo_ref,
                   wk_bf16, wv_bf16, wq_bf16, wo_bf16):
    @pl.when(pl.program_id(1) == 0)
    def _():
        for src, dst in ((wk_ref, wk_bf16), (wv_ref, wv_bf16), (wq_ref, wq_bf16), (wo_ref, wo_bf16)):
            dst[...] = src[...].astype(BF16)

    m = _rms(m_ref[0], g_ref[...]).astype(BF16)
    k = _dot(m, wk_bf16[...])
    v = _dot(m, wv_bf16[...])
    contract_last = (((1,), (1,)), ((), ()))
    for h in range(XATTN_HEADS):
        lo, hi = h * XATTN_HEAD_DIM, (h + 1) * XATTN_HEAD_DIM
        ok_ref[0, :, h, :] = k[:, lo:hi]
        ov_ref[0, :, h, :] = v[:, lo:hi]
        qk = lax.dot_general(wq_bf16[:, lo:hi], k[:, lo:hi].astype(BF16), contract_last,
                             preferred_element_type=F32)
        oqk_ref[0, :, h * MEM_LEN:(h + 1) * MEM_LEN] = (qk * (XATTN_HEAD_DIM ** -0.5)).astype(BF16)
        ovo_ref[0, h * MEM_LEN:(h + 1) * MEM_LEN, :] = _dot(v[:, lo:hi].astype(BF16), wo_bf16[lo:hi, :]).astype(BF16)


def _mem_kv(mem, norm_mem, w_xk, w_xv, w_xq, w_xo):
    depth = w_xk.shape[0]
    bsz = mem.shape[0]
    g = norm_mem.reshape(depth, 1, D_MODEL)
    per_layer = lambda *shape: pl.BlockSpec((None,) + shape, lambda l, b: (l,) + (0,) * len(shape),
                                            pipeline_mode=pl.Buffered(1))
    out_blk = lambda *shape: pl.BlockSpec((None, 1) + shape, lambda l, b: (l, b) + (0,) * len(shape))
    weights = [w_xk, w_xv, w_xq, w_xo]
    return pl.pallas_call(
        _mem_kv_kernel,
        grid=(depth, bsz),
        in_specs=[pl.BlockSpec((1, MEM_LEN, D_MODEL), lambda l, b: (b, 0, 0)), per_layer(1, D_MODEL)]
        + [per_layer(D_MODEL, D_MODEL)] * len(weights),
        out_specs=[out_blk(MEM_LEN, XATTN_HEADS, XATTN_HEAD_DIM), out_blk(MEM_LEN, XATTN_HEADS, XATTN_HEAD_DIM),
                   out_blk(D_MODEL, XATTN_HEADS * MEM_LEN), out_blk(XATTN_HEADS * MEM_LEN, D_MODEL)],
        out_shape=[jax.ShapeDtypeStruct((depth, bsz, MEM_LEN, XATTN_HEADS, XATTN_HEAD_DIM), F32)] * 2
        + [jax.ShapeDtypeStruct((depth, bsz, D_MODEL, XATTN_HEADS * MEM_LEN), BF16),
           jax.ShapeDtypeStruct((depth, bsz, XATTN_HEADS * MEM_LEN, D_MODEL), BF16)],
        scratch_shapes=[pltpu.VMEM((D_MODEL, D_MODEL), BF16)] * len(weights),
        compiler_params=pltpu.CompilerParams(dimension_semantics=("arbitrary", "arbitrary"),
                                             vmem_limit_bytes=VMEM_LIMIT),
        name="mem_kv",
    )(mem, g, *weights)


def _xattn_prompt_kernel(x_ref, g_ref, qk_ref, vo_ref, o_ref):
    x = x_ref[0]
    h = _rms(x, g_ref[...]).astype(BF16)
    s = _dot(h, qk_ref[0])
    probs = []
    for hh in range(XATTN_HEADS):
        sh = s[:, hh * MEM_LEN:(hh + 1) * MEM_LEN]
        e = jnp.exp(sh - jnp.max(sh, axis=-1, keepdims=True))
        probs.append((e * (1.0 / jnp.sum(e, axis=-1, keepdims=True))).astype(BF16))
    o_ref[0] = x + _dot(jnp.concatenate(probs, axis=1), vo_ref[0])


def _xattn_prompt(x, qk, vo, layer, lw, *, tt=1024):
    bsz, seq, _ = x.shape
    g = lw['norm_xattn'].reshape(1, -1)
    folded = lambda a: pl.BlockSpec((None, 1) + a.shape[2:], lambda b, t: (layer, b, 0, 0))
    return pl.pallas_call(
        _xattn_prompt_kernel,
        grid=(bsz, seq // tt),
        in_specs=[pl.BlockSpec((1, tt, D_MODEL), lambda b, t: (b, t, 0)), _const_spec(g.shape),
                  folded(qk), folded(vo)],
        out_specs=pl.BlockSpec((1, tt, D_MODEL), lambda b, t: (b, t, 0)),
        out_shape=jax.ShapeDtypeStruct(x.shape, F32),
        compiler_params=pltpu.CompilerParams(dimension_semantics=("arbitrary", "arbitrary"),
                                             vmem_limit_bytes=VMEM_LIMIT),
        name="xattn_prompt",
    )(x, g, qk, vo)


def _norm_proj_kernel(x_ref, g_ref, w_ref, o_ref, *, scale):
    o_ref[...] = _dot(_rms(x_ref[...], g_ref[...]).astype(BF16), w_ref[...]) * scale


def _norm_proj(x, g, w, scale):
    g = g.reshape(1, -1)
    out_shape = (x.shape[0], w.shape[1])
    return pl.pallas_call(
        functools.partial(_norm_proj_kernel, scale=scale),
        grid=(1,),
        in_specs=[_const_spec(x.shape), _const_spec(g.shape), _const_spec(w.shape)],
        out_specs=pl.BlockSpec(out_shape, lambda i: (0, 0)),
        out_shape=jax.ShapeDtypeStruct(out_shape, F32),
        compiler_params=pltpu.CompilerParams(dimension_semantics=("arbitrary",), vmem_limit_bytes=VMEM_LIMIT),
        name="norm_proj",
    )(x, g, w)


def _proj_residual_kernel(x_ref, a_ref, w_ref, o_ref):
    o_ref[...] = x_ref[...] + _dot(a_ref[...].astype(BF16), w_ref[...])


def _proj_residual(x, a, w):
    return pl.pallas_call(
        _proj_residual_kernel,
        grid=(1,),
        in_specs=[_const_spec(x.shape), _const_spec(a.shape), _const_spec(w.shape)],
        out_specs=pl.BlockSpec(x.shape, lambda i: (0, 0)),
        out_shape=jax.ShapeDtypeStruct(x.shape, F32),
        compiler_params=pltpu.CompilerParams(dimension_semantics=("arbitrary",), vmem_limit_bytes=VMEM_LIMIT),
        name="proj_residual",
    )(x, a, w)


def _split_head_dim(a):
    lead = a.shape[:-2]
    a = a.reshape(*lead, XATTN_HEADS, 2, XATTN_HEAD_DIM // 2)
    return jnp.swapaxes(a, -3, -2).reshape(*lead, 2 * XATTN_HEADS, XATTN_HEAD_DIM // 2)


def _merge_head_dim(a):
    lead = a.shape[:-2]
    a = a.reshape(*lead, 2, XATTN_HEADS, XATTN_HEAD_DIM // 2)
    return jnp.swapaxes(a, -3, -2).reshape(*lead, XATTN_HEADS, XATTN_HEAD_DIM)


def _attend_rows(q_ref, k_ref, v_ref, o_ref, rows):
    for r in rows:
        part = jnp.sum(k_ref[r] * q_ref[r][None], axis=-1, keepdims=True)
        s = part + pltpu.roll(part, XATTN_HEADS, axis=1)
        e = jnp.exp(s - jnp.max(s, axis=0, keepdims=True))
        o_ref[r] = jnp.sum(e * v_ref[r], axis=0) * (1.0 / jnp.sum(e, axis=0))


def _ffn_kernel(x_ref, g_ref, w1_ref, w2_ref, gf_ref, o_ref, *, final_norm, between_chunks=None):
    x = x_ref[...]
    h = _rms(x, g_ref[...]).astype(BF16)
    y = x
    for i, c in enumerate(range(0, D_FF, FF_CHUNK)):
        if between_chunks is not None:
            between_chunks(i)
        a = jnp.maximum(_dot(h, w1_ref[:, c:c + FF_CHUNK]), 0.0)
        y = y + _dot((a * a).astype(BF16), w2_ref[c:c + FF_CHUNK, :])
    o_ref[...] = _rms(y, gf_ref[...]) if final_norm else y


FFN_ATTEND_INPUTS = 8
FFN_ATTEND_OUTPUTS = 2


def _ffn_attend_kernel(*refs, final_norm, rows, n_cast):
    x_ref, g_ref, w1_ref, w2_ref, gf_ref, q_ref, k_ref, v_ref = refs[:FFN_ATTEND_INPUTS]
    cast_src = refs[FFN_ATTEND_INPUTS:FFN_ATTEND_INPUTS + n_cast]
    o_ref, oa_ref = refs[FFN_ATTEND_INPUTS + n_cast:FFN_ATTEND_INPUTS + n_cast + FFN_ATTEND_OUTPUTS]
    cast_dst = refs[FFN_ATTEND_INPUTS + n_cast + FFN_ATTEND_OUTPUTS:]
    _run_cast_jobs(cast_src, cast_dst)
    n_chunks = D_FF // FF_CHUNK
    assert rows % n_chunks == 0
    per_chunk = rows // n_chunks
    attend = lambda i: _attend_rows(q_ref, k_ref, v_ref, oa_ref, range(i * per_chunk, (i + 1) * per_chunk))
    _ffn_kernel(x_ref, g_ref, w1_ref, w2_ref, gf_ref, o_ref, final_norm=final_norm, between_chunks=attend)


def _ffn(x2d, lw, norm_final, *, final_norm, tm=512, attend=None, cast=()):
    n = x2d.shape[0]
    tm = min(tm, n)
    g = lw['norm_ffn'].reshape(1, -1)
    gf = norm_final.reshape(1, -1)
    in_specs = [pl.BlockSpec((tm, D_MODEL), lambda i: (i, 0)), _const_spec(g.shape),
                _const_spec(lw['w_ff1'].shape), _const_spec(lw['w_ff2'].shape), _const_spec(gf.shape)]
    out_spec = pl.BlockSpec((tm, D_MODEL), lambda i: (i, 0))
    out_shape = jax.ShapeDtypeStruct(x2d.shape, F32)
    params = pltpu.CompilerParams(dimension_semantics=("arbitrary",), vmem_limit_bytes=VMEM_LIMIT)
    if attend is None:
        return pl.pallas_call(
            functools.partial(_ffn_kernel, final_norm=final_norm), grid=(n // tm,), in_specs=in_specs,
            out_specs=out_spec, out_shape=out_shape, compiler_params=params, name="ffn",
        )(x2d, g, lw['w_ff1'], lw['w_ff2'], gf)
    q, cache_k, cache_v, layer = attend
    rows = q.shape[0] // (n // tm)
    assert rows * (n // tm) == q.shape[0]
    q_spec = pl.BlockSpec((rows,) + q.shape[1:], lambda i: (i, 0, 0))
    kv_spec = pl.BlockSpec((None, rows) + cache_k.shape[2:], lambda i: (layer, i, 0, 0, 0))
    jobs = [_cast_job(w, wl, n // tm, lambda i: i) for w, wl in cast]
    outs = pl.pallas_call(
        functools.partial(_ffn_attend_kernel, final_norm=final_norm, rows=rows, n_cast=len(jobs)), grid=(n // tm,),
        in_specs=in_specs + [q_spec, kv_spec, kv_spec] + [j[0] for j in jobs],
        out_specs=[out_spec, q_spec] + [j[1] for j in jobs],
        out_shape=[out_shape, jax.ShapeDtypeStruct(q.shape, F32)] + [j[2] for j in jobs],
        compiler_params=params, name="ffn_attend",
    )(x2d, g, lw['w_ff1'], lw['w_ff2'], gf, q, cache_k, cache_v, *[w for w, _ in cast])
    return outs[0], outs[1], list(outs[FFN_ATTEND_OUTPUTS:])


PROJ_WEIGHTS = ('w_in', 'w_out', 'w_xq', 'w_xo')


def _layer_weights(l, p, proj_bf16):
    tril = jnp.tril(jnp.ones((CHUNK, CHUNK), dtype=bool))
    ws = jnp.where(tril[None], p['gmlp_ws'][l], 0.0)
    pool_bd = jax.scipy.linalg.block_diag(*[p['pool_w'][l, g] for g in range(len(POOL_WINDOWS))])
    lw = {k: p[k][l] for k in ('norm_mix', 'gmlp_ln_g', 'gmlp_ln_b', 'conf_dw', 'conf_dw_b', 'conf_ln_g',
                               'conf_ln_b', 'sc_dw', 'pool_scale', 'mix_out_g', 'norm_xattn', 'norm_ffn')}
    lw.update(proj_bf16)
    lw.update(
        gmlp_wcat=jnp.concatenate([ws[h] for h in range(GMLP_HEADS)], axis=1).astype(BF16),
        gmlp_bias_tile=jnp.repeat(p['gmlp_bs'][l].T, GMLP_HEAD_DIM, axis=1),
        gmlp_w00=jnp.repeat(ws[:, 0, 0], GMLP_HEAD_DIM), gmlp_b0=jnp.repeat(p['gmlp_bs'][l][:, 0], GMLP_HEAD_DIM),
        pool_w_bd=pool_bd.astype(BF16))
    return lw


def kernel(x_prompt, x_sample, mem_prompt, cache_mem_k, cache_mem_v, state_conv_glu, state_conv_short, state_pool, norm_mix, w_in, gmlp_ln_g, gmlp_ln_b, gmlp_ws, gmlp_bs, conf_dw, conf_dw_b, conf_ln_g, conf_ln_b, sc_dw, pool_w, pool_scale, mix_out_g, w_out, norm_xattn, norm_mem, w_xq, w_xk, w_xv, w_xo, norm_ffn, w_ff1, w_ff2, norm_final):
    params = dict(norm_mix=norm_mix, w_in=w_in, gmlp_ln_g=gmlp_ln_g, gmlp_ln_b=gmlp_ln_b, gmlp_ws=gmlp_ws,
                  gmlp_bs=gmlp_bs, conf_dw=conf_dw, conf_dw_b=conf_dw_b, conf_ln_g=conf_ln_g, conf_ln_b=conf_ln_b,
                  sc_dw=sc_dw, pool_w=pool_w, pool_scale=pool_scale, mix_out_g=mix_out_g, w_out=w_out,
                  norm_xattn=norm_xattn, norm_mem=norm_mem, w_xq=w_xq, w_xk=w_xk, w_xv=w_xv, w_xo=w_xo,
                  norm_ffn=norm_ffn, w_ff1=w_ff1, w_ff2=w_ff2)
    depth = w_in.shape[0]
    bsz, seq, _ = x_prompt.shape
    n_s = x_sample.shape[0]
    xp = x_prompt
    xs = x_sample.reshape(n_s, D_MODEL)
    outs = {k: [] for k in ('glu_p', 'glu_s', 'sh_p', 'sh_s', 'pl_p', 'pl_s', 'v_s')}
    mem_k, mem_v, mem_qk, mem_vo = _mem_kv(mem_prompt, norm_mem, w_xk, w_xv, w_xq, w_xo)
    cache_k_split = _split_head_dim(cache_mem_k)
    cache_v_split = _split_head_dim(cache_mem_v)
    proj_bf16 = {k: params[k][0].astype(BF16) for k in PROJ_WEIGHTS}
    for l in range(depth):
        lw = _layer_weights(l, params, proj_bf16)
        last = l == depth - 1
        xp, glu_p, sh_p, pool_p, (lw['w_ff1'], lw['w_ff2']) = _mixer_prompt(
            xp, lw, cast=[(params['w_ff1'], l), (params['w_ff2'], l)])
        xp = _xattn_prompt(xp, mem_qk, mem_vo, l, lw)
        outs['glu_p'].append(glu_p); outs['sh_p'].append(sh_p); outs['pl_p'].append(pool_p)
        xs, vn_s, glu_s, sh_s, pool_s = _mixer_sample(xs, state_conv_glu, state_conv_short, state_pool, l, lw)
        q_s = _norm_proj(xs, lw['norm_xattn'], lw['w_xq'], XATTN_HEAD_DIM ** -0.5)
        q_s = _split_head_dim(q_s.reshape(n_s, XATTN_HEADS, XATTN_HEAD_DIM))
        xp, o_s, nxt = _ffn(xp.reshape(bsz * seq, D_MODEL), lw, norm_final, final_norm=last,
                            attend=(q_s, cache_k_split, cache_v_split, l),
                            cast=[] if last else [(params[k], l + 1) for k in PROJ_WEIGHTS])
        proj_bf16 = dict(zip(PROJ_WEIGHTS, nxt))
        xp = xp.reshape(bsz, seq, D_MODEL)
        xs = _proj_residual(xs, _merge_head_dim(o_s).reshape(n_s, D_MODEL), lw['w_xo'])
        xs = _ffn(xs, lw, norm_final, final_norm=last)
        outs['glu_s'].append(glu_s); outs['sh_s'].append(sh_s); outs['pl_s'].append(pool_s)
        outs['v_s'].append(vn_s.reshape(n_s, 1, D_GROUP))
    st = lambda k: jnp.stack(outs[k], axis=0)
    return (xp, xs.reshape(n_s, 1, D_MODEL), mem_k, mem_v, st('glu_p'), st('glu_s'), st('sh_p'), st('sh_s'),
            st('pl_p'), st('pl_s'), st('v_s'))
```

```python
import functools

import jax
import jax.numpy as jnp
from jax import lax
from jax.experimental import pallas as pl
from jax.experimental.pallas import tpu as pltpu

F32 = jnp.float32
BF16 = jnp.bfloat16

D_MODEL = 1024
D_GROUP = 256
N_GROUPS = 4
D_IN_PROJ = 8 * D_GROUP
GMLP_HEADS = 4
GMLP_HEAD_DIM = D_GROUP // GMLP_HEADS
CHUNK = 128
CONF_WIDTH = 31
SC_WIDTH = 3
POOL_WINDOWS = (2, 4, 8, 16)
POOL_GROUP_DIM = D_GROUP // len(POOL_WINDOWS)
POOL_BUF = max(POOL_WINDOWS) - 1
MEM_LEN = 256
XATTN_HEADS = 4
XATTN_HEAD_DIM = D_MODEL // XATTN_HEADS
D_FF = 4 * D_MODEL
PAST_LEN = 16384
EPS = 1e-6

GLU_HALO = 32
SC_HALO = 8
POOL_HALO = 32
LANES = 128
SUBLANES = 8
BF16_SUBLANES = 16
LANE_SLABS = D_GROUP // LANES
CONV_STRIDE = 4
CONV_ROWS = 64
FF_CHUNK = 4096
VMEM_LIMIT = 56 * 1024 * 1024


def _rms(x, g):
    return x * lax.rsqrt(jnp.mean(x * x, axis=-1, keepdims=True) + EPS) * g


def _layer_norm(x, g, b):
    xc = x - jnp.mean(x, axis=-1, keepdims=True)
    return xc * lax.rsqrt(jnp.mean(xc * xc, axis=-1, keepdims=True) + EPS) * g + b


def _sigmoid(x):
    return 0.5 * jnp.tanh(0.5 * x) + 0.5


def _dot(a, b):
    return jnp.dot(a, b, preferred_element_type=F32)


def _const_spec(shape):
    zeros = (0,) * len(shape)
    return pl.BlockSpec(shape, lambda *_: zeros, pipeline_mode=pl.Buffered(1))


def _cast_job(stacked, layer, steps, step_of):
    _, r, c = stacked.shape
    blk = r // steps
    assert blk * steps == r and blk % BF16_SUBLANES == 0
    return (pl.BlockSpec((None, blk, c), lambda *ids: (layer, step_of(*ids), 0)),
            pl.BlockSpec((blk, c), lambda *ids: (step_of(*ids), 0)),
            jax.ShapeDtypeStruct((r, c), BF16))


def _run_cast_jobs(src_refs, dst_refs):
    for src, dst in zip(src_refs, dst_refs):
        dst[...] = src[...].astype(BF16)


def _pool_window_lanes(shape):
    grp = lax.broadcasted_iota(jnp.int32, shape, len(shape) - 1) // POOL_GROUP_DIM
    win = jnp.full(shape, float(POOL_WINDOWS[-1]), F32)
    for g in range(len(POOL_WINDOWS) - 2, -1, -1):
        win = jnp.where(grp == g, float(POOL_WINDOWS[g]), win)
    return grp, win


def _select_by_group(grp, vals):
    out = vals[-1]
    for g in range(len(vals) - 2, -1, -1):
        out = jnp.where(grp == g, vals[g], out)
    return out


def _mix_out(x, y_groups, mog_ref, wout_ref):
    yn = [_rms(y, mog_ref[:, g * D_GROUP:(g + 1) * D_GROUP]).astype(BF16) for g, y in enumerate(y_groups)]
    return x + _dot(jnp.concatenate(yn, axis=1), wout_ref[...])


MIXER_INPUTS = 16
MIXER_OUTPUTS = 4


def _mixer_prompt_kernel(*refs, tt, n_t, sub, n_cast):
    (x_ref, g_ref, win_ref, lng_ref, lnb_ref, wcat_ref, bst_ref, cw_ref, cb_ref, clg_ref,
     clb_ref, sw_ref, pw_ref, ps_ref, mog_ref, wout_ref) = refs[:MIXER_INPUTS]
    cast_src = refs[MIXER_INPUTS:MIXER_INPUTS + n_cast]
    outs = refs[MIXER_INPUTS + n_cast:]
    o_ref, oglu_ref, osh_ref, opool_ref = outs[:MIXER_OUTPUTS]
    cast_dst = outs[MIXER_OUTPUTS:MIXER_OUTPUTS + n_cast]
    glu_buf0, glu_buf1, yb_buf0, yb_buf1, sc_buf, p_buf, s2_buf, s4_buf, s8_buf = outs[MIXER_OUTPUTS + n_cast:]
    _run_cast_jobs(cast_src, cast_dst)
    t = pl.program_id(1)
    glu_buf = (glu_buf0, glu_buf1)
    yb_buf = (yb_buf0, yb_buf1)

    @pl.when(t == 0)
    def _():
        for s in range(LANE_SLABS):
            glu_buf[s][0:GLU_HALO, :] = jnp.zeros((GLU_HALO, LANES), F32)
        sc_buf[0:SC_HALO, :] = jnp.zeros((SC_HALO, D_GROUP), F32)
        p_buf[0:POOL_HALO, :] = jnp.zeros((POOL_HALO, D_GROUP), F32)

    for r0 in range(0, tt, sub):
        _mix_rows(t, r0, sub, x_ref, g_ref, win_ref, lng_ref, lnb_ref, wcat_ref, bst_ref, cw_ref, cb_ref, clg_ref,
                  clb_ref, sw_ref, pw_ref, ps_ref, mog_ref, wout_ref, o_ref,
                  glu_buf, yb_buf, sc_buf, p_buf, s2_buf, s4_buf, s8_buf, tt=tt)

    @pl.when(t == n_t - 1)
    def _():
        oglu_ref[0] = jnp.concatenate(
            [glu_buf[s][GLU_HALO + tt - (CONF_WIDTH - 1):GLU_HALO + tt, :] for s in range(LANE_SLABS)], axis=1)
        osh_ref[0] = sc_buf[SC_HALO + tt - (SC_WIDTH - 1):SC_HALO + tt, :]
        opool_ref[0] = p_buf[POOL_HALO + tt - POOL_BUF:POOL_HALO + tt, :]

    for s in range(LANE_SLABS):
        glu_buf[s][0:GLU_HALO, :] = glu_buf[s][tt:tt + GLU_HALO, :]
    sc_buf[0:SC_HALO, :] = sc_buf[tt:tt + SC_HALO, :]
    p_buf[0:POOL_HALO, :] = p_buf[tt:tt + POOL_HALO, :]


def _mix_rows(t, r0, sub, x_ref, g_ref, win_ref, lng_ref, lnb_ref, wcat_ref, bst_ref, cw_ref, cb_ref, clg_ref,
              clb_ref, sw_ref, pw_ref, ps_ref, mog_ref, wout_ref, o_ref,
              glu_buf, yb_buf, sc_buf, p_buf, s2_buf, s4_buf, s8_buf, *, tt):
    x = x_ref[0, r0:r0 + sub, :]
    z = _dot(_rms(x, g_ref[...]).astype(BF16), win_ref[...])
    zcol = lambda i: z[:, i * D_GROUP:(i + 1) * D_GROUP]
    group_norm = lambda g, y: _rms(y, mog_ref[:, g * D_GROUP:(g + 1) * D_GROUP]).astype(BF16)
    yn = [None] * N_GROUPS

    vn = _layer_norm(zcol(1), lng_ref[...], lnb_ref[...])
    head = lax.broadcasted_iota(jnp.int32, (CHUNK, D_GROUP), 1) // GMLP_HEAD_DIM
    gates = []
    for c in range(sub // CHUNK):
        vc = vn[c * CHUNK:(c + 1) * CHUNK]
        stack = jnp.concatenate([jnp.where(head == hh, vc, 0.0) for hh in range(GMLP_HEADS)], axis=0)
        gates.append(_dot(wcat_ref[...], stack.astype(BF16)) + bst_ref[...])
    yn[0] = group_norm(0, zcol(0) * jnp.concatenate(gates, axis=0))

    first = GLU_HALO - (CONF_WIDTH - 1)
    span = SUBLANES * CONV_STRIDE
    glu = zcol(2) * _sigmoid(zcol(3))
    for s in range(LANE_SLABS):
        glu_buf[s][GLU_HALO + r0:GLU_HALO + r0 + sub, :] = glu[:, s * LANES:(s + 1) * LANES]
    for c0 in range(r0, r0 + sub, CONV_ROWS):
        starts = [c0 + (i // CONV_STRIDE) * span + i % CONV_STRIDE for i in range(CONV_ROWS // SUBLANES)]
        accs = [[None] * len(starts) for _ in range(LANE_SLABS)]
        for k in range(CONF_WIDTH):
            for s in range(LANE_SLABS):
                wk = jnp.broadcast_to(cw_ref[k:k + 1, s * LANES:(s + 1) * LANES], (SUBLANES, LANES))
                for i, t0 in enumerate(starts):
                    term = wk * glu_buf[s][pl.ds(first + t0 + k, SUBLANES, stride=CONV_STRIDE), :]
                    accs[s][i] = term if k == 0 else accs[s][i] + term
        conv = jnp.concatenate([jnp.concatenate(a, axis=0) for a in accs], axis=1)
        ln = _layer_norm(conv + cb_ref[...], clg_ref[...], clb_ref[...])
        yb = ln * _sigmoid(ln)
        for s in range(LANE_SLABS):
            for i, t0 in enumerate(starts):
                yb_buf[s][pl.ds(t0, SUBLANES, stride=CONV_STRIDE), :] = (
                    yb[i * SUBLANES:(i + 1) * SUBLANES, s * LANES:(s + 1) * LANES])
    yn[1] = group_norm(1, jnp.concatenate([yb_buf[s][r0:r0 + sub, :] for s in range(LANE_SLABS)], axis=1))

    sxc = zcol(5) * zcol(6)
    lo = SC_HALO + r0
    sc_buf[lo:lo + sub, :] = sxc
    conv_c = (sw_ref[0:1, :] * sc_buf[lo - 2:lo - 2 + sub, :]
              + sw_ref[1:2, :] * sc_buf[lo - 1:lo - 1 + sub, :]
              + sw_ref[2:3, :] * sxc)
    yn[2] = group_norm(2, zcol(4) * conv_c)

    pool_x = zcol(7)
    lo = POOL_HALO + r0
    end = lo + sub
    p_buf[lo:end, :] = pool_x
    lo2, lo4, lo8 = (8, 16, 24) if r0 == 0 else (lo, lo, lo)
    s2_buf[lo2:end, :] = p_buf[lo2:end, :] + p_buf[lo2 - 1:end - 1, :]
    s4_buf[lo4:end, :] = s2_buf[lo4:end, :] + s2_buf[lo4 - 2:end - 2, :]
    s8_buf[lo8:end, :] = s4_buf[lo8:end, :] + s4_buf[lo8 - 4:end - 4, :]
    s16 = s8_buf[lo:end, :] + s8_buf[lo - 8:end - 8, :]
    grp = _pool_window_lanes((sub, D_GROUP))[0]
    ssum = _select_by_group(grp, [s2_buf[lo:end, :], s4_buf[lo:end, :], s8_buf[lo:end, :], s16])
    head_rows = POOL_BUF + 1 if r0 == 0 else 0
    inv_win = _select_by_group(_pool_window_lanes((sub - head_rows, D_GROUP))[0], [1.0 / w for w in POOL_WINDOWS])
    mean = ssum[head_rows:] * inv_win
    if head_rows:
        pos = t * tt + lax.broadcasted_iota(jnp.int32, (head_rows, D_GROUP), 0)
        cnt = jnp.minimum(_pool_window_lanes((head_rows, D_GROUP))[1], (pos + 1).astype(F32))
        mean = jnp.concatenate([ssum[:head_rows] / cnt, mean], axis=0)
    pooled = mean - pool_x
    yn[3] = group_norm(3, _dot(pooled.astype(BF16), pw_ref[...]) * ps_ref[...])

    o_ref[0, r0:r0 + sub, :] = x + _dot(jnp.concatenate(yn, axis=1), wout_ref[...])


def _mixer_prompt(x, lw, *, tt=1024, sub=1024, cast=()):
    bsz, seq, _ = x.shape
    n_t = seq // tt
    row = lambda a: a.reshape(1, -1)
    small = [row(lw['norm_mix']), lw['w_in'], row(lw['gmlp_ln_g']), row(lw['gmlp_ln_b']), lw['gmlp_wcat'],
             lw['gmlp_bias_tile'], lw['conf_dw'], row(lw['conf_dw_b']), row(lw['conf_ln_g']), row(lw['conf_ln_b']),
             lw['sc_dw'], lw['pool_w_bd'], row(lw['pool_scale']), row(lw['mix_out_g']), lw['w_out']]
    assert 1 + len(small) == MIXER_INPUTS
    jobs = [_cast_job(w, layer, bsz * n_t, lambda b, t: b * n_t + t) for w, layer in cast]
    state_spec = lambda rows: pl.BlockSpec((1, rows, D_GROUP), lambda b, t: (b, 0, 0))
    outs = pl.pallas_call(
        functools.partial(_mixer_prompt_kernel, tt=tt, n_t=n_t, sub=sub, n_cast=len(jobs)),
        grid=(bsz, n_t),
        in_specs=[pl.BlockSpec((1, tt, D_MODEL), lambda b, t: (b, t, 0))] + [_const_spec(a.shape) for a in small]
        + [j[0] for j in jobs],
        out_specs=[pl.BlockSpec((1, tt, D_MODEL), lambda b, t: (b, t, 0)),
                   state_spec(CONF_WIDTH - 1), state_spec(SC_WIDTH - 1), state_spec(POOL_BUF)] + [j[1] for j in jobs],
        out_shape=[jax.ShapeDtypeStruct(x.shape, F32),
                   jax.ShapeDtypeStruct((bsz, CONF_WIDTH - 1, D_GROUP), F32),
                   jax.ShapeDtypeStruct((bsz, SC_WIDTH - 1, D_GROUP), F32),
                   jax.ShapeDtypeStruct((bsz, POOL_BUF, D_GROUP), F32)] + [j[2] for j in jobs],
        scratch_shapes=[pltpu.VMEM((GLU_HALO + tt, LANES), F32)] * LANE_SLABS + [pltpu.VMEM((tt, LANES), F32)] * LANE_SLABS
        + [pltpu.VMEM((SC_HALO + tt, D_GROUP), F32)]
        + [pltpu.VMEM((POOL_HALO + tt, D_GROUP), F32)] * 4,
        compiler_params=pltpu.CompilerParams(dimension_semantics=("arbitrary", "arbitrary"),
                                             vmem_limit_bytes=VMEM_LIMIT),
        name="mixer_prompt",
    )(x, *small, *[w for w, _ in cast])
    return (*outs[:MIXER_OUTPUTS], list(outs[MIXER_OUTPUTS:]))


def _mixer_sample_kernel(x_ref, stg_ref, sts_ref, stp_ref, g_ref, win_ref, lng_ref, lnb_ref, w00_ref, b0_ref,
                         cw_ref, cb_ref, clg_ref, clb_ref, sw_ref, pw_ref, ps_ref, mog_ref, wout_ref, gx_ref, wq_ref,
                         o_ref, oq_ref, ovn_ref, oglu_ref, osh_ref, opool_ref):
    x = x_ref[...]
    h = _rms(x, g_ref[...]).astype(BF16)
    z = _dot(h, win_ref[...])
    u, v, glu_a, glu_g, sc_b, sc_c, sc_x, pool_x = [z[:, i * D_GROUP:(i + 1) * D_GROUP] for i in range(8)]
    hist = lambda ref, k: ref[:, k, :]

    def push(new_ref, old_ref, row):
        n_hist = old_ref.shape[1]
        new_ref[:, 0:n_hist - 1, :] = old_ref[:, 1:n_hist, :]
        new_ref[:, n_hist - 1, :] = row

    vn = _layer_norm(v, lng_ref[...], lnb_ref[...])
    ovn_ref[...] = vn
    y_a = u * (w00_ref[...] * vn + b0_ref[...])

    glu = glu_a * _sigmoid(glu_g)
    n_hist = CONF_WIDTH - 1
    acc = cw_ref[n_hist:n_hist + 1, :] * glu
    for k in range(n_hist):
        acc = acc + cw_ref[k:k + 1, :] * hist(stg_ref, k)
    ln = _layer_norm(acc + cb_ref[...], clg_ref[...], clb_ref[...])
    y_b = ln * _sigmoid(ln)
    push(oglu_ref, stg_ref, glu)

    sxc = sc_c * sc_x
    y_c = sc_b * (sw_ref[0:1, :] * hist(sts_ref, 0) + sw_ref[1:2, :] * hist(sts_ref, 1) + sw_ref[2:3, :] * sxc)
    push(osh_ref, sts_ref, sxc)

    run = pool_x
    sums = []
    back = 0
    for w in POOL_WINDOWS:
        while back < w - 1:
            run = run + hist(stp_ref, POOL_BUF - 1 - back)
            back += 1
        sums.append(run)
    grp, win = _pool_window_lanes(pool_x.shape)
    cnt = jnp.minimum(win, float(PAST_LEN + 1))
    pooled = _select_by_group(grp, sums) / cnt - pool_x
    y_d = _dot(pooled.astype(BF16), pw_ref[...]) * ps_ref[...]
    push(opool_ref, stp_ref, pool_x)

    x_mixed = _mix_out(x, [y_a, y_b, y_c, y_d], mog_ref, wout_ref)
    o_ref[...] = x_mixed
    oq_ref[...] = _dot(_rms(x_mixed, gx_ref[...]).astype(BF16), wq_ref[...]) * (XATTN_HEAD_DIM ** -0.5)


def _mixer_sample(xs, st_glu, st_sh, st_pool, layer, lw):
    n = xs.shape[0]
    row = lambda a: a.reshape(1, -1)
    states = [st_glu, st_sh, st_pool]
    consts = [row(lw['norm_mix']), lw['w_in'], row(lw['gmlp_ln_g']), row(lw['gmlp_ln_b']), row(lw['gmlp_w00']),
              row(lw['gmlp_b0']), lw['conf_dw'], row(lw['conf_dw_b']), row(lw['conf_ln_g']), row(lw['conf_ln_b']),
              lw['sc_dw'], lw['pool_w_bd'], row(lw['pool_scale']), row(lw['mix_out_g']), lw['w_out'],
              row(lw['norm_xattn']), lw['w_xq']]
    state_spec = lambda s: pl.BlockSpec((None,) + s.shape[1:], lambda i: (layer, 0, 0, 0), pipeline_mode=pl.Buffered(1))
    out_shapes = [(n, D_MODEL), (n, D_MODEL), (n, D_GROUP)] + [s.shape[1:] for s in states]
    return pl.pallas_call(
        _mixer_sample_kernel,
        grid=(1,),
        in_specs=[_const_spec(xs.shape)] + [state_spec(s) for s in states] + [_const_spec(a.shape) for a in consts],
        out_specs=[pl.BlockSpec(s, lambda i, nd=len(s): (0,) * nd) for s in out_shapes],
        out_shape=[jax.ShapeDtypeStruct(s, F32) for s in out_shapes],
        compiler_params=pltpu.CompilerParams(dimension_semantics=("arbitrary",), vmem_limit_bytes=VMEM_LIMIT),
        name="mixer_sample",
    )(xs, *states, *consts)


MEM_KV_INPUTS = 6
MEM_KV_OUTPUTS = 6


def _mem_kv_kernel(*refs, n_cast):
    m_ref, g_ref, wk_ref, wv_ref, wq_ref, wo_ref = refs[:MEM_KV_INPUTS]
    cast_src = refs[MEM_KV_INPUTS:MEM_KV_INPUTS + n_cast]
    outs = refs[MEM_KV_INPUTS + n_cast:]
    ok_ref, ov_ref, oqk_ref, ovo_ref, wq_bf16, wo_bf16 = outs[:MEM_KV_OUTPUTS]
    cast_dst = outs[MEM_KV_OUTPUTS:MEM_KV_OUTPUTS + n_cast]
    wk_bf16, wv_bf16 = outs[MEM_KV_OUTPUTS + n_cast:]
    _run_cast_jobs(cast_src, cast_dst)

    @pl.when(pl.program_id(1) == 0)
    def _():
        for src, dst in ((wk_ref, wk_bf16), (wv_ref, wv_bf16), (wq_ref, wq_bf16), (wo_ref, wo_bf16)):
            dst[...] = src[...].astype(BF16)

    m = _rms(m_ref[0], g_ref[...]).astype(BF16)
    k = _dot(m, wk_bf16[...])
    v = _dot(m, wv_bf16[...])
    contract_last = (((1,), (1,)), ((), ()))
    for h in range(XATTN_HEADS):
        lo, hi = h * XATTN_HEAD_DIM, (h + 1) * XATTN_HEAD_DIM
        ok_ref[0, :, h, :] = k[:, lo:hi]
        ov_ref[0, :, h, :] = v[:, lo:hi]
        qk = lax.dot_general(wq_bf16[:, lo:hi], k[:, lo:hi].astype(BF16), contract_last,
                             preferred_element_type=F32)
        oqk_ref[0, :, h * MEM_LEN:(h + 1) * MEM_LEN] = (qk * (XATTN_HEAD_DIM ** -0.5)).astype(BF16)
        ovo_ref[0, h * MEM_LEN:(h + 1) * MEM_LEN, :] = _dot(v[:, lo:hi].astype(BF16), wo_bf16[lo:hi, :]).astype(BF16)


def _mem_kv(mem, norm_mem, w_xk, w_xv, w_xq, w_xo, *, cast=()):
    depth = w_xk.shape[0]
    bsz = mem.shape[0]
    g = norm_mem.reshape(depth, 1, D_MODEL)
    per_layer = lambda *shape, **kw: pl.BlockSpec((None,) + shape, lambda l, b: (l,) + (0,) * len(shape), **kw)
    out_blk = lambda *shape: pl.BlockSpec((None, 1) + shape, lambda l, b: (l, b) + (0,) * len(shape))
    weights = [w_xk, w_xv, w_xq, w_xo]
    jobs = [_cast_job(w, layer, depth * bsz, lambda l, b: l * bsz + b) for w, layer in cast]
    outs = pl.pallas_call(
        functools.partial(_mem_kv_kernel, n_cast=len(jobs)),
        grid=(depth, bsz),
        in_specs=[pl.BlockSpec((1, MEM_LEN, D_MODEL), lambda l, b: (b, 0, 0)), per_layer(1, D_MODEL)]
        + [per_layer(D_MODEL, D_MODEL, pipeline_mode=pl.Buffered(1))] * len(weights) + [j[0] for j in jobs],
        out_specs=[out_blk(MEM_LEN, XATTN_HEADS, XATTN_HEAD_DIM), out_blk(MEM_LEN, XATTN_HEADS, XATTN_HEAD_DIM),
                   out_blk(D_MODEL, XATTN_HEADS * MEM_LEN), out_blk(XATTN_HEADS * MEM_LEN, D_MODEL),
                   per_layer(D_MODEL, D_MODEL), per_layer(D_MODEL, D_MODEL)] + [j[1] for j in jobs],
        out_shape=[jax.ShapeDtypeStruct((depth, bsz, MEM_LEN, XATTN_HEADS, XATTN_HEAD_DIM), F32)] * 2
        + [jax.ShapeDtypeStruct((depth, bsz, D_MODEL, XATTN_HEADS * MEM_LEN), BF16),
           jax.ShapeDtypeStruct((depth, bsz, XATTN_HEADS * MEM_LEN, D_MODEL), BF16)]
        + [jax.ShapeDtypeStruct((depth, D_MODEL, D_MODEL), BF16)] * 2 + [j[2] for j in jobs],
        scratch_shapes=[pltpu.VMEM((D_MODEL, D_MODEL), BF16)] * 2,
        compiler_params=pltpu.CompilerParams(dimension_semantics=("arbitrary", "arbitrary"),
                                             vmem_limit_bytes=VMEM_LIMIT),
        name="mem_kv",
    )(mem, g, *weights, *[w for w, _ in cast])
    return (*outs[:MEM_KV_OUTPUTS], list(outs[MEM_KV_OUTPUTS:]))


def _xattn_prompt_kernel(x_ref, g_ref, qk_ref, vo_ref, o_ref):
    x = x_ref[0]
    h = _rms(x, g_ref[...]).astype(BF16)
    s = _dot(h, qk_ref[0])
    probs = []
    for hh in range(XATTN_HEADS):
        sh = s[:, hh * MEM_LEN:(hh + 1) * MEM_LEN]
        e = jnp.exp(sh - jnp.max(sh, axis=-1, keepdims=True))
        probs.append((e * (1.0 / jnp.sum(e, axis=-1, keepdims=True))).astype(BF16))
    o_ref[0] = x + _dot(jnp.concatenate(probs, axis=1), vo_ref[0])


def _xattn_prompt(x, qk, vo, layer, lw, *, tt=1024):
    bsz, seq, _ = x.shape
    g = lw['norm_xattn'].reshape(1, -1)
    folded = lambda a: pl.BlockSpec((None, 1) + a.shape[2:], lambda b, t: (layer, b, 0, 0))
    return pl.pallas_call(
        _xattn_prompt_kernel,
        grid=(bsz, seq // tt),
        in_specs=[pl.BlockSpec((1, tt, D_MODEL), lambda b, t: (b, t, 0)), _const_spec(g.shape),
                  folded(qk), folded(vo)],
        out_specs=pl.BlockSpec((1, tt, D_MODEL), lambda b, t: (b, t, 0)),
        out_shape=jax.ShapeDtypeStruct(x.shape, F32),
        compiler_params=pltpu.CompilerParams(dimension_semantics=("arbitrary", "arbitrary"),
                                             vmem_limit_bytes=VMEM_LIMIT),
        name="xattn_prompt",
    )(x, g, qk, vo)


def _split_head_dim(a):
    lead = a.shape[:-2]
    a = a.reshape(*lead, XATTN_HEADS, 2, XATTN_HEAD_DIM // 2)
    return jnp.swapaxes(a, -3, -2).reshape(*lead, 2 * XATTN_HEADS, XATTN_HEAD_DIM // 2)


def _merge_head_dim(a):
    lead = a.shape[:-2]
    a = a.reshape(*lead, 2, XATTN_HEADS, XATTN_HEAD_DIM // 2)
    return jnp.swapaxes(a, -3, -2).reshape(*lead, XATTN_HEADS, XATTN_HEAD_DIM)


def _attend_rows(q_ref, k_ref, v_ref, o_ref, rows):
    for r in rows:
        part = jnp.sum(k_ref[r] * q_ref[r][None], axis=-1, keepdims=True)
        s = part + pltpu.roll(part, XATTN_HEADS, axis=1)
        e = jnp.exp(s - jnp.max(s, axis=0, keepdims=True))
        o_ref[r] = jnp.sum(e * v_ref[r], axis=0) * (1.0 / jnp.sum(e, axis=0))


def _ffn_rows(x, g_ref, w1_ref, w2_ref, gf_ref, *, final_norm, between_chunks=None):
    h = _rms(x, g_ref[...]).astype(BF16)
    y = x
    for i, c in enumerate(range(0, D_FF, FF_CHUNK)):
        if between_chunks is not None:
            between_chunks(i)
        a = jnp.maximum(_dot(h, w1_ref[:, c:c + FF_CHUNK]), 0.0)
        y = y + _dot((a * a).astype(BF16), w2_ref[c:c + FF_CHUNK, :])
    return _rms(y, gf_ref[...]) if final_norm else y


def _ffn_kernel(x_ref, g_ref, w1_ref, w2_ref, gf_ref, o_ref, *, final_norm, between_chunks=None):
    o_ref[...] = _ffn_rows(x_ref[...], g_ref, w1_ref, w2_ref, gf_ref, final_norm=final_norm,
                           between_chunks=between_chunks)


def _oproj_ffn_kernel(x_ref, a_ref, wo_ref, g_ref, w1_ref, w2_ref, gf_ref, o_ref, *, final_norm):
    x = x_ref[...] + _dot(a_ref[...].astype(BF16), wo_ref[...])
    o_ref[...] = _ffn_rows(x, g_ref, w1_ref, w2_ref, gf_ref, final_norm=final_norm)


def _oproj_ffn(x, attn, lw, norm_final, *, final_norm):
    args = [x, attn, lw['w_xo'], lw['norm_ffn'].reshape(1, -1), lw['w_ff1'], lw['w_ff2'], norm_final.reshape(1, -1)]
    return pl.pallas_call(
        functools.partial(_oproj_ffn_kernel, final_norm=final_norm),
        grid=(1,),
        in_specs=[_const_spec(a.shape) for a in args],
        out_specs=pl.BlockSpec(x.shape, lambda i: (0, 0)),
        out_shape=jax.ShapeDtypeStruct(x.shape, F32),
        compiler_params=pltpu.CompilerParams(dimension_semantics=("arbitrary",), vmem_limit_bytes=VMEM_LIMIT),
        name="oproj_ffn",
    )(*args)


FFN_ATTEND_INPUTS = 8
FFN_ATTEND_OUTPUTS = 2


def _ffn_attend_kernel(*refs, final_norm, rows, n_cast):
    x_ref, g_ref, w1_ref, w2_ref, gf_ref, q_ref, k_ref, v_ref = refs[:FFN_ATTEND_INPUTS]
    cast_src = refs[FFN_ATTEND_INPUTS:FFN_ATTEND_INPUTS + n_cast]
    o_ref, oa_ref = refs[FFN_ATTEND_INPUTS + n_cast:FFN_ATTEND_INPUTS + n_cast + FFN_ATTEND_OUTPUTS]
    cast_dst = refs[FFN_ATTEND_INPUTS + n_cast + FFN_ATTEND_OUTPUTS:]
    _run_cast_jobs(cast_src, cast_dst)
    n_chunks = D_FF // FF_CHUNK
    assert rows % n_chunks == 0
    per_chunk = rows // n_chunks
    attend = lambda i: _attend_rows(q_ref, k_ref, v_ref, oa_ref, range(i * per_chunk, (i + 1) * per_chunk))
    _ffn_kernel(x_ref, g_ref, w1_ref, w2_ref, gf_ref, o_ref, final_norm=final_norm, between_chunks=attend)


def _ffn_attend(x2d, lw, norm_final, attend, *, final_norm, tm=512, cast=()):
    n = x2d.shape[0]
    g = lw['norm_ffn'].reshape(1, -1)
    gf = norm_final.reshape(1, -1)
    in_specs = [pl.BlockSpec((tm, D_MODEL), lambda i: (i, 0)), _const_spec(g.shape),
                _const_spec(lw['w_ff1'].shape), _const_spec(lw['w_ff2'].shape), _const_spec(gf.shape)]
    out_spec = pl.BlockSpec((tm, D_MODEL), lambda i: (i, 0))
    out_shape = jax.ShapeDtypeStruct(x2d.shape, F32)
    params = pltpu.CompilerParams(dimension_semantics=("arbitrary",), vmem_limit_bytes=VMEM_LIMIT)
    q, cache_k, cache_v, layer = attend
    rows = q.shape[0] // (n // tm)
    assert rows * (n // tm) == q.shape[0]
    q_spec = pl.BlockSpec((rows,) + q.shape[1:], lambda i: (i, 0, 0))
    kv_spec = pl.BlockSpec((None, rows) + cache_k.shape[2:], lambda i: (layer, i, 0, 0, 0))
    jobs = [_cast_job(w, wl, n // tm, lambda i: i) for w, wl in cast]
    outs = pl.pallas_call(
        functools.partial(_ffn_attend_kernel, final_norm=final_norm, rows=rows, n_cast=len(jobs)), grid=(n // tm,),
        in_specs=in_specs + [q_spec, kv_spec, kv_spec] + [j[0] for j in jobs],
        out_specs=[out_spec, q_spec] + [j[1] for j in jobs],
        out_shape=[out_shape, jax.ShapeDtypeStruct(q.shape, F32)] + [j[2] for j in jobs],
        compiler_params=params, name="ffn_attend",
    )(x2d, g, lw['w_ff1'], lw['w_ff2'], gf, q, cache_k, cache_v, *[w for w, _ in cast])
    return outs[0], outs[1], list(outs[FFN_ATTEND_OUTPUTS:])


PROJ_WEIGHTS = ('w_in', 'w_out')


def _layer_weights(l, p, proj_bf16):
    tril = jnp.tril(jnp.ones((CHUNK, CHUNK), dtype=bool))
    ws = jnp.where(tril[None], p['gmlp_ws'][l], 0.0)
    pool_bd = jax.scipy.linalg.block_diag(*[p['pool_w'][l, g] for g in range(len(POOL_WINDOWS))])
    lw = {k: p[k][l] for k in ('norm_mix', 'gmlp_ln_g', 'gmlp_ln_b', 'conf_dw', 'conf_dw_b', 'conf_ln_g',
                               'conf_ln_b', 'sc_dw', 'pool_scale', 'mix_out_g', 'norm_xattn', 'norm_ffn')}
    lw.update(proj_bf16)
    lw.update(
        gmlp_wcat=jnp.concatenate([ws[h] for h in range(GMLP_HEADS)], axis=1).astype(BF16),
        gmlp_bias_tile=jnp.repeat(p['gmlp_bs'][l].T, GMLP_HEAD_DIM, axis=1),
        gmlp_w00=jnp.repeat(ws[:, 0, 0], GMLP_HEAD_DIM), gmlp_b0=jnp.repeat(p['gmlp_bs'][l][:, 0], GMLP_HEAD_DIM),
        pool_w_bd=pool_bd.astype(BF16))
    return lw


def kernel(x_prompt, x_sample, mem_prompt, cache_mem_k, cache_mem_v, state_conv_glu, state_conv_short, state_pool, norm_mix, w_in, gmlp_ln_g, gmlp_ln_b, gmlp_ws, gmlp_bs, conf_dw, conf_dw_b, conf_ln_g, conf_ln_b, sc_dw, pool_w, pool_scale, mix_out_g, w_out, norm_xattn, norm_mem, w_xq, w_xk, w_xv, w_xo, norm_ffn, w_ff1, w_ff2, norm_final):
    params = dict(norm_mix=norm_mix, w_in=w_in, gmlp_ln_g=gmlp_ln_g, gmlp_ln_b=gmlp_ln_b, gmlp_ws=gmlp_ws,
                  gmlp_bs=gmlp_bs, conf_dw=conf_dw, conf_dw_b=conf_dw_b, conf_ln_g=conf_ln_g, conf_ln_b=conf_ln_b,
                  sc_dw=sc_dw, pool_w=pool_w, pool_scale=pool_scale, mix_out_g=mix_out_g, w_out=w_out,
                  norm_xattn=norm_xattn, norm_mem=norm_mem, w_xq=w_xq, w_xk=w_xk, w_xv=w_xv, w_xo=w_xo,
                  norm_ffn=norm_ffn, w_ff1=w_ff1, w_ff2=w_ff2)
    depth = w_in.shape[0]
    bsz, seq, _ = x_prompt.shape
    n_s = x_sample.shape[0]
    xp = x_prompt
    xs = x_sample.reshape(n_s, D_MODEL)
    outs = {k: [] for k in ('glu_p', 'glu_s', 'sh_p', 'sh_s', 'pl_p', 'pl_s', 'v_s')}
    mem_k, mem_v, mem_qk, mem_vo, wq_bf16, wo_bf16, proj = _mem_kv(
        mem_prompt, norm_mem, w_xk, w_xv, w_xq, w_xo, cast=[(params[k], 0) for k in PROJ_WEIGHTS])
    cache_k_split = _split_head_dim(cache_mem_k)
    cache_v_split = _split_head_dim(cache_mem_v)
    for l in range(depth):
        lw = _layer_weights(l, params, dict(zip(PROJ_WEIGHTS, proj), w_xq=wq_bf16[l], w_xo=wo_bf16[l]))
        last = l == depth - 1
        xp, glu_p, sh_p, pool_p, (lw['w_ff1'], lw['w_ff2']) = _mixer_prompt(
            xp, lw, cast=[(params['w_ff1'], l), (params['w_ff2'], l)])
        xp = _xattn_prompt(xp, mem_qk, mem_vo, l, lw)
        outs['glu_p'].append(glu_p); outs['sh_p'].append(sh_p); outs['pl_p'].append(pool_p)
        xs, q_s, vn_s, glu_s, sh_s, pool_s = _mixer_sample(xs, state_conv_glu, state_conv_short, state_pool, l, lw)
        q_s = _split_head_dim(q_s.reshape(n_s, XATTN_HEADS, XATTN_HEAD_DIM))
        xp, o_s, proj = _ffn_attend(xp.reshape(bsz * seq, D_MODEL), lw, norm_final,
                                    (q_s, cache_k_split, cache_v_split, l), final_norm=last,
                                    cast=[] if last else [(params[k], l + 1) for k in PROJ_WEIGHTS])
        xp = xp.reshape(bsz, seq, D_MODEL)
        xs = _oproj_ffn(xs, _merge_head_dim(o_s).reshape(n_s, D_MODEL), lw, norm_final, final_norm=last)
        outs['glu_s'].append(glu_s); outs['sh_s'].append(sh_s); outs['pl_s'].append(pool_s)
        outs['v_s'].append(vn_s.reshape(n_s, 1, D_GROUP))
    st = lambda k: jnp.stack(outs[k], axis=0)
    return (xp, xs.reshape(n_s, 1, D_MODEL), mem_k, mem_v, st('glu_p'), st('glu_s'), st('sh_p'), st('sh_s'),
            st('pl_p'), st('pl_s'), st('v_s'))
```

```python
import functools

import jax
import jax.numpy as jnp
from jax import lax
from jax.experimental import pallas as pl
from jax.experimental.pallas import tpu as pltpu

F32 = jnp.float32
BF16 = jnp.bfloat16

D_MODEL = 1024
D_GROUP = 256
N_GROUPS = 4
D_IN_PROJ = 8 * D_GROUP
GMLP_HEADS = 4
GMLP_HEAD_DIM = D_GROUP // GMLP_HEADS
CHUNK = 128
CONF_WIDTH = 31
SC_WIDTH = 3
POOL_WINDOWS = (2, 4, 8, 16)
POOL_GROUP_DIM = D_GROUP // len(POOL_WINDOWS)
POOL_BUF = max(POOL_WINDOWS) - 1
MEM_LEN = 256
XATTN_HEADS = 4
XATTN_HEAD_DIM = D_MODEL // XATTN_HEADS
D_FF = 4 * D_MODEL
PAST_LEN = 16384
EPS = 1e-6

GLU_HALO = 32
SC_HALO = 8
POOL_HALO = 32
LANES = 128
SUBLANES = 8
BF16_SUBLANES = 16
LANE_SLABS = D_GROUP // LANES
CONV_STRIDE = 4
CONV_ROWS = 128
FF_CHUNK = 4096
VMEM_LIMIT = 56 * 1024 * 1024


def _rms(x, g):
    return x * lax.rsqrt(jnp.mean(x * x, axis=-1, keepdims=True) + EPS) * g


def _layer_norm(x, g, b):
    xc = x - jnp.mean(x, axis=-1, keepdims=True)
    return xc * lax.rsqrt(jnp.mean(xc * xc, axis=-1, keepdims=True) + EPS) * g + b


def _sigmoid(x):
    return 0.5 * jnp.tanh(0.5 * x) + 0.5


def _dot(a, b):
    return jnp.dot(a, b, preferred_element_type=F32)


def _const_spec(shape):
    zeros = (0,) * len(shape)
    return pl.BlockSpec(shape, lambda *_: zeros, pipeline_mode=pl.Buffered(1))


def _layer_spec(stacked, layer):
    zeros = (0,) * (stacked.ndim - 1)
    return pl.BlockSpec((None,) + stacked.shape[1:], lambda *_: (layer,) + zeros, pipeline_mode=pl.Buffered(1))


def _cast_job(stacked, layer, steps, step_of):
    _, r, c = stacked.shape
    blk = r // steps
    assert blk * steps == r and blk % BF16_SUBLANES == 0
    return (pl.BlockSpec((None, blk, c), lambda *ids: (layer, step_of(*ids), 0)),
            pl.BlockSpec((blk, c), lambda *ids: (step_of(*ids), 0)),
            jax.ShapeDtypeStruct((r, c), BF16))


def _run_cast_jobs(src_refs, dst_refs):
    for src, dst in zip(src_refs, dst_refs):
        dst[...] = src[...].astype(BF16)


def _pool_window_lanes(shape):
    grp = lax.broadcasted_iota(jnp.int32, shape, len(shape) - 1) // POOL_GROUP_DIM
    win = jnp.full(shape, float(POOL_WINDOWS[-1]), F32)
    for g in range(len(POOL_WINDOWS) - 2, -1, -1):
        win = jnp.where(grp == g, float(POOL_WINDOWS[g]), win)
    return grp, win


def _select_by_group(grp, vals):
    out = vals[-1]
    for g in range(len(vals) - 2, -1, -1):
        out = jnp.where(grp == g, vals[g], out)
    return out


def _mix_out(x, y_groups, mog_ref, wout_ref):
    yn = [_rms(y, mog_ref[:, g * D_GROUP:(g + 1) * D_GROUP]).astype(BF16) for g, y in enumerate(y_groups)]
    return x + _dot(jnp.concatenate(yn, axis=1), wout_ref[...])


MIXER_INPUTS = 16
MIXER_OUTPUTS = 4


def _mixer_prompt_kernel(*refs, tt, n_t, sub, n_cast):
    (x_ref, g_ref, win_ref, lng_ref, lnb_ref, wcat_ref, bst_ref, cw_ref, cb_ref, clg_ref,
     clb_ref, sw_ref, pw_ref, ps_ref, mog_ref, wout_ref) = refs[:MIXER_INPUTS]
    cast_src = refs[MIXER_INPUTS:MIXER_INPUTS + n_cast]
    outs = refs[MIXER_INPUTS + n_cast:]
    o_ref, oglu_ref, osh_ref, opool_ref = outs[:MIXER_OUTPUTS]
    cast_dst = outs[MIXER_OUTPUTS:MIXER_OUTPUTS + n_cast]
    glu_buf0, glu_buf1, yb_buf0, yb_buf1, sc_buf, p_buf, s2_buf, s4_buf, s8_buf = outs[MIXER_OUTPUTS + n_cast:]
    _run_cast_jobs(cast_src, cast_dst)
    t = pl.program_id(1)
    glu_buf = (glu_buf0, glu_buf1)
    yb_buf = (yb_buf0, yb_buf1)

    @pl.when(t == 0)
    def _():
        for s in range(LANE_SLABS):
            glu_buf[s][0:GLU_HALO, :] = jnp.zeros((GLU_HALO, LANES), F32)
        sc_buf[0:SC_HALO, :] = jnp.zeros((SC_HALO, D_GROUP), F32)
        p_buf[0:POOL_HALO, :] = jnp.zeros((POOL_HALO, D_GROUP), F32)

    for r0 in range(0, tt, sub):
        _mix_rows(t, r0, sub, x_ref, g_ref, win_ref, lng_ref, lnb_ref, wcat_ref, bst_ref, cw_ref, cb_ref, clg_ref,
                  clb_ref, sw_ref, pw_ref, ps_ref, mog_ref, wout_ref, o_ref,
                  glu_buf, yb_buf, sc_buf, p_buf, s2_buf, s4_buf, s8_buf, tt=tt)

    @pl.when(t == n_t - 1)
    def _():
        oglu_ref[0] = jnp.concatenate(
            [glu_buf[s][GLU_HALO + tt - (CONF_WIDTH - 1):GLU_HALO + tt, :] for s in range(LANE_SLABS)], axis=1)
        osh_ref[0] = sc_buf[SC_HALO + tt - (SC_WIDTH - 1):SC_HALO + tt, :]
        opool_ref[0] = p_buf[POOL_HALO + tt - POOL_BUF:POOL_HALO + tt, :]

    for s in range(LANE_SLABS):
        glu_buf[s][0:GLU_HALO, :] = glu_buf[s][tt:tt + GLU_HALO, :]
    sc_buf[0:SC_HALO, :] = sc_buf[tt:tt + SC_HALO, :]
    p_buf[0:POOL_HALO, :] = p_buf[tt:tt + POOL_HALO, :]


def _mix_rows(t, r0, sub, x_ref, g_ref, win_ref, lng_ref, lnb_ref, wcat_ref, bst_ref, cw_ref, cb_ref, clg_ref,
              clb_ref, sw_ref, pw_ref, ps_ref, mog_ref, wout_ref, o_ref,
              glu_buf, yb_buf, sc_buf, p_buf, s2_buf, s4_buf, s8_buf, *, tt):
    x = x_ref[0, r0:r0 + sub, :]
    z = _dot(_rms(x, g_ref[...]).astype(BF16), win_ref[...])
    zcol = lambda i: z[:, i * D_GROUP:(i + 1) * D_GROUP]
    group_norm = lambda g, y: _rms(y, mog_ref[:, g * D_GROUP:(g + 1) * D_GROUP]).astype(BF16)
    yn = [None] * N_GROUPS

    vn = _layer_norm(zcol(1), lng_ref[...], lnb_ref[...])
    head = lax.broadcasted_iota(jnp.int32, (CHUNK, D_GROUP), 1) // GMLP_HEAD_DIM
    gates = []
    for c in range(sub // CHUNK):
        vc = vn[c * CHUNK:(c + 1) * CHUNK]
        stack = jnp.concatenate([jnp.where(head == hh, vc, 0.0) for hh in range(GMLP_HEADS)], axis=0)
        gates.append(_dot(wcat_ref[...], stack.astype(BF16)) + bst_ref[...])
    yn[0] = group_norm(0, zcol(0) * jnp.concatenate(gates, axis=0))

    first = GLU_HALO - (CONF_WIDTH - 1)
    span = SUBLANES * CONV_STRIDE
    glu = zcol(2) * _sigmoid(zcol(3))
    for s in range(LANE_SLABS):
        glu_buf[s][GLU_HALO + r0:GLU_HALO + r0 + sub, :] = glu[:, s * LANES:(s + 1) * LANES]
    for c0 in range(r0, r0 + sub, CONV_ROWS):
        starts = [c0 + (i // CONV_STRIDE) * span + i % CONV_STRIDE for i in range(CONV_ROWS // SUBLANES)]
        accs = [[None] * len(starts) for _ in range(LANE_SLABS)]
        for k in range(CONF_WIDTH):
            for s in range(LANE_SLABS):
                wk = jnp.broadcast_to(cw_ref[k:k + 1, s * LANES:(s + 1) * LANES], (SUBLANES, LANES))
                for i, t0 in enumerate(starts):
                    term = wk * glu_buf[s][pl.ds(first + t0 + k, SUBLANES, stride=CONV_STRIDE), :]
                    accs[s][i] = term if k == 0 else accs[s][i] + term
        conv = jnp.concatenate([jnp.concatenate(a, axis=0) for a in accs], axis=1)
        ln = _layer_norm(conv + cb_ref[...], clg_ref[...], clb_ref[...])
        yb = ln * _sigmoid(ln)
        for s in range(LANE_SLABS):
            for i, t0 in enumerate(starts):
                yb_buf[s][pl.ds(t0, SUBLANES, stride=CONV_STRIDE), :] = (
                    yb[i * SUBLANES:(i + 1) * SUBLANES, s * LANES:(s + 1) * LANES])
    yn[1] = group_norm(1, jnp.concatenate([yb_buf[s][r0:r0 + sub, :] for s in range(LANE_SLABS)], axis=1))

    sxc = zcol(5) * zcol(6)
    lo = SC_HALO + r0
    sc_buf[lo:lo + sub, :] = sxc
    conv_c = (sw_ref[0:1, :] * sc_buf[lo - 2:lo - 2 + sub, :]
              + sw_ref[1:2, :] * sc_buf[lo - 1:lo - 1 + sub, :]
              + sw_ref[2:3, :] * sxc)
    yn[2] = group_norm(2, zcol(4) * conv_c)

    pool_x = zcol(7)
    lo = POOL_HALO + r0
    end = lo + sub
    p_buf[lo:end, :] = pool_x
    lo2, lo4, lo8 = (8, 16, 24) if r0 == 0 else (lo, lo, lo)
    s2_buf[lo2:end, :] = p_buf[lo2:end, :] + p_buf[lo2 - 1:end - 1, :]
    s4_buf[lo4:end, :] = s2_buf[lo4:end, :] + s2_buf[lo4 - 2:end - 2, :]
    s8_buf[lo8:end, :] = s4_buf[lo8:end, :] + s4_buf[lo8 - 4:end - 4, :]
    s16 = s8_buf[lo:end, :] + s8_buf[lo - 8:end - 8, :]
    grp = _pool_window_lanes((sub, D_GROUP))[0]
    ssum = _select_by_group(grp, [s2_buf[lo:end, :], s4_buf[lo:end, :], s8_buf[lo:end, :], s16])
    head_rows = POOL_BUF + 1 if r0 == 0 else 0
    inv_win = _select_by_group(_pool_window_lanes((sub - head_rows, D_GROUP))[0], [1.0 / w for w in POOL_WINDOWS])
    mean = ssum[head_rows:] * inv_win
    if head_rows:
        pos = t * tt + lax.broadcasted_iota(jnp.int32, (head_rows, D_GROUP), 0)
        cnt = jnp.minimum(_pool_window_lanes((head_rows, D_GROUP))[1], (pos + 1).astype(F32))
        mean = jnp.concatenate([ssum[:head_rows] / cnt, mean], axis=0)
    pooled = mean - pool_x
    yn[3] = group_norm(3, _dot(pooled.astype(BF16), pw_ref[...]) * ps_ref[...])

    o_ref[0, r0:r0 + sub, :] = x + _dot(jnp.concatenate(yn, axis=1), wout_ref[...])


def _mixer_prompt(x, lw, *, tt=1024, sub=1024, cast=()):
    bsz, seq, _ = x.shape
    n_t = seq // tt
    row = lambda a: a.reshape(1, -1)
    small = [row(lw['norm_mix']), lw['w_in'], row(lw['gmlp_ln_g']), row(lw['gmlp_ln_b']), lw['gmlp_wcat'],
             lw['gmlp_bias_tile'], lw['conf_dw'], row(lw['conf_dw_b']), row(lw['conf_ln_g']), row(lw['conf_ln_b']),
             lw['sc_dw'], lw['pool_w_bd'], row(lw['pool_scale']), row(lw['mix_out_g']), lw['w_out']]
    assert 1 + len(small) == MIXER_INPUTS
    jobs = [_cast_job(w, layer, bsz * n_t, lambda b, t: b * n_t + t) for w, layer in cast]
    state_spec = lambda rows: pl.BlockSpec((1, rows, D_GROUP), lambda b, t: (b, 0, 0))
    outs = pl.pallas_call(
        functools.partial(_mixer_prompt_kernel, tt=tt, n_t=n_t, sub=sub, n_cast=len(jobs)),
        grid=(bsz, n_t),
        in_specs=[pl.BlockSpec((1, tt, D_MODEL), lambda b, t: (b, t, 0))] + [_const_spec(a.shape) for a in small]
        + [j[0] for j in jobs],
        out_specs=[pl.BlockSpec((1, tt, D_MODEL), lambda b, t: (b, t, 0)),
                   state_spec(CONF_WIDTH - 1), state_spec(SC_WIDTH - 1), state_spec(POOL_BUF)] + [j[1] for j in jobs],
        out_shape=[jax.ShapeDtypeStruct(x.shape, F32),
                   jax.ShapeDtypeStruct((bsz, CONF_WIDTH - 1, D_GROUP), F32),
                   jax.ShapeDtypeStruct((bsz, SC_WIDTH - 1, D_GROUP), F32),
                   jax.ShapeDtypeStruct((bsz, POOL_BUF, D_GROUP), F32)] + [j[2] for j in jobs],
        scratch_shapes=[pltpu.VMEM((GLU_HALO + tt, LANES), F32)] * LANE_SLABS + [pltpu.VMEM((tt, LANES), F32)] * LANE_SLABS
        + [pltpu.VMEM((SC_HALO + tt, D_GROUP), F32)]
        + [pltpu.VMEM((POOL_HALO + tt, D_GROUP), F32)] * 4,
        compiler_params=pltpu.CompilerParams(dimension_semantics=("arbitrary", "arbitrary"),
                                             vmem_limit_bytes=VMEM_LIMIT),
        name="mixer_prompt",
    )(x, *small, *[w for w, _ in cast])
    return (*outs[:MIXER_OUTPUTS], list(outs[MIXER_OUTPUTS:]))


def _mixer_sample_kernel(x_ref, stg_ref, sts_ref, stp_ref, g_ref, win_ref, lng_ref, lnb_ref, w00_ref, b0_ref,
                         cw_ref, cb_ref, clg_ref, clb_ref, sw_ref, pw_ref, ps_ref, mog_ref, wout_ref, gx_ref, wq_ref,
                         o_ref, oq_ref, ovn_ref, oglu_ref, osh_ref, opool_ref):
    x = x_ref[...]
    h = _rms(x, g_ref[...]).astype(BF16)
    z = _dot(h, win_ref[...])
    u, v, glu_a, glu_g, sc_b, sc_c, sc_x, pool_x = [z[:, i * D_GROUP:(i + 1) * D_GROUP] for i in range(8)]
    hist = lambda ref, k: ref[:, k, :]

    def push(new_ref, old_ref, row):
        n_hist = old_ref.shape[1]
        new_ref[:, 0:n_hist - 1, :] = old_ref[:, 1:n_hist, :]
        new_ref[:, n_hist - 1, :] = row

    vn = _layer_norm(v, lng_ref[...], lnb_ref[...])
    ovn_ref[...] = vn
    y_a = u * (w00_ref[...] * vn + b0_ref[...])

    glu = glu_a * _sigmoid(glu_g)
    n_hist = CONF_WIDTH - 1
    acc = cw_ref[n_hist:n_hist + 1, :] * glu
    for k in range(n_hist):
        acc = acc + cw_ref[k:k + 1, :] * hist(stg_ref, k)
    ln = _layer_norm(acc + cb_ref[...], clg_ref[...], clb_ref[...])
    y_b = ln * _sigmoid(ln)
    push(oglu_ref, stg_ref, glu)

    sxc = sc_c * sc_x
    y_c = sc_b * (sw_ref[0:1, :] * hist(sts_ref, 0) + sw_ref[1:2, :] * hist(sts_ref, 1) + sw_ref[2:3, :] * sxc)
    push(osh_ref, sts_ref, sxc)

    run = pool_x
    sums = []
    back = 0
    for w in POOL_WINDOWS:
        while back < w - 1:
            run = run + hist(stp_ref, POOL_BUF - 1 - back)
            back += 1
        sums.append(run)
    grp, win = _pool_window_lanes(pool_x.shape)
    cnt = jnp.minimum(win, float(PAST_LEN + 1))
    pooled = _select_by_group(grp, sums) / cnt - pool_x
    y_d = _dot(pooled.astype(BF16), pw_ref[...]) * ps_ref[...]
    push(opool_ref, stp_ref, pool_x)

    x_mixed = _mix_out(x, [y_a, y_b, y_c, y_d], mog_ref, wout_ref)
    o_ref[...] = x_mixed
    oq_ref[...] = _dot(_rms(x_mixed, gx_ref[...]).astype(BF16), wq_ref[...]) * (XATTN_HEAD_DIM ** -0.5)


def _mixer_sample(xs, st_glu, st_sh, st_pool, layer, lw):
    n = xs.shape[0]
    row = lambda a: a.reshape(1, -1)
    states = [st_glu, st_sh, st_pool]
    consts = [row(lw['norm_mix']), lw['w_in'], row(lw['gmlp_ln_g']), row(lw['gmlp_ln_b']), row(lw['gmlp_w00']),
              row(lw['gmlp_b0']), lw['conf_dw'], row(lw['conf_dw_b']), row(lw['conf_ln_g']), row(lw['conf_ln_b']),
              lw['sc_dw'], lw['pool_w_bd'], row(lw['pool_scale']), row(lw['mix_out_g']), lw['w_out'],
              row(lw['norm_xattn'])]
    out_shapes = [(n, D_MODEL), (n, D_MODEL), (n, D_GROUP)] + [s.shape[1:] for s in states]
    return pl.pallas_call(
        _mixer_sample_kernel,
        grid=(1,),
        in_specs=[_const_spec(xs.shape)] + [_layer_spec(s, layer) for s in states]
        + [_const_spec(a.shape) for a in consts] + [_layer_spec(lw['w_xq_layers'], layer)],
        out_specs=[pl.BlockSpec(s, lambda i, nd=len(s): (0,) * nd) for s in out_shapes],
        out_shape=[jax.ShapeDtypeStruct(s, F32) for s in out_shapes],
        compiler_params=pltpu.CompilerParams(dimension_semantics=("arbitrary",), vmem_limit_bytes=VMEM_LIMIT),
        name="mixer_sample",
    )(xs, *states, *consts, lw['w_xq_layers'])


MEM_KV_INPUTS = 6
MEM_KV_OUTPUTS = 6


def _mem_kv_kernel(*refs, n_cast):
    m_ref, g_ref, wk_ref, wv_ref, wq_ref, wo_ref = refs[:MEM_KV_INPUTS]
    cast_src = refs[MEM_KV_INPUTS:MEM_KV_INPUTS + n_cast]
    outs = refs[MEM_KV_INPUTS + n_cast:]
    ok_ref, ov_ref, oqk_ref, ovo_ref, wq_bf16, wo_bf16 = outs[:MEM_KV_OUTPUTS]
    cast_dst = outs[MEM_KV_OUTPUTS:MEM_KV_OUTPUTS + n_cast]
    wk_bf16, wv_bf16 = outs[MEM_KV_OUTPUTS + n_cast:]
    _run_cast_jobs(cast_src, cast_dst)

    @pl.when(pl.program_id(1) == 0)
    def _():
        for src, dst in ((wk_ref, wk_bf16), (wv_ref, wv_bf16), (wq_ref, wq_bf16), (wo_ref, wo_bf16)):
            dst[...] = src[...].astype(BF16)

    m = _rms(m_ref[0], g_ref[...]).astype(BF16)
    k = _dot(m, wk_bf16[...])
    v = _dot(m, wv_bf16[...])
    contract_last = (((1,), (1,)), ((), ()))
    for h in range(XATTN_HEADS):
        lo, hi = h * XATTN_HEAD_DIM, (h + 1) * XATTN_HEAD_DIM
        ok_ref[0, :, h, :] = k[:, lo:hi]
        ov_ref[0, :, h, :] = v[:, lo:hi]
        qk = lax.dot_general(wq_bf16[:, lo:hi], k[:, lo:hi].astype(BF16), contract_last,
                             preferred_element_type=F32)
        oqk_ref[0, :, h * MEM_LEN:(h + 1) * MEM_LEN] = (qk * (XATTN_HEAD_DIM ** -0.5)).astype(BF16)
        ovo_ref[0, h * MEM_LEN:(h + 1) * MEM_LEN, :] = _dot(v[:, lo:hi].astype(BF16), wo_bf16[lo:hi, :]).astype(BF16)


def _mem_kv(mem, norm_mem, w_xk, w_xv, w_xq, w_xo, *, cast=()):
    depth = w_xk.shape[0]
    bsz = mem.shape[0]
    g = norm_mem.reshape(depth, 1, D_MODEL)
    per_layer = lambda *shape, **kw: pl.BlockSpec((None,) + shape, lambda l, b: (l,) + (0,) * len(shape), **kw)
    out_blk = lambda *shape: pl.BlockSpec((None, 1) + shape, lambda l, b: (l, b) + (0,) * len(shape))
    weights = [w_xk, w_xv, w_xq, w_xo]
    jobs = [_cast_job(w, layer, depth * bsz, lambda l, b: l * bsz + b) for w, layer in cast]
    outs = pl.pallas_call(
        functools.partial(_mem_kv_kernel, n_cast=len(jobs)),
        grid=(depth, bsz),
        in_specs=[pl.BlockSpec((1, MEM_LEN, D_MODEL), lambda l, b: (b, 0, 0)), per_layer(1, D_MODEL)]
        + [per_layer(D_MODEL, D_MODEL, pipeline_mode=pl.Buffered(1))] * len(weights) + [j[0] for j in jobs],
        out_specs=[out_blk(MEM_LEN, XATTN_HEADS, XATTN_HEAD_DIM), out_blk(MEM_LEN, XATTN_HEADS, XATTN_HEAD_DIM),
                   out_blk(D_MODEL, XATTN_HEADS * MEM_LEN), out_blk(XATTN_HEADS * MEM_LEN, D_MODEL),
                   per_layer(D_MODEL, D_MODEL), per_layer(D_MODEL, D_MODEL)] + [j[1] for j in jobs],
        out_shape=[jax.ShapeDtypeStruct((depth, bsz, MEM_LEN, XATTN_HEADS, XATTN_HEAD_DIM), F32)] * 2
        + [jax.ShapeDtypeStruct((depth, bsz, D_MODEL, XATTN_HEADS * MEM_LEN), BF16),
           jax.ShapeDtypeStruct((depth, bsz, XATTN_HEADS * MEM_LEN, D_MODEL), BF16)]
        + [jax.ShapeDtypeStruct((depth, D_MODEL, D_MODEL), BF16)] * 2 + [j[2] for j in jobs],
        scratch_shapes=[pltpu.VMEM((D_MODEL, D_MODEL), BF16)] * 2,
        compiler_params=pltpu.CompilerParams(dimension_semantics=("arbitrary", "arbitrary"),
                                             vmem_limit_bytes=VMEM_LIMIT),
        name="mem_kv",
    )(mem, g, *weights, *[w for w, _ in cast])
    return (*outs[:MEM_KV_OUTPUTS], list(outs[MEM_KV_OUTPUTS:]))


def _xattn_prompt_kernel(x_ref, g_ref, qk_ref, vo_ref, o_ref):
    x = x_ref[0]
    h = _rms(x, g_ref[...]).astype(BF16)
    s = _dot(h, qk_ref[0])
    probs = []
    for hh in range(XATTN_HEADS):
        sh = s[:, hh * MEM_LEN:(hh + 1) * MEM_LEN]
        e = jnp.exp(sh - jnp.max(sh, axis=-1, keepdims=True))
        probs.append((e * (1.0 / jnp.sum(e, axis=-1, keepdims=True))).astype(BF16))
    o_ref[0] = x + _dot(jnp.concatenate(probs, axis=1), vo_ref[0])


def _xattn_prompt(x, qk, vo, layer, lw, *, tt=1024):
    bsz, seq, _ = x.shape
    g = lw['norm_xattn'].reshape(1, -1)
    folded = lambda a: pl.BlockSpec((None, 1) + a.shape[2:], lambda b, t: (layer, b, 0, 0))
    return pl.pallas_call(
        _xattn_prompt_kernel,
        grid=(bsz, seq // tt),
        in_specs=[pl.BlockSpec((1, tt, D_MODEL), lambda b, t: (b, t, 0)), _const_spec(g.shape),
                  folded(qk), folded(vo)],
        out_specs=pl.BlockSpec((1, tt, D_MODEL), lambda b, t: (b, t, 0)),
        out_shape=jax.ShapeDtypeStruct(x.shape, F32),
        compiler_params=pltpu.CompilerParams(dimension_semantics=("arbitrary", "arbitrary"),
                                             vmem_limit_bytes=VMEM_LIMIT),
        name="xattn_prompt",
    )(x, g, qk, vo)


def _split_head_dim(a):
    lead = a.shape[:-2]
    a = a.reshape(*lead, XATTN_HEADS, 2, XATTN_HEAD_DIM // 2)
    return jnp.swapaxes(a, -3, -2).reshape(*lead, 2 * XATTN_HEADS, XATTN_HEAD_DIM // 2)


def _merge_head_dim(a):
    lead = a.shape[:-2]
    a = a.reshape(*lead, 2, XATTN_HEADS, XATTN_HEAD_DIM // 2)
    return jnp.swapaxes(a, -3, -2).reshape(*lead, XATTN_HEADS, XATTN_HEAD_DIM)


def _attend_rows(q_ref, k_ref, v_ref, o_ref, rows):
    for r in rows:
        part = jnp.sum(k_ref[r] * q_ref[r][None], axis=-1, keepdims=True)
        s = part + pltpu.roll(part, XATTN_HEADS, axis=1)
        e = jnp.exp(s - jnp.max(s, axis=0, keepdims=True))
        o_ref[r] = jnp.sum(e * v_ref[r], axis=0) * (1.0 / jnp.sum(e, axis=0))


def _ffn_rows(x, g_ref, w1_ref, w2_ref, gf_ref, *, final_norm, between_chunks=None):
    h = _rms(x, g_ref[...]).astype(BF16)
    y = x
    for i, c in enumerate(range(0, D_FF, FF_CHUNK)):
        if between_chunks is not None:
            between_chunks(i)
        a = jnp.maximum(_dot(h, w1_ref[:, c:c + FF_CHUNK]), 0.0)
        y = y + _dot((a * a).astype(BF16), w2_ref[c:c + FF_CHUNK, :])
    return _rms(y, gf_ref[...]) if final_norm else y


def _ffn_kernel(x_ref, g_ref, w1_ref, w2_ref, gf_ref, o_ref, *, final_norm, between_chunks=None):
    o_ref[...] = _ffn_rows(x_ref[...], g_ref, w1_ref, w2_ref, gf_ref, final_norm=final_norm,
                           between_chunks=between_chunks)


def _oproj_ffn_kernel(x_ref, a_ref, wo_ref, g_ref, w1_ref, w2_ref, gf_ref, o_ref, *, final_norm):
    x = x_ref[...] + _dot(a_ref[...].astype(BF16), wo_ref[...])
    o_ref[...] = _ffn_rows(x, g_ref, w1_ref, w2_ref, gf_ref, final_norm=final_norm)


def _oproj_ffn(x, attn, layer, lw, norm_final, *, final_norm):
    args = [x, attn, lw['w_xo_layers'], lw['norm_ffn'].reshape(1, -1), lw['w_ff1'], lw['w_ff2'],
            norm_final.reshape(1, -1)]
    return pl.pallas_call(
        functools.partial(_oproj_ffn_kernel, final_norm=final_norm),
        grid=(1,),
        in_specs=[_layer_spec(a, layer) if a is lw['w_xo_layers'] else _const_spec(a.shape) for a in args],
        out_specs=pl.BlockSpec(x.shape, lambda i: (0, 0)),
        out_shape=jax.ShapeDtypeStruct(x.shape, F32),
        compiler_params=pltpu.CompilerParams(dimension_semantics=("arbitrary",), vmem_limit_bytes=VMEM_LIMIT),
        name="oproj_ffn",
    )(*args)


FFN_ATTEND_INPUTS = 8
FFN_ATTEND_OUTPUTS = 2


def _ffn_attend_kernel(*refs, final_norm, rows, n_cast):
    x_ref, g_ref, w1_ref, w2_ref, gf_ref, q_ref, k_ref, v_ref = refs[:FFN_ATTEND_INPUTS]
    cast_src = refs[FFN_ATTEND_INPUTS:FFN_ATTEND_INPUTS + n_cast]
    o_ref, oa_ref = refs[FFN_ATTEND_INPUTS + n_cast:FFN_ATTEND_INPUTS + n_cast + FFN_ATTEND_OUTPUTS]
    cast_dst = refs[FFN_ATTEND_INPUTS + n_cast + FFN_ATTEND_OUTPUTS:]
    _run_cast_jobs(cast_src, cast_dst)
    n_chunks = D_FF // FF_CHUNK
    assert rows % n_chunks == 0
    per_chunk = rows // n_chunks
    attend = lambda i: _attend_rows(q_ref, k_ref, v_ref, oa_ref, range(i * per_chunk, (i + 1) * per_chunk))
    _ffn_kernel(x_ref, g_ref, w1_ref, w2_ref, gf_ref, o_ref, final_norm=final_norm, between_chunks=attend)


def _ffn_attend(x2d, lw, norm_final, attend, *, final_norm, tm=512, cast=()):
    n = x2d.shape[0]
    g = lw['norm_ffn'].reshape(1, -1)
    gf = norm_final.reshape(1, -1)
    in_specs = [pl.BlockSpec((tm, D_MODEL), lambda i: (i, 0)), _const_spec(g.shape),
                _const_spec(lw['w_ff1'].shape), _const_spec(lw['w_ff2'].shape), _const_spec(gf.shape)]
    out_spec = pl.BlockSpec((tm, D_MODEL), lambda i: (i, 0))
    out_shape = jax.ShapeDtypeStruct(x2d.shape, F32)
    params = pltpu.CompilerParams(dimension_semantics=("arbitrary",), vmem_limit_bytes=VMEM_LIMIT)
    q, cache_k, cache_v, layer = attend
    rows = q.shape[0] // (n // tm)
    assert rows * (n // tm) == q.shape[0]
    q_spec = pl.BlockSpec((rows,) + q.shape[1:], lambda i: (i, 0, 0))
    kv_spec = pl.BlockSpec((None, rows) + cache_k.shape[2:], lambda i: (layer, i, 0, 0, 0))
    jobs = [_cast_job(w, wl, n // tm, lambda i: i) for w, wl in cast]
    outs = pl.pallas_call(
        functools.partial(_ffn_attend_kernel, final_norm=final_norm, rows=rows, n_cast=len(jobs)), grid=(n // tm,),
        in_specs=in_specs + [q_spec, kv_spec, kv_spec] + [j[0] for j in jobs],
        out_specs=[out_spec, q_spec] + [j[1] for j in jobs],
        out_shape=[out_shape, jax.ShapeDtypeStruct(q.shape, F32)] + [j[2] for j in jobs],
        compiler_params=params, name="ffn_attend",
    )(x2d, g, lw['w_ff1'], lw['w_ff2'], gf, q, cache_k, cache_v, *[w for w, _ in cast])
    return outs[0], outs[1], list(outs[FFN_ATTEND_OUTPUTS:])


PROJ_WEIGHTS = ('w_in', 'w_out')


def _layer_weights(l, p, proj_bf16):
    tril = jnp.tril(jnp.ones((CHUNK, CHUNK), dtype=bool))
    ws = jnp.where(tril[None], p['gmlp_ws'][l], 0.0)
    pool_bd = jax.scipy.linalg.block_diag(*[p['pool_w'][l, g] for g in range(len(POOL_WINDOWS))])
    lw = {k: p[k][l] for k in ('norm_mix', 'gmlp_ln_g', 'gmlp_ln_b', 'conf_dw', 'conf_dw_b', 'conf_ln_g',
                               'conf_ln_b', 'sc_dw', 'pool_scale', 'mix_out_g', 'norm_xattn', 'norm_ffn')}
    lw.update(proj_bf16)
    lw.update(
        gmlp_wcat=jnp.concatenate([ws[h] for h in range(GMLP_HEADS)], axis=1).astype(BF16),
        gmlp_bias_tile=jnp.repeat(p['gmlp_bs'][l].T, GMLP_HEAD_DIM, axis=1),
        gmlp_w00=jnp.repeat(ws[:, 0, 0], GMLP_HEAD_DIM), gmlp_b0=jnp.repeat(p['gmlp_bs'][l][:, 0], GMLP_HEAD_DIM),
        pool_w_bd=pool_bd.astype(BF16))
    return lw


def kernel(x_prompt, x_sample, mem_prompt, cache_mem_k, cache_mem_v, state_conv_glu, state_conv_short, state_pool, norm_mix, w_in, gmlp_ln_g, gmlp_ln_b, gmlp_ws, gmlp_bs, conf_dw, conf_dw_b, conf_ln_g, conf_ln_b, sc_dw, pool_w, pool_scale, mix_out_g, w_out, norm_xattn, norm_mem, w_xq, w_xk, w_xv, w_xo, norm_ffn, w_ff1, w_ff2, norm_final):
    params = dict(norm_mix=norm_mix, w_in=w_in, gmlp_ln_g=gmlp_ln_g, gmlp_ln_b=gmlp_ln_b, gmlp_ws=gmlp_ws,
                  gmlp_bs=gmlp_bs, conf_dw=conf_dw, conf_dw_b=conf_dw_b, conf_ln_g=conf_ln_g, conf_ln_b=conf_ln_b,
                  sc_dw=sc_dw, pool_w=pool_w, pool_scale=pool_scale, mix_out_g=mix_out_g, w_out=w_out,
                  norm_xattn=norm_xattn, norm_mem=norm_mem, w_xq=w_xq, w_xk=w_xk, w_xv=w_xv, w_xo=w_xo,
                  norm_ffn=norm_ffn, w_ff1=w_ff1, w_ff2=w_ff2)
    depth = w_in.shape[0]
    bsz, seq, _ = x_prompt.shape
    n_s = x_sample.shape[0]
    xp = x_prompt
    xs = x_sample.reshape(n_s, D_MODEL)
    outs = {k: [] for k in ('glu_p', 'glu_s', 'sh_p', 'sh_s', 'pl_p', 'pl_s', 'v_s')}
    mem_k, mem_v, mem_qk, mem_vo, wq_bf16, wo_bf16, proj = _mem_kv(
        mem_prompt, norm_mem, w_xk, w_xv, w_xq, w_xo, cast=[(params[k], 0) for k in PROJ_WEIGHTS])
    cache_k_split = _split_head_dim(cache_mem_k)
    cache_v_split = _split_head_dim(cache_mem_v)
    for l in range(depth):
        lw = _layer_weights(l, params, dict(zip(PROJ_WEIGHTS, proj), w_xq_layers=wq_bf16, w_xo_layers=wo_bf16))
        last = l == depth - 1
        xp, glu_p, sh_p, pool_p, (lw['w_ff1'], lw['w_ff2']) = _mixer_prompt(
            xp, lw, cast=[(params['w_ff1'], l), (params['w_ff2'], l)])
        xp = _xattn_prompt(xp, mem_qk, mem_vo, l, lw)
        outs['glu_p'].append(glu_p); outs['sh_p'].append(sh_p); outs['pl_p'].append(pool_p)
        xs, q_s, vn_s, glu_s, sh_s, pool_s = _mixer_sample(xs, state_conv_glu, state_conv_short, state_pool, l, lw)
        q_s = _split_head_dim(q_s.reshape(n_s, XATTN_HEADS, XATTN_HEAD_DIM))
        xp, o_s, proj = _ffn_attend(xp.reshape(bsz * seq, D_MODEL), lw, norm_final,
                                    (q_s, cache_k_split, cache_v_split, l), final_norm=last,
                                    cast=[] if last else [(params[k], l + 1) for k in PROJ_WEIGHTS])
        xp = xp.reshape(bsz, seq, D_MODEL)
        xs = _oproj_ffn(xs, _merge_head_dim(o_s).reshape(n_s, D_MODEL), l, lw, norm_final, final_norm=last)
        outs['glu_s'].append(glu_s); outs['sh_s'].append(sh_s); outs['pl_s'].append(pool_s)
        outs['v_s'].append(vn_s.reshape(n_s, 1, D_GROUP))
    st = lambda k: jnp.stack(outs[k], axis=0)
    return (xp, xs.reshape(n_s, 1, D_MODEL), mem_k, mem_v, st('glu_p'), st('glu_s'), st('sh_p'), st('sh_s'),
            st('pl_p'), st('pl_s'), st('v_s'))
```

```python
import functools

import jax
import jax.numpy as jnp
from jax import lax
from jax.experimental import pallas as pl
from jax.experimental.pallas import tpu as pltpu

F32 = jnp.float32
BF16 = jnp.bfloat16

D_MODEL = 1024
D_GROUP = 256
N_GROUPS = 4
D_IN_PROJ = 8 * D_GROUP
GMLP_HEADS = 4
GMLP_HEAD_DIM = D_GROUP // GMLP_HEADS
CHUNK = 128
CONF_WIDTH = 31
SC_WIDTH = 3
POOL_WINDOWS = (2, 4, 8, 16)
POOL_GROUP_DIM = D_GROUP // len(POOL_WINDOWS)
POOL_BUF = max(POOL_WINDOWS) - 1
MEM_LEN = 256
XATTN_HEADS = 4
XATTN_HEAD_DIM = D_MODEL // XATTN_HEADS
D_FF = 4 * D_MODEL
PAST_LEN = 16384
EPS = 1e-6

GLU_HALO = 32
SC_HALO = 8
POOL_HALO = 32
LANES = 128
SUBLANES = 8
BF16_SUBLANES = 16
LANE_SLABS = D_GROUP // LANES
CONV_STRIDE = 4
CONV_ROWS = 128
FF_CHUNK = 4096
VMEM_LIMIT = 56 * 1024 * 1024


def _rms(x, g):
    return x * lax.rsqrt(jnp.mean(x * x, axis=-1, keepdims=True) + EPS) * g


def _layer_norm(x, g, b):
    xc = x - jnp.mean(x, axis=-1, keepdims=True)
    return xc * lax.rsqrt(jnp.mean(xc * xc, axis=-1, keepdims=True) + EPS) * g + b


def _sigmoid(x):
    return 0.5 * jnp.tanh(0.5 * x) + 0.5


def _dot(a, b):
    return jnp.dot(a, b, preferred_element_type=F32)


def _const_spec(shape):
    zeros = (0,) * len(shape)
    return pl.BlockSpec(shape, lambda *_: zeros, pipeline_mode=pl.Buffered(1))


def _layer_spec(stacked, layer):
    zeros = (0,) * (stacked.ndim - 1)
    return pl.BlockSpec((None,) + stacked.shape[1:], lambda *_: (layer,) + zeros, pipeline_mode=pl.Buffered(1))


def _cast_job(stacked, layer, steps, step_of):
    _, r, c = stacked.shape
    blk = r // steps
    assert blk * steps == r and blk % BF16_SUBLANES == 0
    return (pl.BlockSpec((None, blk, c), lambda *ids: (layer, step_of(*ids), 0)),
            pl.BlockSpec((blk, c), lambda *ids: (step_of(*ids), 0)),
            jax.ShapeDtypeStruct((r, c), BF16))


def _run_cast_jobs(src_refs, dst_refs):
    for src, dst in zip(src_refs, dst_refs):
        dst[...] = src[...].astype(BF16)


def _pool_window_lanes(shape):
    grp = lax.broadcasted_iota(jnp.int32, shape, len(shape) - 1) // POOL_GROUP_DIM
    win = jnp.full(shape, float(POOL_WINDOWS[-1]), F32)
    for g in range(len(POOL_WINDOWS) - 2, -1, -1):
        win = jnp.where(grp == g, float(POOL_WINDOWS[g]), win)
    return grp, win


def _select_by_group(grp, vals):
    out = vals[-1]
    for g in range(len(vals) - 2, -1, -1):
        out = jnp.where(grp == g, vals[g], out)
    return out


def _mix_out(x, y_groups, mog_ref, wout_ref):
    yn = [_rms(y, mog_ref[:, g * D_GROUP:(g + 1) * D_GROUP]).astype(BF16) for g, y in enumerate(y_groups)]
    return x + _dot(jnp.concatenate(yn, axis=1), wout_ref[...])


MIXER_INPUTS = 16
MIXER_OUTPUTS = 4


def _mixer_prompt_kernel(*refs, tt, n_t, n_cast):
    (x_ref, g_ref, win_ref, lng_ref, lnb_ref, wcat_ref, bst_ref, cw_ref, cb_ref, clg_ref,
     clb_ref, sw_ref, pw_ref, ps_ref, mog_ref, wout_ref) = refs[:MIXER_INPUTS]
    cast_src = refs[MIXER_INPUTS:MIXER_INPUTS + n_cast]
    outs = refs[MIXER_INPUTS + n_cast:]
    o_ref, oglu_ref, osh_ref, opool_ref = outs[:MIXER_OUTPUTS]
    cast_dst = outs[MIXER_OUTPUTS:MIXER_OUTPUTS + n_cast]
    glu_buf0, glu_buf1, yb_buf0, yb_buf1, sc_buf, p_buf, s2_buf, s4_buf, s8_buf = outs[MIXER_OUTPUTS + n_cast:]
    _run_cast_jobs(cast_src, cast_dst)
    t = pl.program_id(1)
    glu_buf = (glu_buf0, glu_buf1)
    yb_buf = (yb_buf0, yb_buf1)

    @pl.when(t == 0)
    def _():
        for s in range(LANE_SLABS):
            glu_buf[s][0:GLU_HALO, :] = jnp.zeros((GLU_HALO, LANES), F32)
        sc_buf[0:SC_HALO, :] = jnp.zeros((SC_HALO, D_GROUP), F32)
        p_buf[0:POOL_HALO, :] = jnp.zeros((POOL_HALO, D_GROUP), F32)

    _mix_tile(t, x_ref, g_ref, win_ref, lng_ref, lnb_ref, wcat_ref, bst_ref, cw_ref, cb_ref, clg_ref,
              clb_ref, sw_ref, pw_ref, ps_ref, mog_ref, wout_ref, o_ref,
              glu_buf, yb_buf, sc_buf, p_buf, s2_buf, s4_buf, s8_buf, tt=tt)

    @pl.when(t == n_t - 1)
    def _():
        oglu_ref[0] = jnp.concatenate(
            [glu_buf[s][GLU_HALO + tt - (CONF_WIDTH - 1):GLU_HALO + tt, :] for s in range(LANE_SLABS)], axis=1)
        osh_ref[0] = sc_buf[SC_HALO + tt - (SC_WIDTH - 1):SC_HALO + tt, :]
        opool_ref[0] = p_buf[POOL_HALO + tt - POOL_BUF:POOL_HALO + tt, :]

    for s in range(LANE_SLABS):
        glu_buf[s][0:GLU_HALO, :] = glu_buf[s][tt:tt + GLU_HALO, :]
    sc_buf[0:SC_HALO, :] = sc_buf[tt:tt + SC_HALO, :]
    p_buf[0:POOL_HALO, :] = p_buf[tt:tt + POOL_HALO, :]


def _mix_tile(t, x_ref, g_ref, win_ref, lng_ref, lnb_ref, wcat_ref, bst_ref, cw_ref, cb_ref, clg_ref,
              clb_ref, sw_ref, pw_ref, ps_ref, mog_ref, wout_ref, o_ref,
              glu_buf, yb_buf, sc_buf, p_buf, s2_buf, s4_buf, s8_buf, *, tt):
    x = x_ref[0]
    z = _dot(_rms(x, g_ref[...]).astype(BF16), win_ref[...])
    zcol = lambda i: z[:, i * D_GROUP:(i + 1) * D_GROUP]
    group_norm = lambda g, y: _rms(y, mog_ref[:, g * D_GROUP:(g + 1) * D_GROUP]).astype(BF16)
    yn = [None] * N_GROUPS

    vn = _layer_norm(zcol(1), lng_ref[...], lnb_ref[...])
    head = lax.broadcasted_iota(jnp.int32, (CHUNK, D_GROUP), 1) // GMLP_HEAD_DIM
    gates = []
    for c in range(tt // CHUNK):
        vc = vn[c * CHUNK:(c + 1) * CHUNK]
        stack = jnp.concatenate([jnp.where(head == hh, vc, 0.0) for hh in range(GMLP_HEADS)], axis=0)
        gates.append(_dot(wcat_ref[...], stack.astype(BF16)) + bst_ref[...])
    yn[0] = group_norm(0, zcol(0) * jnp.concatenate(gates, axis=0))

    first = GLU_HALO - (CONF_WIDTH - 1)
    span = SUBLANES * CONV_STRIDE
    glu = zcol(2) * _sigmoid(zcol(3))
    for s in range(LANE_SLABS):
        glu_buf[s][GLU_HALO:GLU_HALO + tt, :] = glu[:, s * LANES:(s + 1) * LANES]
    for c0 in range(0, tt, CONV_ROWS):
        starts = [c0 + (i // CONV_STRIDE) * span + i % CONV_STRIDE for i in range(CONV_ROWS // SUBLANES)]
        accs = [[None] * len(starts) for _ in range(LANE_SLABS)]
        for k in range(CONF_WIDTH):
            for s in range(LANE_SLABS):
                wk = jnp.broadcast_to(cw_ref[k:k + 1, s * LANES:(s + 1) * LANES], (SUBLANES, LANES))
                for i, t0 in enumerate(starts):
                    term = wk * glu_buf[s][pl.ds(first + t0 + k, SUBLANES, stride=CONV_STRIDE), :]
                    accs[s][i] = term if k == 0 else accs[s][i] + term
        conv = jnp.concatenate([jnp.concatenate(a, axis=0) for a in accs], axis=1)
        ln = _layer_norm(conv + cb_ref[...], clg_ref[...], clb_ref[...])
        yb = ln * _sigmoid(ln)
        for s in range(LANE_SLABS):
            for i, t0 in enumerate(starts):
                yb_buf[s][pl.ds(t0, SUBLANES, stride=CONV_STRIDE), :] = (
                    yb[i * SUBLANES:(i + 1) * SUBLANES, s * LANES:(s + 1) * LANES])
    yn[1] = group_norm(1, jnp.concatenate([yb_buf[s][...] for s in range(LANE_SLABS)], axis=1))

    sxc = zcol(5) * zcol(6)
    sc_buf[SC_HALO:SC_HALO + tt, :] = sxc
    conv_c = (sw_ref[0:1, :] * sc_buf[SC_HALO - 2:SC_HALO - 2 + tt, :]
              + sw_ref[1:2, :] * sc_buf[SC_HALO - 1:SC_HALO - 1 + tt, :]
              + sw_ref[2:3, :] * sxc)
    yn[2] = group_norm(2, zcol(4) * conv_c)

    pool_x = zcol(7)
    lo, end = POOL_HALO, POOL_HALO + tt
    p_buf[lo:end, :] = pool_x
    lo2, lo4, lo8 = lo - 3 * SUBLANES, lo - 2 * SUBLANES, lo - SUBLANES
    s2_buf[lo2:end, :] = p_buf[lo2:end, :] + p_buf[lo2 - 1:end - 1, :]
    s4_buf[lo4:end, :] = s2_buf[lo4:end, :] + s2_buf[lo4 - 2:end - 2, :]
    s8_buf[lo8:end, :] = s4_buf[lo8:end, :] + s4_buf[lo8 - 4:end - 4, :]
    s16 = s8_buf[lo:end, :] + s8_buf[lo - 8:end - 8, :]
    grp = _pool_window_lanes((tt, D_GROUP))[0]
    ssum = _select_by_group(grp, [s2_buf[lo:end, :], s4_buf[lo:end, :], s8_buf[lo:end, :], s16])
    head_rows = POOL_BUF + 1
    pos = t * tt + lax.broadcasted_iota(jnp.int32, (head_rows, D_GROUP), 0)
    cnt = jnp.minimum(_pool_window_lanes((head_rows, D_GROUP))[1], (pos + 1).astype(F32))
    inv_win = _select_by_group(_pool_window_lanes((tt - head_rows, D_GROUP))[0], [1.0 / w for w in POOL_WINDOWS])
    mean = jnp.concatenate([ssum[:head_rows] / cnt, ssum[head_rows:] * inv_win], axis=0)
    pooled = mean - pool_x
    yn[3] = group_norm(3, _dot(pooled.astype(BF16), pw_ref[...]) * ps_ref[...])

    o_ref[0] = x + _dot(jnp.concatenate(yn, axis=1), wout_ref[...])


def _mixer_prompt(x, lw, *, tt=1024, cast=()):
    bsz, seq, _ = x.shape
    n_t = seq // tt
    step_of = lambda b, t: b * n_t + t
    row = lambda a: a.reshape(1, -1)
    small = [row(lw['norm_mix']), lw['w_in'], row(lw['gmlp_ln_g']), row(lw['gmlp_ln_b']), lw['gmlp_wcat'],
             lw['gmlp_bias_tile'], lw['conf_dw'], row(lw['conf_dw_b']), row(lw['conf_ln_g']), row(lw['conf_ln_b']),
             lw['sc_dw'], lw['pool_w_bd'], row(lw['pool_scale']), row(lw['mix_out_g']), lw['w_out']]
    assert 1 + len(small) == MIXER_INPUTS
    jobs = [_cast_job(w, layer, bsz * n_t, step_of) for w, layer in cast]
    state_spec = lambda rows: pl.BlockSpec((1, rows, D_GROUP), lambda b, t: (b, 0, 0))
    outs = pl.pallas_call(
        functools.partial(_mixer_prompt_kernel, tt=tt, n_t=n_t, n_cast=len(jobs)),
        grid=(bsz, n_t),
        in_specs=[pl.BlockSpec((1, tt, D_MODEL), lambda b, t: (b, t, 0))] + [_const_spec(a.shape) for a in small]
        + [j[0] for j in jobs],
        out_specs=[pl.BlockSpec((1, tt, D_MODEL), lambda b, t: (b, t, 0)),
                   state_spec(CONF_WIDTH - 1), state_spec(SC_WIDTH - 1), state_spec(POOL_BUF)] + [j[1] for j in jobs],
        out_shape=[jax.ShapeDtypeStruct(x.shape, F32),
                   jax.ShapeDtypeStruct((bsz, CONF_WIDTH - 1, D_GROUP), F32),
                   jax.ShapeDtypeStruct((bsz, SC_WIDTH - 1, D_GROUP), F32),
                   jax.ShapeDtypeStruct((bsz, POOL_BUF, D_GROUP), F32)] + [j[2] for j in jobs],
        scratch_shapes=[pltpu.VMEM((GLU_HALO + tt, LANES), F32)] * LANE_SLABS + [pltpu.VMEM((tt, LANES), F32)] * LANE_SLABS
        + [pltpu.VMEM((SC_HALO + tt, D_GROUP), F32)]
        + [pltpu.VMEM((POOL_HALO + tt, D_GROUP), F32)] * 4,
        compiler_params=pltpu.CompilerParams(dimension_semantics=("arbitrary", "arbitrary"),
                                             vmem_limit_bytes=VMEM_LIMIT),
        name="mixer_prompt",
    )(x, *small, *[w for w, _ in cast])
    return (*outs[:MIXER_OUTPUTS], list(outs[MIXER_OUTPUTS:]))


def _mixer_sample_kernel(x_ref, stg_ref, sts_ref, stp_ref, g_ref, win_ref, lng_ref, lnb_ref, w00_ref, b0_ref,
                         cw_ref, cb_ref, clg_ref, clb_ref, sw_ref, pw_ref, ps_ref, mog_ref, wout_ref, gx_ref, wq_ref,
                         o_ref, oq_ref, ovn_ref, oglu_ref, osh_ref, opool_ref):
    x = x_ref[...]
    h = _rms(x, g_ref[...]).astype(BF16)
    z = _dot(h, win_ref[...])
    u, v, glu_a, glu_g, sc_b, sc_c, sc_x, pool_x = [z[:, i * D_GROUP:(i + 1) * D_GROUP]
                                                    for i in range(D_IN_PROJ // D_GROUP)]
    hist = lambda ref, k: ref[:, k, :]

    def push(new_ref, old_ref, row):
        n_hist = old_ref.shape[1]
        new_ref[:, 0:n_hist - 1, :] = old_ref[:, 1:n_hist, :]
        new_ref[:, n_hist - 1, :] = row

    vn = _layer_norm(v, lng_ref[...], lnb_ref[...])
    ovn_ref[...] = vn
    y_a = u * (w00_ref[...] * vn + b0_ref[...])

    glu = glu_a * _sigmoid(glu_g)
    n_hist = CONF_WIDTH - 1
    acc = cw_ref[n_hist:n_hist + 1, :] * glu
    for k in range(n_hist):
        acc = acc + cw_ref[k:k + 1, :] * hist(stg_ref, k)
    ln = _layer_norm(acc + cb_ref[...], clg_ref[...], clb_ref[...])
    y_b = ln * _sigmoid(ln)
    push(oglu_ref, stg_ref, glu)

    sxc = sc_c * sc_x
    y_c = sc_b * (sw_ref[0:1, :] * hist(sts_ref, 0) + sw_ref[1:2, :] * hist(sts_ref, 1) + sw_ref[2:3, :] * sxc)
    push(osh_ref, sts_ref, sxc)

    run = pool_x
    sums = []
    back = 0
    for w in POOL_WINDOWS:
        while back < w - 1:
            run = run + hist(stp_ref, POOL_BUF - 1 - back)
            back += 1
        sums.append(run)
    grp, win = _pool_window_lanes(pool_x.shape)
    cnt = jnp.minimum(win, float(PAST_LEN + 1))
    pooled = _select_by_group(grp, sums) / cnt - pool_x
    y_d = _dot(pooled.astype(BF16), pw_ref[...]) * ps_ref[...]
    push(opool_ref, stp_ref, pool_x)

    x_mixed = _mix_out(x, [y_a, y_b, y_c, y_d], mog_ref, wout_ref)
    o_ref[...] = x_mixed
    oq_ref[...] = _dot(_rms(x_mixed, gx_ref[...]).astype(BF16), wq_ref[...]) * (XATTN_HEAD_DIM ** -0.5)


def _mixer_sample(xs, st_glu, st_sh, st_pool, layer, lw):
    n = xs.shape[0]
    row = lambda a: a.reshape(1, -1)
    states = [st_glu, st_sh, st_pool]
    consts = [row(lw['norm_mix']), lw['w_in'], row(lw['gmlp_ln_g']), row(lw['gmlp_ln_b']), row(lw['gmlp_w00']),
              row(lw['gmlp_b0']), lw['conf_dw'], row(lw['conf_dw_b']), row(lw['conf_ln_g']), row(lw['conf_ln_b']),
              lw['sc_dw'], lw['pool_w_bd'], row(lw['pool_scale']), row(lw['mix_out_g']), lw['w_out'],
              row(lw['norm_xattn'])]
    out_shapes = [(n, D_MODEL), (n, D_MODEL), (n, D_GROUP)] + [s.shape[1:] for s in states]
    return pl.pallas_call(
        _mixer_sample_kernel,
        grid=(1,),
        in_specs=[_const_spec(xs.shape)] + [_layer_spec(s, layer) for s in states]
        + [_const_spec(a.shape) for a in consts] + [_layer_spec(lw['w_xq_layers'], layer)],
        out_specs=[pl.BlockSpec(s, lambda i, nd=len(s): (0,) * nd) for s in out_shapes],
        out_shape=[jax.ShapeDtypeStruct(s, F32) for s in out_shapes],
        compiler_params=pltpu.CompilerParams(dimension_semantics=("arbitrary",), vmem_limit_bytes=VMEM_LIMIT),
        name="mixer_sample",
    )(xs, *states, *consts, lw['w_xq_layers'])


MEM_KV_INPUTS = 6
MEM_KV_OUTPUTS = 6


def _mem_kv_kernel(*refs, n_cast):
    m_ref, g_ref, wk_ref, wv_ref, wq_ref, wo_ref = refs[:MEM_KV_INPUTS]
    cast_src = refs[MEM_KV_INPUTS:MEM_KV_INPUTS + n_cast]
    outs = refs[MEM_KV_INPUTS + n_cast:]
    ok_ref, ov_ref, oqk_ref, ovo_ref, wq_bf16, wo_bf16 = outs[:MEM_KV_OUTPUTS]
    cast_dst = outs[MEM_KV_OUTPUTS:MEM_KV_OUTPUTS + n_cast]
    wk_bf16, wv_bf16 = outs[MEM_KV_OUTPUTS + n_cast:]
    _run_cast_jobs(cast_src, cast_dst)

    @pl.when(pl.program_id(1) == 0)
    def _():
        for src, dst in ((wk_ref, wk_bf16), (wv_ref, wv_bf16), (wq_ref, wq_bf16), (wo_ref, wo_bf16)):
            dst[...] = src[...].astype(BF16)

    m = _rms(m_ref[0], g_ref[...]).astype(BF16)
    k = _dot(m, wk_bf16[...])
    v = _dot(m, wv_bf16[...])
    contract_last = (((1,), (1,)), ((), ()))
    for h in range(XATTN_HEADS):
        lo, hi = h * XATTN_HEAD_DIM, (h + 1) * XATTN_HEAD_DIM
        ok_ref[0, :, h, :] = k[:, lo:hi]
        ov_ref[0, :, h, :] = v[:, lo:hi]
        qk = lax.dot_general(wq_bf16[:, lo:hi], k[:, lo:hi].astype(BF16), contract_last,
                             preferred_element_type=F32)
        oqk_ref[0, :, h * MEM_LEN:(h + 1) * MEM_LEN] = (qk * (XATTN_HEAD_DIM ** -0.5)).astype(BF16)
        ovo_ref[0, h * MEM_LEN:(h + 1) * MEM_LEN, :] = _dot(v[:, lo:hi].astype(BF16), wo_bf16[lo:hi, :]).astype(BF16)


def _mem_kv(mem, norm_mem, w_xk, w_xv, w_xq, w_xo, *, cast=()):
    depth = w_xk.shape[0]
    bsz = mem.shape[0]
    g = norm_mem.reshape(depth, 1, D_MODEL)
    per_layer = lambda *shape, **kw: pl.BlockSpec((None,) + shape, lambda l, b: (l,) + (0,) * len(shape), **kw)
    out_blk = lambda *shape: pl.BlockSpec((None, 1) + shape, lambda l, b: (l, b) + (0,) * len(shape))
    weights = [w_xk, w_xv, w_xq, w_xo]
    jobs = [_cast_job(w, layer, depth * bsz, lambda l, b: l * bsz + b) for w, layer in cast]
    outs = pl.pallas_call(
        functools.partial(_mem_kv_kernel, n_cast=len(jobs)),
        grid=(depth, bsz),
        in_specs=[pl.BlockSpec((1, MEM_LEN, D_MODEL), lambda l, b: (b, 0, 0)), per_layer(1, D_MODEL)]
        + [per_layer(D_MODEL, D_MODEL, pipeline_mode=pl.Buffered(1))] * len(weights) + [j[0] for j in jobs],
        out_specs=[out_blk(MEM_LEN, XATTN_HEADS, XATTN_HEAD_DIM), out_blk(MEM_LEN, XATTN_HEADS, XATTN_HEAD_DIM),
                   out_blk(D_MODEL, XATTN_HEADS * MEM_LEN), out_blk(XATTN_HEADS * MEM_LEN, D_MODEL),
                   per_layer(D_MODEL, D_MODEL), per_layer(D_MODEL, D_MODEL)] + [j[1] for j in jobs],
        out_shape=[jax.ShapeDtypeStruct((depth, bsz, MEM_LEN, XATTN_HEADS, XATTN_HEAD_DIM), F32)] * 2
        + [jax.ShapeDtypeStruct((depth, bsz, D_MODEL, XATTN_HEADS * MEM_LEN), BF16),
           jax.ShapeDtypeStruct((depth, bsz, XATTN_HEADS * MEM_LEN, D_MODEL), BF16)]
        + [jax.ShapeDtypeStruct((depth, D_MODEL, D_MODEL), BF16)] * 2 + [j[2] for j in jobs],
        scratch_shapes=[pltpu.VMEM((D_MODEL, D_MODEL), BF16)] * 2,
        compiler_params=pltpu.CompilerParams(dimension_semantics=("arbitrary", "arbitrary"),
                                             vmem_limit_bytes=VMEM_LIMIT),
        name="mem_kv",
    )(mem, g, *weights, *[w for w, _ in cast])
    return (*outs[:MEM_KV_OUTPUTS], list(outs[MEM_KV_OUTPUTS:]))


def _xattn_prompt_kernel(x_ref, g_ref, qk_ref, vo_ref, o_ref):
    x = x_ref[0]
    h = _rms(x, g_ref[...]).astype(BF16)
    s = _dot(h, qk_ref[0])
    probs = []
    for hh in range(XATTN_HEADS):
        sh = s[:, hh * MEM_LEN:(hh + 1) * MEM_LEN]
        e = jnp.exp(sh - jnp.max(sh, axis=-1, keepdims=True))
        probs.append((e * (1.0 / jnp.sum(e, axis=-1, keepdims=True))).astype(BF16))
    o_ref[0] = x + _dot(jnp.concatenate(probs, axis=1), vo_ref[0])


def _xattn_prompt(x, qk, vo, layer, lw, *, tt=1024):
    bsz, seq, _ = x.shape
    g = lw['norm_xattn'].reshape(1, -1)
    folded = lambda a: pl.BlockSpec((None, 1) + a.shape[2:], lambda b, t: (layer, b, 0, 0))
    return pl.pallas_call(
        _xattn_prompt_kernel,
        grid=(bsz, seq // tt),
        in_specs=[pl.BlockSpec((1, tt, D_MODEL), lambda b, t: (b, t, 0)), _const_spec(g.shape),
                  folded(qk), folded(vo)],
        out_specs=pl.BlockSpec((1, tt, D_MODEL), lambda b, t: (b, t, 0)),
        out_shape=jax.ShapeDtypeStruct(x.shape, F32),
        compiler_params=pltpu.CompilerParams(dimension_semantics=("arbitrary", "arbitrary"),
                                             vmem_limit_bytes=VMEM_LIMIT),
        name="xattn_prompt",
    )(x, g, qk, vo)


def _split_head_dim(a):
    lead = a.shape[:-2]
    a = a.reshape(*lead, XATTN_HEADS, 2, XATTN_HEAD_DIM // 2)
    return jnp.swapaxes(a, -3, -2).reshape(*lead, 2 * XATTN_HEADS, XATTN_HEAD_DIM // 2)


def _merge_head_dim(a):
    lead = a.shape[:-2]
    a = a.reshape(*lead, 2, XATTN_HEADS, XATTN_HEAD_DIM // 2)
    return jnp.swapaxes(a, -3, -2).reshape(*lead, XATTN_HEADS, XATTN_HEAD_DIM)


def _attend_rows(q_ref, k_ref, v_ref, o_ref, rows):
    for r in rows:
        part = jnp.sum(k_ref[r] * q_ref[r][None], axis=-1, keepdims=True)
        s = part + pltpu.roll(part, XATTN_HEADS, axis=1)
        e = jnp.exp(s - jnp.max(s, axis=0, keepdims=True))
        o_ref[r] = jnp.sum(e * v_ref[r], axis=0) * (1.0 / jnp.sum(e, axis=0))


def _ffn_rows(x, g_ref, w1_ref, w2_ref, gf_ref, *, final_norm, between_chunks=None):
    h = _rms(x, g_ref[...]).astype(BF16)
    y = x
    for i, c in enumerate(range(0, D_FF, FF_CHUNK)):
        if between_chunks is not None:
            between_chunks(i)
        a = jnp.maximum(_dot(h, w1_ref[:, c:c + FF_CHUNK]), 0.0)
        y = y + _dot((a * a).astype(BF16), w2_ref[c:c + FF_CHUNK, :])
    return _rms(y, gf_ref[...]) if final_norm else y


def _ffn_kernel(x_ref, g_ref, w1_ref, w2_ref, gf_ref, o_ref, *, final_norm, between_chunks=None):
    o_ref[...] = _ffn_rows(x_ref[...], g_ref, w1_ref, w2_ref, gf_ref, final_norm=final_norm,
                           between_chunks=between_chunks)


def _oproj_ffn_kernel(x_ref, a_ref, wo_ref, g_ref, w1_ref, w2_ref, gf_ref, o_ref, *, final_norm):
    x = x_ref[...] + _dot(a_ref[...].astype(BF16), wo_ref[...])
    o_ref[...] = _ffn_rows(x, g_ref, w1_ref, w2_ref, gf_ref, final_norm=final_norm)


def _oproj_ffn(x, attn, layer, lw, norm_final, *, final_norm):
    args = [x, attn, lw['w_xo_layers'], lw['norm_ffn'].reshape(1, -1), lw['w_ff1'], lw['w_ff2'],
            norm_final.reshape(1, -1)]
    return pl.pallas_call(
        functools.partial(_oproj_ffn_kernel, final_norm=final_norm),
        grid=(1,),
        in_specs=[_layer_spec(a, layer) if a is lw['w_xo_layers'] else _const_spec(a.shape) for a in args],
        out_specs=pl.BlockSpec(x.shape, lambda i: (0, 0)),
        out_shape=jax.ShapeDtypeStruct(x.shape, F32),
        compiler_params=pltpu.CompilerParams(dimension_semantics=("arbitrary",), vmem_limit_bytes=VMEM_LIMIT),
        name="oproj_ffn",
    )(*args)


FFN_ATTEND_INPUTS = 8
FFN_ATTEND_OUTPUTS = 2


def _ffn_attend_kernel(*refs, final_norm, rows, n_cast):
    x_ref, g_ref, w1_ref, w2_ref, gf_ref, q_ref, k_ref, v_ref = refs[:FFN_ATTEND_INPUTS]
    cast_src = refs[FFN_ATTEND_INPUTS:FFN_ATTEND_INPUTS + n_cast]
    o_ref, oa_ref = refs[FFN_ATTEND_INPUTS + n_cast:FFN_ATTEND_INPUTS + n_cast + FFN_ATTEND_OUTPUTS]
    cast_dst = refs[FFN_ATTEND_INPUTS + n_cast + FFN_ATTEND_OUTPUTS:]
    _run_cast_jobs(cast_src, cast_dst)
    n_chunks = D_FF // FF_CHUNK
    assert rows % n_chunks == 0
    per_chunk = rows // n_chunks
    attend = lambda i: _attend_rows(q_ref, k_ref, v_ref, oa_ref, range(i * per_chunk, (i + 1) * per_chunk))
    _ffn_kernel(x_ref, g_ref, w1_ref, w2_ref, gf_ref, o_ref, final_norm=final_norm, between_chunks=attend)


def _ffn_attend(x2d, lw, norm_final, attend, *, final_norm, tm=512, cast=()):
    n = x2d.shape[0]
    g = lw['norm_ffn'].reshape(1, -1)
    gf = norm_final.reshape(1, -1)
    in_specs = [pl.BlockSpec((tm, D_MODEL), lambda i: (i, 0)), _const_spec(g.shape),
                _const_spec(lw['w_ff1'].shape), _const_spec(lw['w_ff2'].shape), _const_spec(gf.shape)]
    out_spec = pl.BlockSpec((tm, D_MODEL), lambda i: (i, 0))
    out_shape = jax.ShapeDtypeStruct(x2d.shape, F32)
    params = pltpu.CompilerParams(dimension_semantics=("arbitrary",), vmem_limit_bytes=VMEM_LIMIT)
    q, cache_k, cache_v, layer = attend
    rows = q.shape[0] // (n // tm)
    assert rows * (n // tm) == q.shape[0]
    q_spec = pl.BlockSpec((rows,) + q.shape[1:], lambda i: (i, 0, 0))
    kv_spec = pl.BlockSpec((None, rows) + cache_k.shape[2:], lambda i: (layer, i, 0, 0, 0))
    jobs = [_cast_job(w, wl, n // tm, lambda i: i) for w, wl in cast]
    outs = pl.pallas_call(
        functools.partial(_ffn_attend_kernel, final_norm=final_norm, rows=rows, n_cast=len(jobs)), grid=(n // tm,),
        in_specs=in_specs + [q_spec, kv_spec, kv_spec] + [j[0] for j in jobs],
        out_specs=[out_spec, q_spec] + [j[1] for j in jobs],
        out_shape=[out_shape, jax.ShapeDtypeStruct(q.shape, F32)] + [j[2] for j in jobs],
        compiler_params=params, name="ffn_attend",
    )(x2d, g, lw['w_ff1'], lw['w_ff2'], gf, q, cache_k, cache_v, *[w for w, _ in cast])
    return outs[0], outs[1], list(outs[FFN_ATTEND_OUTPUTS:])


PROJ_WEIGHTS = ('w_in', 'w_out')


def _layer_weights(l, p, proj_bf16):
    tril = jnp.tril(jnp.ones((CHUNK, CHUNK), dtype=bool))
    ws = jnp.where(tril[None], p['gmlp_ws'][l], 0.0)
    pool_bd = jax.scipy.linalg.block_diag(*[p['pool_w'][l, g] for g in range(len(POOL_WINDOWS))])
    lw = {k: p[k][l] for k in ('norm_mix', 'gmlp_ln_g', 'gmlp_ln_b', 'conf_dw', 'conf_dw_b', 'conf_ln_g',
                               'conf_ln_b', 'sc_dw', 'pool_scale', 'mix_out_g', 'norm_xattn', 'norm_ffn')}
    lw.update(proj_bf16)
    lw.update(
        gmlp_wcat=jnp.concatenate([ws[h] for h in range(GMLP_HEADS)], axis=1).astype(BF16),
        gmlp_bias_tile=jnp.repeat(p['gmlp_bs'][l].T, GMLP_HEAD_DIM, axis=1),
        gmlp_w00=jnp.repeat(ws[:, 0, 0], GMLP_HEAD_DIM), gmlp_b0=jnp.repeat(p['gmlp_bs'][l][:, 0], GMLP_HEAD_DIM),
        pool_w_bd=pool_bd.astype(BF16))
    return lw


def kernel(x_prompt, x_sample, mem_prompt, cache_mem_k, cache_mem_v, state_conv_glu, state_conv_short, state_pool, norm_mix, w_in, gmlp_ln_g, gmlp_ln_b, gmlp_ws, gmlp_bs, conf_dw, conf_dw_b, conf_ln_g, conf_ln_b, sc_dw, pool_w, pool_scale, mix_out_g, w_out, norm_xattn, norm_mem, w_xq, w_xk, w_xv, w_xo, norm_ffn, w_ff1, w_ff2, norm_final):
    params = dict(norm_mix=norm_mix, w_in=w_in, gmlp_ln_g=gmlp_ln_g, gmlp_ln_b=gmlp_ln_b, gmlp_ws=gmlp_ws,
                  gmlp_bs=gmlp_bs, conf_dw=conf_dw, conf_dw_b=conf_dw_b, conf_ln_g=conf_ln_g, conf_ln_b=conf_ln_b,
                  sc_dw=sc_dw, pool_w=pool_w, pool_scale=pool_scale, mix_out_g=mix_out_g, w_out=w_out,
                  norm_xattn=norm_xattn, norm_mem=norm_mem, w_xq=w_xq, w_xk=w_xk, w_xv=w_xv, w_xo=w_xo,
                  norm_ffn=norm_ffn, w_ff1=w_ff1, w_ff2=w_ff2)
    depth = w_in.shape[0]
    bsz, seq, _ = x_prompt.shape
    n_s = x_sample.shape[0]
    xp = x_prompt
    xs = x_sample.reshape(n_s, D_MODEL)
    outs = {k: [] for k in ('glu_p', 'glu_s', 'sh_p', 'sh_s', 'pl_p', 'pl_s', 'v_s')}
    mem_k, mem_v, mem_qk, mem_vo, wq_bf16, wo_bf16, proj = _mem_kv(
        mem_prompt, norm_mem, w_xk, w_xv, w_xq, w_xo, cast=[(params[k], 0) for k in PROJ_WEIGHTS])
    cache_k_split = _split_head_dim(cache_mem_k)
    cache_v_split = _split_head_dim(cache_mem_v)
    for l in range(depth):
        lw = _layer_weights(l, params, dict(zip(PROJ_WEIGHTS, proj), w_xq_layers=wq_bf16, w_xo_layers=wo_bf16))
        last = l == depth - 1
        xs, q_s, vn_s, glu_s, sh_s, pool_s = _mixer_sample(xs, state_conv_glu, state_conv_short, state_pool, l, lw)
        q_s = _split_head_dim(q_s.reshape(n_s, XATTN_HEADS, XATTN_HEAD_DIM))
        xp, glu_p, sh_p, pool_p, (lw['w_ff1'], lw['w_ff2']) = _mixer_prompt(
            xp, lw, cast=[(params['w_ff1'], l), (params['w_ff2'], l)])
        xp = _xattn_prompt(xp, mem_qk, mem_vo, l, lw)
        outs['glu_p'].append(glu_p); outs['sh_p'].append(sh_p); outs['pl_p'].append(pool_p)
        xp, o_s, proj = _ffn_attend(xp.reshape(bsz * seq, D_MODEL), lw, norm_final,
                                    (q_s, cache_k_split, cache_v_split, l), final_norm=last,
                                    cast=[] if last else [(params[k], l + 1) for k in PROJ_WEIGHTS])
        xp = xp.reshape(bsz, seq, D_MODEL)
        xs = _oproj_ffn(xs, _merge_head_dim(o_s).reshape(n_s, D_MODEL), l, lw, norm_final, final_norm=last)
        outs['glu_s'].append(glu_s); outs['sh_s'].append(sh_s); outs['pl_s'].append(pool_s)
        outs['v_s'].append(vn_s.reshape(n_s, 1, D_GROUP))
    st = lambda k: jnp.stack(outs[k], axis=0)
    return (xp, xs.reshape(n_s, 1, D_MODEL), mem_k, mem_v, st('glu_p'), st('glu_s'), st('sh_p'), st('sh_s'),
            st('pl_p'), st('pl_s'), st('v_s'))
```

```python
import functools

import jax
import jax.numpy as jnp
from jax import lax
from jax.experimental import pallas as pl
from jax.experimental.pallas import tpu as pltpu

F32 = jnp.float32
BF16 = jnp.bfloat16

D_MODEL = 1024
D_GROUP = 256
N_GROUPS = 4
D_IN_PROJ = 8 * D_GROUP
GMLP_HEADS = 4
GMLP_HEAD_DIM = D_GROUP // GMLP_HEADS
CHUNK = 128
CONF_WIDTH = 31
SC_WIDTH = 3
POOL_WINDOWS = (2, 4, 8, 16)
POOL_GROUP_DIM = D_GROUP // len(POOL_WINDOWS)
POOL_BUF = max(POOL_WINDOWS) - 1
MEM_LEN = 256
XATTN_HEADS = 4
XATTN_HEAD_DIM = D_MODEL // XATTN_HEADS
D_FF = 4 * D_MODEL
PAST_LEN = 16384
EPS = 1e-6

GLU_HALO = 32
SC_HALO = 8
POOL_HALO = 32
LANES = 128
SUBLANES = 8
BF16_SUBLANES = 16
LANE_SLABS = D_GROUP // LANES
CONV_STRIDE = 4
CONV_ROWS = 128
FF_CHUNK = 4096
VMEM_LIMIT = 56 * 1024 * 1024


def _rms(x, g):
    return x * lax.rsqrt(jnp.mean(x * x, axis=-1, keepdims=True) + EPS) * g


def _layer_norm(x, g, b):
    xc = x - jnp.mean(x, axis=-1, keepdims=True)
    return xc * lax.rsqrt(jnp.mean(xc * xc, axis=-1, keepdims=True) + EPS) * g + b


def _sigmoid(x):
    return 0.5 * jnp.tanh(0.5 * x) + 0.5


def _dot(a, b):
    return jnp.dot(a, b, preferred_element_type=F32)


def _const_spec(shape):
    zeros = (0,) * len(shape)
    return pl.BlockSpec(shape, lambda *_: zeros, pipeline_mode=pl.Buffered(1))


def _layer_spec(stacked, layer):
    zeros = (0,) * (stacked.ndim - 1)
    return pl.BlockSpec((None,) + stacked.shape[1:], lambda *_: (layer,) + zeros, pipeline_mode=pl.Buffered(1))


def _cast_job(stacked, layer, steps, step_of):
    _, r, c = stacked.shape
    blk = r // steps
    assert blk * steps == r and blk % BF16_SUBLANES == 0
    return (pl.BlockSpec((None, blk, c), lambda *ids: (layer, step_of(*ids), 0)),
            pl.BlockSpec((blk, c), lambda *ids: (step_of(*ids), 0)),
            jax.ShapeDtypeStruct((r, c), BF16))


def _run_cast_jobs(src_refs, dst_refs):
    for src, dst in zip(src_refs, dst_refs):
        dst[...] = src[...].astype(BF16)


def _pool_window_lanes(shape):
    grp = lax.broadcasted_iota(jnp.int32, shape, len(shape) - 1) // POOL_GROUP_DIM
    win = jnp.full(shape, float(POOL_WINDOWS[-1]), F32)
    for g in range(len(POOL_WINDOWS) - 2, -1, -1):
        win = jnp.where(grp == g, float(POOL_WINDOWS[g]), win)
    return grp, win


def _select_by_group(grp, vals):
    out = vals[-1]
    for g in range(len(vals) - 2, -1, -1):
        out = jnp.where(grp == g, vals[g], out)
    return out


def _mix_out(x, y_groups, mog_ref, wout_ref):
    yn = [_rms(y, mog_ref[:, g * D_GROUP:(g + 1) * D_GROUP]).astype(BF16) for g, y in enumerate(y_groups)]
    return x + _dot(jnp.concatenate(yn, axis=1), wout_ref[...])


MIXER_INPUTS = 16
MIXER_OUTPUTS = 4


def _mixer_prompt_kernel(*refs, tt, n_t, n_cast):
    (x_ref, g_ref, win_ref, lng_ref, lnb_ref, wcat_ref, bst_ref, cw_ref, cb_ref, clg_ref,
     clb_ref, sw_ref, pw_ref, ps_ref, mog_ref, wout_ref) = refs[:MIXER_INPUTS]
    cast_src = refs[MIXER_INPUTS:MIXER_INPUTS + n_cast]
    outs = refs[MIXER_INPUTS + n_cast:]
    o_ref, oglu_ref, osh_ref, opool_ref = outs[:MIXER_OUTPUTS]
    cast_dst = outs[MIXER_OUTPUTS:MIXER_OUTPUTS + n_cast]
    glu_buf0, glu_buf1, yb_buf0, yb_buf1, sc_buf, p_buf, s2_buf, s4_buf, s8_buf = outs[MIXER_OUTPUTS + n_cast:]
    _run_cast_jobs(cast_src, cast_dst)
    t = pl.program_id(1)
    glu_buf = (glu_buf0, glu_buf1)
    yb_buf = (yb_buf0, yb_buf1)

    @pl.when(t == 0)
    def _():
        for s in range(LANE_SLABS):
            glu_buf[s][0:GLU_HALO, :] = jnp.zeros((GLU_HALO, LANES), F32)
        sc_buf[0:SC_HALO, :] = jnp.zeros((SC_HALO, D_GROUP), F32)
        p_buf[0:POOL_HALO, :] = jnp.zeros((POOL_HALO, D_GROUP), F32)

    _mix_tile(t, x_ref, g_ref, win_ref, lng_ref, lnb_ref, wcat_ref, bst_ref, cw_ref, cb_ref, clg_ref,
              clb_ref, sw_ref, pw_ref, ps_ref, mog_ref, wout_ref, o_ref,
              glu_buf, yb_buf, sc_buf, p_buf, s2_buf, s4_buf, s8_buf, tt=tt)

    @pl.when(t == n_t - 1)
    def _():
        oglu_ref[0] = jnp.concatenate(
            [glu_buf[s][GLU_HALO + tt - (CONF_WIDTH - 1):GLU_HALO + tt, :] for s in range(LANE_SLABS)], axis=1)
        osh_ref[0] = sc_buf[SC_HALO + tt - (SC_WIDTH - 1):SC_HALO + tt, :]
        opool_ref[0] = p_buf[POOL_HALO + tt - POOL_BUF:POOL_HALO + tt, :]

    for s in range(LANE_SLABS):
        glu_buf[s][0:GLU_HALO, :] = glu_buf[s][tt:tt + GLU_HALO, :]
    sc_buf[0:SC_HALO, :] = sc_buf[tt:tt + SC_HALO, :]
    p_buf[0:POOL_HALO, :] = p_buf[tt:tt + POOL_HALO, :]


def _mix_tile(t, x_ref, g_ref, win_ref, lng_ref, lnb_ref, wcat_ref, bst_ref, cw_ref, cb_ref, clg_ref,
              clb_ref, sw_ref, pw_ref, ps_ref, mog_ref, wout_ref, o_ref,
              glu_buf, yb_buf, sc_buf, p_buf, s2_buf, s4_buf, s8_buf, *, tt):
    x = x_ref[0]
    z = _dot(_rms(x, g_ref[...]).astype(BF16), win_ref[...])
    zcol = lambda i: z[:, i * D_GROUP:(i + 1) * D_GROUP]
    group_norm = lambda g, y: _rms(y, mog_ref[:, g * D_GROUP:(g + 1) * D_GROUP]).astype(BF16)
    yn = [None] * N_GROUPS

    vn = _layer_norm(zcol(1), lng_ref[...], lnb_ref[...])
    head = lax.broadcasted_iota(jnp.int32, (CHUNK, D_GROUP), 1) // GMLP_HEAD_DIM
    gates = []
    for c in range(tt // CHUNK):
        vc = vn[c * CHUNK:(c + 1) * CHUNK]
        stack = jnp.concatenate([jnp.where(head == hh, vc, 0.0) for hh in range(GMLP_HEADS)], axis=0)
        gates.append(_dot(wcat_ref[...], stack.astype(BF16)) + bst_ref[...])
    yn[0] = group_norm(0, zcol(0) * jnp.concatenate(gates, axis=0))

    first = GLU_HALO - (CONF_WIDTH - 1)
    span = SUBLANES * CONV_STRIDE
    glu = zcol(2) * _sigmoid(zcol(3))
    for s in range(LANE_SLABS):
        glu_buf[s][GLU_HALO:GLU_HALO + tt, :] = glu[:, s * LANES:(s + 1) * LANES]
    for c0 in range(0, tt, CONV_ROWS):
        starts = [c0 + (i // CONV_STRIDE) * span + i % CONV_STRIDE for i in range(CONV_ROWS // SUBLANES)]
        accs = [[None] * len(starts) for _ in range(LANE_SLABS)]
        for k in range(CONF_WIDTH):
            for s in range(LANE_SLABS):
                wk = jnp.broadcast_to(cw_ref[k:k + 1, s * LANES:(s + 1) * LANES], (SUBLANES, LANES))
                for i, t0 in enumerate(starts):
                    term = wk * glu_buf[s][pl.ds(first + t0 + k, SUBLANES, stride=CONV_STRIDE), :]
                    accs[s][i] = term if k == 0 else accs[s][i] + term
        conv = jnp.concatenate([jnp.concatenate(a, axis=0) for a in accs], axis=1)
        ln = _layer_norm(conv + cb_ref[...], clg_ref[...], clb_ref[...])
        yb = ln * _sigmoid(ln)
        for s in range(LANE_SLABS):
            for i, t0 in enumerate(starts):
                yb_buf[s][pl.ds(t0, SUBLANES, stride=CONV_STRIDE), :] = (
                    yb[i * SUBLANES:(i + 1) * SUBLANES, s * LANES:(s + 1) * LANES])
    yn[1] = group_norm(1, jnp.concatenate([yb_buf[s][...] for s in range(LANE_SLABS)], axis=1))

    sxc = zcol(5) * zcol(6)
    sc_buf[SC_HALO:SC_HALO + tt, :] = sxc
    conv_c = (sw_ref[0:1, :] * sc_buf[SC_HALO - 2:SC_HALO - 2 + tt, :]
              + sw_ref[1:2, :] * sc_buf[SC_HALO - 1:SC_HALO - 1 + tt, :]
              + sw_ref[2:3, :] * sxc)
    yn[2] = group_norm(2, zcol(4) * conv_c)

    pool_x = zcol(7)
    lo, end = POOL_HALO, POOL_HALO + tt
    p_buf[lo:end, :] = pool_x
    lo2, lo4, lo8 = lo - 3 * SUBLANES, lo - 2 * SUBLANES, lo - SUBLANES
    s2_buf[lo2:end, :] = p_buf[lo2:end, :] + p_buf[lo2 - 1:end - 1, :]
    s4_buf[lo4:end, :] = s2_buf[lo4:end, :] + s2_buf[lo4 - 2:end - 2, :]
    s8_buf[lo8:end, :] = s4_buf[lo8:end, :] + s4_buf[lo8 - 4:end - 4, :]
    s16 = s8_buf[lo:end, :] + s8_buf[lo - 8:end - 8, :]
    grp = _pool_window_lanes((tt, D_GROUP))[0]
    ssum = _select_by_group(grp, [s2_buf[lo:end, :], s4_buf[lo:end, :], s8_buf[lo:end, :], s16])
    head_rows = POOL_BUF + 1
    pos = t * tt + lax.broadcasted_iota(jnp.int32, (head_rows, D_GROUP), 0)
    cnt = jnp.minimum(_pool_window_lanes((head_rows, D_GROUP))[1], (pos + 1).astype(F32))
    inv_win = _select_by_group(_pool_window_lanes((tt - head_rows, D_GROUP))[0], [1.0 / w for w in POOL_WINDOWS])
    mean = jnp.concatenate([ssum[:head_rows] / cnt, ssum[head_rows:] * inv_win], axis=0)
    pooled = mean - pool_x
    yn[3] = group_norm(3, _dot(pooled.astype(BF16), pw_ref[...]) * ps_ref[...])

    o_ref[0] = x + _dot(jnp.concatenate(yn, axis=1), wout_ref[...])


def _mixer_prompt(x, lw, *, tt=1024, cast=()):
    bsz, seq, _ = x.shape
    n_t = seq // tt
    step_of = lambda b, t: b * n_t + t
    row = lambda a: a.reshape(1, -1)
    small = [row(lw['norm_mix']), lw['w_in'], row(lw['gmlp_ln_g']), row(lw['gmlp_ln_b']), lw['gmlp_wcat'],
             lw['gmlp_bias_tile'], lw['conf_dw'], row(lw['conf_dw_b']), row(lw['conf_ln_g']), row(lw['conf_ln_b']),
             lw['sc_dw'], lw['pool_w_bd'], row(lw['pool_scale']), row(lw['mix_out_g']), lw['w_out']]
    assert 1 + len(small) == MIXER_INPUTS
    jobs = [_cast_job(w, layer, bsz * n_t, step_of) for w, layer in cast]
    state_spec = lambda rows: pl.BlockSpec((1, rows, D_GROUP), lambda b, t: (b, 0, 0))
    outs = pl.pallas_call(
        functools.partial(_mixer_prompt_kernel, tt=tt, n_t=n_t, n_cast=len(jobs)),
        grid=(bsz, n_t),
        in_specs=[pl.BlockSpec((1, tt, D_MODEL), lambda b, t: (b, t, 0))] + [_const_spec(a.shape) for a in small]
        + [j[0] for j in jobs],
        out_specs=[pl.BlockSpec((1, tt, D_MODEL), lambda b, t: (b, t, 0)),
                   state_spec(CONF_WIDTH - 1), state_spec(SC_WIDTH - 1), state_spec(POOL_BUF)] + [j[1] for j in jobs],
        out_shape=[jax.ShapeDtypeStruct(x.shape, F32),
                   jax.ShapeDtypeStruct((bsz, CONF_WIDTH - 1, D_GROUP), F32),
                   jax.ShapeDtypeStruct((bsz, SC_WIDTH - 1, D_GROUP), F32),
                   jax.ShapeDtypeStruct((bsz, POOL_BUF, D_GROUP), F32)] + [j[2] for j in jobs],
        scratch_shapes=[pltpu.VMEM((GLU_HALO + tt, LANES), F32)] * LANE_SLABS + [pltpu.VMEM((tt, LANES), F32)] * LANE_SLABS
        + [pltpu.VMEM((SC_HALO + tt, D_GROUP), F32)]
        + [pltpu.VMEM((POOL_HALO + tt, D_GROUP), F32)] * 4,
        compiler_params=pltpu.CompilerParams(dimension_semantics=("arbitrary", "arbitrary"),
                                             vmem_limit_bytes=VMEM_LIMIT),
        name="mixer_prompt",
    )(x, *small, *[w for w, _ in cast])
    return (*outs[:MIXER_OUTPUTS], list(outs[MIXER_OUTPUTS:]))


SAMPLE_MIXER_INPUTS = 21


def _mixer_sample_kernel(*refs, layer):
    (x_ref, stg_ref, sts_ref, stp_ref, g_ref, win_ref, lng_ref, lnb_ref, w00_ref, b0_ref, cw_ref, cb_ref, clg_ref,
     clb_ref, sw_ref, pw_ref, ps_ref, mog_ref, wout_ref, gx_ref, wq_ref) = refs[:SAMPLE_MIXER_INPUTS]
    n_earlier = 3 if layer else 0
    earlier = refs[SAMPLE_MIXER_INPUTS:SAMPLE_MIXER_INPUTS + n_earlier]
    o_ref, oq_ref, ovn_ref, *new_hist = refs[SAMPLE_MIXER_INPUTS + n_earlier:]
    for src, dst in zip(earlier, new_hist):
        dst[0:layer] = src[...]
    oglu_ref, osh_ref, opool_ref = [ref.at[layer] for ref in new_hist]
    x = x_ref[...]
    h = _rms(x, g_ref[...]).astype(BF16)
    z = _dot(h, win_ref[...])
    u, v, glu_a, glu_g, sc_b, sc_c, sc_x, pool_x = [z[:, i * D_GROUP:(i + 1) * D_GROUP]
                                                    for i in range(D_IN_PROJ // D_GROUP)]
    hist = lambda ref, k: ref[:, k, :]

    def push(new_ref, old_ref, row):
        n_hist = old_ref.shape[1]
        new_ref[:, 0:n_hist - 1, :] = old_ref[:, 1:n_hist, :]
        new_ref[:, n_hist - 1, :] = row

    vn = _layer_norm(v, lng_ref[...], lnb_ref[...])
    ovn_ref[...] = vn
    y_a = u * (w00_ref[...] * vn + b0_ref[...])

    glu = glu_a * _sigmoid(glu_g)
    n_hist = CONF_WIDTH - 1
    acc = cw_ref[n_hist:n_hist + 1, :] * glu
    for k in range(n_hist):
        acc = acc + cw_ref[k:k + 1, :] * hist(stg_ref, k)
    ln = _layer_norm(acc + cb_ref[...], clg_ref[...], clb_ref[...])
    y_b = ln * _sigmoid(ln)
    push(oglu_ref, stg_ref, glu)

    sxc = sc_c * sc_x
    y_c = sc_b * (sw_ref[0:1, :] * hist(sts_ref, 0) + sw_ref[1:2, :] * hist(sts_ref, 1) + sw_ref[2:3, :] * sxc)
    push(osh_ref, sts_ref, sxc)

    run = pool_x
    sums = []
    back = 0
    for w in POOL_WINDOWS:
        while back < w - 1:
            run = run + hist(stp_ref, POOL_BUF - 1 - back)
            back += 1
        sums.append(run)
    grp, win = _pool_window_lanes(pool_x.shape)
    cnt = jnp.minimum(win, float(PAST_LEN + 1))
    pooled = _select_by_group(grp, sums) / cnt - pool_x
    y_d = _dot(pooled.astype(BF16), pw_ref[...]) * ps_ref[...]
    push(opool_ref, stp_ref, pool_x)

    x_mixed = _mix_out(x, [y_a, y_b, y_c, y_d], mog_ref, wout_ref)
    o_ref[...] = x_mixed
    oq_ref[...] = _dot(_rms(x_mixed, gx_ref[...]).astype(BF16), wq_ref[...]) * (XATTN_HEAD_DIM ** -0.5)


def _mixer_sample(xs, st_glu, st_sh, st_pool, layer, lw, earlier=()):
    n = xs.shape[0]
    row = lambda a: a.reshape(1, -1)
    states = [st_glu, st_sh, st_pool]
    earlier = list(earlier)
    assert len(earlier) == (len(states) if layer else 0)
    consts = [row(lw['norm_mix']), lw['w_in'], row(lw['gmlp_ln_g']), row(lw['gmlp_ln_b']), row(lw['gmlp_w00']),
              row(lw['gmlp_b0']), lw['conf_dw'], row(lw['conf_dw_b']), row(lw['conf_ln_g']), row(lw['conf_ln_b']),
              lw['sc_dw'], lw['pool_w_bd'], row(lw['pool_scale']), row(lw['mix_out_g']), lw['w_out'],
              row(lw['norm_xattn'])]
    assert 1 + len(states) + len(consts) + 1 == SAMPLE_MIXER_INPUTS
    out_shapes = [(n, D_MODEL), (n, D_MODEL), (n, D_GROUP)] + [(layer + 1,) + s.shape[1:] for s in states]
    return pl.pallas_call(
        functools.partial(_mixer_sample_kernel, layer=layer),
        grid=(1,),
        in_specs=[_const_spec(xs.shape)] + [_layer_spec(s, layer) for s in states]
        + [_const_spec(a.shape) for a in consts] + [_layer_spec(lw['w_xq_layers'], layer)]
        + [_const_spec(a.shape) for a in earlier],
        out_specs=[pl.BlockSpec(s, lambda i, nd=len(s): (0,) * nd) for s in out_shapes],
        out_shape=[jax.ShapeDtypeStruct(s, F32) for s in out_shapes],
        compiler_params=pltpu.CompilerParams(dimension_semantics=("arbitrary",), vmem_limit_bytes=VMEM_LIMIT),
        name="mixer_sample",
    )(xs, *states, *consts, lw['w_xq_layers'], *earlier)


MEM_KV_INPUTS = 6
MEM_KV_OUTPUTS = 6


def _mem_kv_kernel(*refs, n_cast):
    m_ref, g_ref, wk_ref, wv_ref, wq_ref, wo_ref = refs[:MEM_KV_INPUTS]
    cast_src = refs[MEM_KV_INPUTS:MEM_KV_INPUTS + n_cast]
    outs = refs[MEM_KV_INPUTS + n_cast:]
    ok_ref, ov_ref, oqk_ref, ovo_ref, wq_bf16, wo_bf16 = outs[:MEM_KV_OUTPUTS]
    cast_dst = outs[MEM_KV_OUTPUTS:MEM_KV_OUTPUTS + n_cast]
    wk_bf16, wv_bf16 = outs[MEM_KV_OUTPUTS + n_cast:]
    _run_cast_jobs(cast_src, cast_dst)

    @pl.when(pl.program_id(1) == 0)
    def _():
        for src, dst in ((wk_ref, wk_bf16), (wv_ref, wv_bf16), (wq_ref, wq_bf16), (wo_ref, wo_bf16)):
            dst[...] = src[...].astype(BF16)

    m = _rms(m_ref[0], g_ref[...]).astype(BF16)
    k = _dot(m, wk_bf16[...])
    v = _dot(m, wv_bf16[...])
    contract_last = (((1,), (1,)), ((), ()))
    for h in range(XATTN_HEADS):
        lo, hi = h * XATTN_HEAD_DIM, (h + 1) * XATTN_HEAD_DIM
        ok_ref[0, :, h, :] = k[:, lo:hi]
        ov_ref[0, :, h, :] = v[:, lo:hi]
        qk = lax.dot_general(wq_bf16[:, lo:hi], k[:, lo:hi].astype(BF16), contract_last,
                             preferred_element_type=F32)
        oqk_ref[0, :, h * MEM_LEN:(h + 1) * MEM_LEN] = (qk * (XATTN_HEAD_DIM ** -0.5)).astype(BF16)
        ovo_ref[0, h * MEM_LEN:(h + 1) * MEM_LEN, :] = _dot(v[:, lo:hi].astype(BF16), wo_bf16[lo:hi, :]).astype(BF16)


def _mem_kv(mem, norm_mem, w_xk, w_xv, w_xq, w_xo, *, cast=()):
    depth = w_xk.shape[0]
    bsz = mem.shape[0]
    g = norm_mem.reshape(depth, 1, D_MODEL)
    per_layer = lambda *shape, **kw: pl.BlockSpec((None,) + shape, lambda l, b: (l,) + (0,) * len(shape), **kw)
    out_blk = lambda *shape: pl.BlockSpec((None, 1) + shape, lambda l, b: (l, b) + (0,) * len(shape))
    weights = [w_xk, w_xv, w_xq, w_xo]
    jobs = [_cast_job(w, layer, depth * bsz, lambda l, b: l * bsz + b) for w, layer in cast]
    outs = pl.pallas_call(
        functools.partial(_mem_kv_kernel, n_cast=len(jobs)),
        grid=(depth, bsz),
        in_specs=[pl.BlockSpec((1, MEM_LEN, D_MODEL), lambda l, b: (b, 0, 0)), per_layer(1, D_MODEL)]
        + [per_layer(D_MODEL, D_MODEL, pipeline_mode=pl.Buffered(1))] * len(weights) + [j[0] for j in jobs],
        out_specs=[out_blk(MEM_LEN, XATTN_HEADS, XATTN_HEAD_DIM), out_blk(MEM_LEN, XATTN_HEADS, XATTN_HEAD_DIM),
                   out_blk(D_MODEL, XATTN_HEADS * MEM_LEN), out_blk(XATTN_HEADS * MEM_LEN, D_MODEL),
                   per_layer(D_MODEL, D_MODEL), per_layer(D_MODEL, D_MODEL)] + [j[1] for j in jobs],
        out_shape=[jax.ShapeDtypeStruct((depth, bsz, MEM_LEN, XATTN_HEADS, XATTN_HEAD_DIM), F32)] * 2
        + [jax.ShapeDtypeStruct((depth, bsz, D_MODEL, XATTN_HEADS * MEM_LEN), BF16),
           jax.ShapeDtypeStruct((depth, bsz, XATTN_HEADS * MEM_LEN, D_MODEL), BF16)]
        + [jax.ShapeDtypeStruct((depth, D_MODEL, D_MODEL), BF16)] * 2 + [j[2] for j in jobs],
        scratch_shapes=[pltpu.VMEM((D_MODEL, D_MODEL), BF16)] * 2,
        compiler_params=pltpu.CompilerParams(dimension_semantics=("arbitrary", "arbitrary"),
                                             vmem_limit_bytes=VMEM_LIMIT),
        name="mem_kv",
    )(mem, g, *weights, *[w for w, _ in cast])
    return (*outs[:MEM_KV_OUTPUTS], list(outs[MEM_KV_OUTPUTS:]))


def _xattn_prompt_kernel(x_ref, g_ref, qk_ref, vo_ref, o_ref):
    x = x_ref[0]
    h = _rms(x, g_ref[...]).astype(BF16)
    s = _dot(h, qk_ref[0])
    probs = []
    for hh in range(XATTN_HEADS):
        sh = s[:, hh * MEM_LEN:(hh + 1) * MEM_LEN]
        e = jnp.exp(sh - jnp.max(sh, axis=-1, keepdims=True))
        probs.append((e * (1.0 / jnp.sum(e, axis=-1, keepdims=True))).astype(BF16))
    o_ref[0] = x + _dot(jnp.concatenate(probs, axis=1), vo_ref[0])


def _xattn_prompt(x, qk, vo, layer, lw, *, tt=1024):
    bsz, seq, _ = x.shape
    g = lw['norm_xattn'].reshape(1, -1)
    folded = lambda a: pl.BlockSpec((None, 1) + a.shape[2:], lambda b, t: (layer, b, 0, 0))
    return pl.pallas_call(
        _xattn_prompt_kernel,
        grid=(bsz, seq // tt),
        in_specs=[pl.BlockSpec((1, tt, D_MODEL), lambda b, t: (b, t, 0)), _const_spec(g.shape),
                  folded(qk), folded(vo)],
        out_specs=pl.BlockSpec((1, tt, D_MODEL), lambda b, t: (b, t, 0)),
        out_shape=jax.ShapeDtypeStruct(x.shape, F32),
        compiler_params=pltpu.CompilerParams(dimension_semantics=("arbitrary", "arbitrary"),
                                             vmem_limit_bytes=VMEM_LIMIT),
        name="xattn_prompt",
    )(x, g, qk, vo)


def _split_head_dim(a):
    lead = a.shape[:-2]
    a = a.reshape(*lead, XATTN_HEADS, 2, XATTN_HEAD_DIM // 2)
    return jnp.swapaxes(a, -3, -2).reshape(*lead, 2 * XATTN_HEADS, XATTN_HEAD_DIM // 2)


def _merge_head_dim(a):
    lead = a.shape[:-2]
    a = a.reshape(*lead, 2, XATTN_HEADS, XATTN_HEAD_DIM // 2)
    return jnp.swapaxes(a, -3, -2).reshape(*lead, XATTN_HEADS, XATTN_HEAD_DIM)


def _attend_rows(q_ref, k_ref, v_ref, o_ref, rows):
    for r in rows:
        part = jnp.sum(k_ref[r] * q_ref[r][None], axis=-1, keepdims=True)
        s = part + pltpu.roll(part, XATTN_HEADS, axis=1)
        e = jnp.exp(s - jnp.max(s, axis=0, keepdims=True))
        o_ref[r] = jnp.sum(e * v_ref[r], axis=0) * (1.0 / jnp.sum(e, axis=0))


def _ffn_rows(x, g_ref, w1_ref, w2_ref, gf_ref, *, final_norm, between_chunks=None):
    h = _rms(x, g_ref[...]).astype(BF16)
    y = x
    for i, c in enumerate(range(0, D_FF, FF_CHUNK)):
        if between_chunks is not None:
            between_chunks(i)
        a = jnp.maximum(_dot(h, w1_ref[:, c:c + FF_CHUNK]), 0.0)
        y = y + _dot((a * a).astype(BF16), w2_ref[c:c + FF_CHUNK, :])
    return _rms(y, gf_ref[...]) if final_norm else y


def _ffn_kernel(x_ref, g_ref, w1_ref, w2_ref, gf_ref, o_ref, *, final_norm, between_chunks=None):
    o_ref[...] = _ffn_rows(x_ref[...], g_ref, w1_ref, w2_ref, gf_ref, final_norm=final_norm,
                           between_chunks=between_chunks)


def _oproj_ffn_kernel(x_ref, a_ref, wo_ref, g_ref, w1_ref, w2_ref, gf_ref, o_ref, h_buf, *, final_norm, n_chunks):
    c = pl.program_id(0)

    @pl.when(c == 0)
    def _():
        x = x_ref[...] + _dot(a_ref[...].astype(BF16), wo_ref[...])
        o_ref[...] = x
        h_buf[...] = _rms(x, g_ref[...]).astype(BF16)

    a = jnp.maximum(_dot(h_buf[...], w1_ref[...]), 0.0)
    o_ref[...] += _dot((a * a).astype(BF16), w2_ref[...])
    if final_norm:
        @pl.when(c == n_chunks - 1)
        def _():
            o_ref[...] = _rms(o_ref[...], gf_ref[...])


def _oproj_ffn(x, attn, layer, lw, norm_final, *, final_norm, chunk=1024):
    n_chunks = D_FF // chunk
    g, gf = lw['norm_ffn'].reshape(1, -1), norm_final.reshape(1, -1)
    return pl.pallas_call(
        functools.partial(_oproj_ffn_kernel, final_norm=final_norm, n_chunks=n_chunks),
        grid=(n_chunks,),
        in_specs=[_const_spec(x.shape), _const_spec(attn.shape), _layer_spec(lw['w_xo_layers'], layer),
                  _const_spec(g.shape), pl.BlockSpec((D_MODEL, chunk), lambda c: (0, c)),
                  pl.BlockSpec((chunk, D_MODEL), lambda c: (c, 0)), _const_spec(gf.shape)],
        out_specs=pl.BlockSpec(x.shape, lambda c: (0, 0)),
        out_shape=jax.ShapeDtypeStruct(x.shape, F32),
        scratch_shapes=[pltpu.VMEM(x.shape, BF16)],
        compiler_params=pltpu.CompilerParams(dimension_semantics=("arbitrary",), vmem_limit_bytes=VMEM_LIMIT),
        name="oproj_ffn",
    )(x, attn, lw['w_xo_layers'], g, lw['w_ff1'], lw['w_ff2'], gf)


FFN_ATTEND_INPUTS = 8
FFN_ATTEND_OUTPUTS = 2


def _ffn_attend_kernel(*refs, final_norm, rows, n_cast):
    x_ref, g_ref, w1_ref, w2_ref, gf_ref, q_ref, k_ref, v_ref = refs[:FFN_ATTEND_INPUTS]
    cast_src = refs[FFN_ATTEND_INPUTS:FFN_ATTEND_INPUTS + n_cast]
    o_ref, oa_ref = refs[FFN_ATTEND_INPUTS + n_cast:FFN_ATTEND_INPUTS + n_cast + FFN_ATTEND_OUTPUTS]
    cast_dst = refs[FFN_ATTEND_INPUTS + n_cast + FFN_ATTEND_OUTPUTS:]
    _run_cast_jobs(cast_src, cast_dst)
    n_chunks = D_FF // FF_CHUNK
    assert rows % n_chunks == 0
    per_chunk = rows // n_chunks
    attend = lambda i: _attend_rows(q_ref, k_ref, v_ref, oa_ref, range(i * per_chunk, (i + 1) * per_chunk))
    _ffn_kernel(x_ref, g_ref, w1_ref, w2_ref, gf_ref, o_ref, final_norm=final_norm, between_chunks=attend)


def _ffn_attend(x2d, lw, norm_final, attend, *, final_norm, tm=512, cast=()):
    n = x2d.shape[0]
    g = lw['norm_ffn'].reshape(1, -1)
    gf = norm_final.reshape(1, -1)
    in_specs = [pl.BlockSpec((tm, D_MODEL), lambda i: (i, 0)), _const_spec(g.shape),
                _const_spec(lw['w_ff1'].shape), _const_spec(lw['w_ff2'].shape), _const_spec(gf.shape)]
    out_spec = pl.BlockSpec((tm, D_MODEL), lambda i: (i, 0))
    out_shape = jax.ShapeDtypeStruct(x2d.shape, F32)
    params = pltpu.CompilerParams(dimension_semantics=("arbitrary",), vmem_limit_bytes=VMEM_LIMIT)
    q, cache_k, cache_v, layer = attend
    rows = q.shape[0] // (n // tm)
    assert rows * (n // tm) == q.shape[0]
    q_spec = pl.BlockSpec((rows,) + q.shape[1:], lambda i: (i, 0, 0))
    kv_spec = pl.BlockSpec((None, rows) + cache_k.shape[2:], lambda i: (layer, i, 0, 0, 0))
    jobs = [_cast_job(w, wl, n // tm, lambda i: i) for w, wl in cast]
    outs = pl.pallas_call(
        functools.partial(_ffn_attend_kernel, final_norm=final_norm, rows=rows, n_cast=len(jobs)), grid=(n // tm,),
        in_specs=in_specs + [q_spec, kv_spec, kv_spec] + [j[0] for j in jobs],
        out_specs=[out_spec, q_spec] + [j[1] for j in jobs],
        out_shape=[out_shape, jax.ShapeDtypeStruct(q.shape, F32)] + [j[2] for j in jobs],
        compiler_params=params, name="ffn_attend",
    )(x2d, g, lw['w_ff1'], lw['w_ff2'], gf, q, cache_k, cache_v, *[w for w, _ in cast])
    return outs[0], outs[1], list(outs[FFN_ATTEND_OUTPUTS:])


PROJ_WEIGHTS = ('w_in', 'w_out')


def _layer_weights(l, p, proj_bf16):
    tril = jnp.tril(jnp.ones((CHUNK, CHUNK), dtype=bool))
    ws = jnp.where(tril[None], p['gmlp_ws'][l], 0.0)
    pool_bd = jax.scipy.linalg.block_diag(*[p['pool_w'][l, g] for g in range(len(POOL_WINDOWS))])
    lw = {k: p[k][l] for k in ('norm_mix', 'gmlp_ln_g', 'gmlp_ln_b', 'conf_dw', 'conf_dw_b', 'conf_ln_g',
                               'conf_ln_b', 'sc_dw', 'pool_scale', 'mix_out_g', 'norm_xattn', 'norm_ffn')}
    lw.update(proj_bf16)
    lw.update(
        gmlp_wcat=jnp.concatenate([ws[h] for h in range(GMLP_HEADS)], axis=1).astype(BF16),
        gmlp_bias_tile=jnp.repeat(p['gmlp_bs'][l].T, GMLP_HEAD_DIM, axis=1),
        gmlp_w00=jnp.repeat(ws[:, 0, 0], GMLP_HEAD_DIM), gmlp_b0=jnp.repeat(p['gmlp_bs'][l][:, 0], GMLP_HEAD_DIM),
        pool_w_bd=pool_bd.astype(BF16))
    return lw


def kernel(x_prompt, x_sample, mem_prompt, cache_mem_k, cache_mem_v, state_conv_glu, state_conv_short, state_pool, norm_mix, w_in, gmlp_ln_g, gmlp_ln_b, gmlp_ws, gmlp_bs, conf_dw, conf_dw_b, conf_ln_g, conf_ln_b, sc_dw, pool_w, pool_scale, mix_out_g, w_out, norm_xattn, norm_mem, w_xq, w_xk, w_xv, w_xo, norm_ffn, w_ff1, w_ff2, norm_final):
    params = dict(norm_mix=norm_mix, w_in=w_in, gmlp_ln_g=gmlp_ln_g, gmlp_ln_b=gmlp_ln_b, gmlp_ws=gmlp_ws,
                  gmlp_bs=gmlp_bs, conf_dw=conf_dw, conf_dw_b=conf_dw_b, conf_ln_g=conf_ln_g, conf_ln_b=conf_ln_b,
                  sc_dw=sc_dw, pool_w=pool_w, pool_scale=pool_scale, mix_out_g=mix_out_g, w_out=w_out,
                  norm_xattn=norm_xattn, norm_mem=norm_mem, w_xq=w_xq, w_xk=w_xk, w_xv=w_xv, w_xo=w_xo,
                  norm_ffn=norm_ffn, w_ff1=w_ff1, w_ff2=w_ff2)
    depth = w_in.shape[0]
    bsz, seq, _ = x_prompt.shape
    n_s = x_sample.shape[0]
    xp = x_prompt
    xs = x_sample.reshape(n_s, D_MODEL)
    outs = {k: [] for k in ('glu_p', 'sh_p', 'pl_p', 'v_s')}
    hist_s = ()
    mem_k, mem_v, mem_qk, mem_vo, wq_bf16, wo_bf16, proj = _mem_kv(
        mem_prompt, norm_mem, w_xk, w_xv, w_xq, w_xo, cast=[(params[k], 0) for k in PROJ_WEIGHTS])
    cache_k_split = _split_head_dim(cache_mem_k)
    cache_v_split = _split_head_dim(cache_mem_v)
    for l in range(depth):
        lw = _layer_weights(l, params, dict(zip(PROJ_WEIGHTS, proj), w_xq_layers=wq_bf16, w_xo_layers=wo_bf16))
        last = l == depth - 1
        xs, q_s, vn_s, *hist_s = _mixer_sample(xs, state_conv_glu, state_conv_short, state_pool, l, lw, hist_s)
        q_s = _split_head_dim(q_s.reshape(n_s, XATTN_HEADS, XATTN_HEAD_DIM))
        xp, glu_p, sh_p, pool_p, (lw['w_ff1'], lw['w_ff2']) = _mixer_prompt(
            xp, lw, cast=[(params['w_ff1'], l), (params['w_ff2'], l)])
        xp = _xattn_prompt(xp, mem_qk, mem_vo, l, lw)
        outs['glu_p'].append(glu_p); outs['sh_p'].append(sh_p); outs['pl_p'].append(pool_p)
        xp, o_s, proj = _ffn_attend(xp.reshape(bsz * seq, D_MODEL), lw, norm_final,
                                    (q_s, cache_k_split, cache_v_split, l), final_norm=last,
                                    cast=[] if last else [(params[k], l + 1) for k in PROJ_WEIGHTS])
        xp = xp.reshape(bsz, seq, D_MODEL)
        xs = _oproj_ffn(xs, _merge_head_dim(o_s).reshape(n_s, D_MODEL), l, lw, norm_final, final_norm=last)
        outs['v_s'].append(vn_s.reshape(n_s, 1, D_GROUP))
    st = lambda k: jnp.stack(outs[k], axis=0)
    glu_s, sh_s, pool_s = hist_s
    return (xp, xs.reshape(n_s, 1, D_MODEL), mem_k, mem_v, st('glu_p'), glu_s, st('sh_p'), sh_s,
            st('pl_p'), pool_s, st('v_s'))
```

```python
import functools

import jax
import jax.numpy as jnp
from jax import lax
from jax.experimental import pallas as pl
from jax.experimental.pallas import tpu as pltpu

F32 = jnp.float32
BF16 = jnp.bfloat16

D_MODEL = 1024
D_GROUP = 256
N_GROUPS = 4
D_IN_PROJ = 8 * D_GROUP
GMLP_HEADS = 4
GMLP_HEAD_DIM = D_GROUP // GMLP_HEADS
CHUNK = 128
CONF_WIDTH = 31
SC_WIDTH = 3
POOL_WINDOWS = (2, 4, 8, 16)
POOL_GROUP_DIM = D_GROUP // len(POOL_WINDOWS)
POOL_BUF = max(POOL_WINDOWS) - 1
MEM_LEN = 256
XATTN_HEADS = 4
XATTN_HEAD_DIM = D_MODEL // XATTN_HEADS
D_FF = 4 * D_MODEL
PAST_LEN = 16384
EPS = 1e-6

GLU_HALO = 32
SC_HALO = 8
POOL_HALO = 32
LANES = 128
SUBLANES = 8
BF16_SUBLANES = 16
LANE_SLABS = D_GROUP // LANES
CONV_STRIDE = 4
CONV_ROWS = 128
FF_CHUNK = 4096
VMEM_LIMIT = 56 * 1024 * 1024


def _rms(x, g):
    return x * lax.rsqrt(jnp.mean(x * x, axis=-1, keepdims=True) + EPS) * g


def _layer_norm(x, g, b):
    xc = x - jnp.mean(x, axis=-1, keepdims=True)
    return xc * lax.rsqrt(jnp.mean(xc * xc, axis=-1, keepdims=True) + EPS) * g + b


def _sigmoid(x):
    return 0.5 * jnp.tanh(0.5 * x) + 0.5


def _dot(a, b):
    return jnp.dot(a, b, preferred_element_type=F32)


def _const_spec(shape):
    zeros = (0,) * len(shape)
    return pl.BlockSpec(shape, lambda *_: zeros, pipeline_mode=pl.Buffered(1))


def _layer_spec(stacked, layer):
    zeros = (0,) * (stacked.ndim - 1)
    return pl.BlockSpec((None,) + stacked.shape[1:], lambda *_: (layer,) + zeros, pipeline_mode=pl.Buffered(1))


def _param_spec(a, layer):
    return _layer_spec(a, layer) if a.ndim == 3 else _const_spec(a.shape)


def _cast_job(stacked, layer, steps, step_of):
    _, r, c = stacked.shape
    blk = r // steps
    assert blk * steps == r and blk % BF16_SUBLANES == 0
    return (pl.BlockSpec((None, blk, c), lambda *ids: (layer, step_of(*ids), 0)),
            pl.BlockSpec((blk, c), lambda *ids: (step_of(*ids), 0)),
            jax.ShapeDtypeStruct((r, c), BF16))


def _run_cast_jobs(src_refs, dst_refs):
    for src, dst in zip(src_refs, dst_refs):
        dst[...] = src[...].astype(BF16)


def _pool_window_lanes(shape):
    grp = lax.broadcasted_iota(jnp.int32, shape, len(shape) - 1) // POOL_GROUP_DIM
    win = jnp.full(shape, float(POOL_WINDOWS[-1]), F32)
    for g in range(len(POOL_WINDOWS) - 2, -1, -1):
        win = jnp.where(grp == g, float(POOL_WINDOWS[g]), win)
    return grp, win


def _select_by_group(grp, vals):
    out = vals[-1]
    for g in range(len(vals) - 2, -1, -1):
        out = jnp.where(grp == g, vals[g], out)
    return out


def _mix_out(x, y_groups, mog_ref, wout_ref):
    yn = [_rms(y, mog_ref[:, g * D_GROUP:(g + 1) * D_GROUP]).astype(BF16) for g, y in enumerate(y_groups)]
    return x + _dot(jnp.concatenate(yn, axis=1), wout_ref[...])


MIXER_INPUTS = 16
MIXER_OUTPUTS = 4


def _mixer_prompt_kernel(*refs, tt, n_t, n_cast):
    (x_ref, g_ref, win_ref, lng_ref, lnb_ref, wcat_ref, bst_ref, cw_ref, cb_ref, clg_ref,
     clb_ref, sw_ref, pw_ref, ps_ref, mog_ref, wout_ref) = refs[:MIXER_INPUTS]
    cast_src = refs[MIXER_INPUTS:MIXER_INPUTS + n_cast]
    outs = refs[MIXER_INPUTS + n_cast:]
    o_ref, oglu_ref, osh_ref, opool_ref = outs[:MIXER_OUTPUTS]
    cast_dst = outs[MIXER_OUTPUTS:MIXER_OUTPUTS + n_cast]
    glu_buf0, glu_buf1, yb_buf0, yb_buf1, sc_buf, p_buf, s2_buf, s4_buf, s8_buf = outs[MIXER_OUTPUTS + n_cast:]
    _run_cast_jobs(cast_src, cast_dst)
    t = pl.program_id(1)
    glu_buf = (glu_buf0, glu_buf1)
    yb_buf = (yb_buf0, yb_buf1)

    @pl.when(t == 0)
    def _():
        for s in range(LANE_SLABS):
            glu_buf[s][0:GLU_HALO, :] = jnp.zeros((GLU_HALO, LANES), F32)
        sc_buf[0:SC_HALO, :] = jnp.zeros((SC_HALO, D_GROUP), F32)
        p_buf[0:POOL_HALO, :] = jnp.zeros((POOL_HALO, D_GROUP), F32)

    _mix_tile(t, x_ref, g_ref, win_ref, lng_ref, lnb_ref, wcat_ref, bst_ref, cw_ref, cb_ref, clg_ref,
              clb_ref, sw_ref, pw_ref, ps_ref, mog_ref, wout_ref, o_ref,
              glu_buf, yb_buf, sc_buf, p_buf, s2_buf, s4_buf, s8_buf, tt=tt)

    @pl.when(t == n_t - 1)
    def _():
        oglu_ref[0] = jnp.concatenate(
            [glu_buf[s][GLU_HALO + tt - (CONF_WIDTH - 1):GLU_HALO + tt, :] for s in range(LANE_SLABS)], axis=1)
        osh_ref[0] = sc_buf[SC_HALO + tt - (SC_WIDTH - 1):SC_HALO + tt, :]
        opool_ref[0] = p_buf[POOL_HALO + tt - POOL_BUF:POOL_HALO + tt, :]

    for s in range(LANE_SLABS):
        glu_buf[s][0:GLU_HALO, :] = glu_buf[s][tt:tt + GLU_HALO, :]
    sc_buf[0:SC_HALO, :] = sc_buf[tt:tt + SC_HALO, :]
    p_buf[0:POOL_HALO, :] = p_buf[tt:tt + POOL_HALO, :]


def _mix_tile(t, x_ref, g_ref, win_ref, lng_ref, lnb_ref, wcat_ref, bst_ref, cw_ref, cb_ref, clg_ref,
              clb_ref, sw_ref, pw_ref, ps_ref, mog_ref, wout_ref, o_ref,
              glu_buf, yb_buf, sc_buf, p_buf, s2_buf, s4_buf, s8_buf, *, tt):
    x = x_ref[0]
    z = _dot(_rms(x, g_ref[...]).astype(BF16), win_ref[...])
    zcol = lambda i: z[:, i * D_GROUP:(i + 1) * D_GROUP]
    group_norm = lambda g, y: _rms(y, mog_ref[:, g * D_GROUP:(g + 1) * D_GROUP]).astype(BF16)
    yn = [None] * N_GROUPS

    vn = _layer_norm(zcol(1), lng_ref[...], lnb_ref[...])
    head = lax.broadcasted_iota(jnp.int32, (CHUNK, D_GROUP), 1) // GMLP_HEAD_DIM
    gates = []
    for c in range(tt // CHUNK):
        vc = vn[c * CHUNK:(c + 1) * CHUNK]
        stack = jnp.concatenate([jnp.where(head == hh, vc, 0.0) for hh in range(GMLP_HEADS)], axis=0)
        gates.append(_dot(wcat_ref[...], stack.astype(BF16)) + bst_ref[...])
    yn[0] = group_norm(0, zcol(0) * jnp.concatenate(gates, axis=0))

    first = GLU_HALO - (CONF_WIDTH - 1)
    span = SUBLANES * CONV_STRIDE
    glu = zcol(2) * _sigmoid(zcol(3))
    for s in range(LANE_SLABS):
        glu_buf[s][GLU_HALO:GLU_HALO + tt, :] = glu[:, s * LANES:(s + 1) * LANES]
    for c0 in range(0, tt, CONV_ROWS):
        starts = [c0 + (i // CONV_STRIDE) * span + i % CONV_STRIDE for i in range(CONV_ROWS // SUBLANES)]
        accs = [[None] * len(starts) for _ in range(LANE_SLABS)]
        for k in range(CONF_WIDTH):
            for s in range(LANE_SLABS):
                wk = jnp.broadcast_to(cw_ref[k:k + 1, s * LANES:(s + 1) * LANES], (SUBLANES, LANES))
                for i, t0 in enumerate(starts):
                    term = wk * glu_buf[s][pl.ds(first + t0 + k, SUBLANES, stride=CONV_STRIDE), :]
                    accs[s][i] = term if k == 0 else accs[s][i] + term
        conv = jnp.concatenate([jnp.concatenate(a, axis=0) for a in accs], axis=1)
        ln = _layer_norm(conv + cb_ref[...], clg_ref[...], clb_ref[...])
        yb = ln * _sigmoid(ln)
        for s in range(LANE_SLABS):
            for i, t0 in enumerate(starts):
                yb_buf[s][pl.ds(t0, SUBLANES, stride=CONV_STRIDE), :] = (
                    yb[i * SUBLANES:(i + 1) * SUBLANES, s * LANES:(s + 1) * LANES])
    yn[1] = group_norm(1, jnp.concatenate([yb_buf[s][...] for s in range(LANE_SLABS)], axis=1))

    sxc = zcol(5) * zcol(6)
    sc_buf[SC_HALO:SC_HALO + tt, :] = sxc
    conv_c = (sw_ref[0:1, :] * sc_buf[SC_HALO - 2:SC_HALO - 2 + tt, :]
              + sw_ref[1:2, :] * sc_buf[SC_HALO - 1:SC_HALO - 1 + tt, :]
              + sw_ref[2:3, :] * sxc)
    yn[2] = group_norm(2, zcol(4) * conv_c)

    pool_x = zcol(7)
    lo, end = POOL_HALO, POOL_HALO + tt
    p_buf[lo:end, :] = pool_x
    lo2, lo4, lo8 = lo - 3 * SUBLANES, lo - 2 * SUBLANES, lo - SUBLANES
    s2_buf[lo2:end, :] = p_buf[lo2:end, :] + p_buf[lo2 - 1:end - 1, :]
    s4_buf[lo4:end, :] = s2_buf[lo4:end, :] + s2_buf[lo4 - 2:end - 2, :]
    s8_buf[lo8:end, :] = s4_buf[lo8:end, :] + s4_buf[lo8 - 4:end - 4, :]
    s16 = s8_buf[lo:end, :] + s8_buf[lo - 8:end - 8, :]
    grp = _pool_window_lanes((tt, D_GROUP))[0]
    ssum = _select_by_group(grp, [s2_buf[lo:end, :], s4_buf[lo:end, :], s8_buf[lo:end, :], s16])
    head_rows = POOL_BUF + 1
    pos = t * tt + lax.broadcasted_iota(jnp.int32, (head_rows, D_GROUP), 0)
    cnt = jnp.minimum(_pool_window_lanes((head_rows, D_GROUP))[1], (pos + 1).astype(F32))
    inv_win = _select_by_group(_pool_window_lanes((tt - head_rows, D_GROUP))[0], [1.0 / w for w in POOL_WINDOWS])
    mean = jnp.concatenate([ssum[:head_rows] / cnt, ssum[head_rows:] * inv_win], axis=0)
    pooled = mean - pool_x
    yn[3] = group_norm(3, _dot(pooled.astype(BF16), pw_ref[...]) * ps_ref[...])

    o_ref[0] = x + _dot(jnp.concatenate(yn, axis=1), wout_ref[...])


def _mixer_prompt(x, layer, lw, *, tt=1024, cast=()):
    bsz, seq, _ = x.shape
    n_t = seq // tt
    step_of = lambda b, t: b * n_t + t
    small = [lw['norm_mix'], lw['w_in'], lw['gmlp_ln_g'], lw['gmlp_ln_b'], lw['gmlp_wcat'],
             lw['gmlp_bias_tile'], lw['conf_dw'], lw['conf_dw_b'], lw['conf_ln_g'], lw['conf_ln_b'],
             lw['sc_dw'], lw['pool_w_bd'], lw['pool_scale'], lw['mix_out_g'], lw['w_out']]
    assert 1 + len(small) == MIXER_INPUTS
    jobs = [_cast_job(w, layer, bsz * n_t, step_of) for w, layer in cast]
    state_spec = lambda rows: pl.BlockSpec((1, rows, D_GROUP), lambda b, t: (b, 0, 0))
    outs = pl.pallas_call(
        functools.partial(_mixer_prompt_kernel, tt=tt, n_t=n_t, n_cast=len(jobs)),
        grid=(bsz, n_t),
        in_specs=[pl.BlockSpec((1, tt, D_MODEL), lambda b, t: (b, t, 0))] + [_param_spec(a, layer) for a in small]
        + [j[0] for j in jobs],
        out_specs=[pl.BlockSpec((1, tt, D_MODEL), lambda b, t: (b, t, 0)),
                   state_spec(CONF_WIDTH - 1), state_spec(SC_WIDTH - 1), state_spec(POOL_BUF)] + [j[1] for j in jobs],
        out_shape=[jax.ShapeDtypeStruct(x.shape, F32),
                   jax.ShapeDtypeStruct((bsz, CONF_WIDTH - 1, D_GROUP), F32),
                   jax.ShapeDtypeStruct((bsz, SC_WIDTH - 1, D_GROUP), F32),
                   jax.ShapeDtypeStruct((bsz, POOL_BUF, D_GROUP), F32)] + [j[2] for j in jobs],
        scratch_shapes=[pltpu.VMEM((GLU_HALO + tt, LANES), F32)] * LANE_SLABS + [pltpu.VMEM((tt, LANES), F32)] * LANE_SLABS
        + [pltpu.VMEM((SC_HALO + tt, D_GROUP), F32)]
        + [pltpu.VMEM((POOL_HALO + tt, D_GROUP), F32)] * 4,
        compiler_params=pltpu.CompilerParams(dimension_semantics=("arbitrary", "arbitrary"),
                                             vmem_limit_bytes=VMEM_LIMIT),
        name="mixer_prompt",
    )(x, *small, *[w for w, _ in cast])
    return (*outs[:MIXER_OUTPUTS], list(outs[MIXER_OUTPUTS:]))


SAMPLE_MIXER_INPUTS = 21


def _mixer_sample_kernel(*refs, layer):
    (x_ref, stg_ref, sts_ref, stp_ref, g_ref, win_ref, lng_ref, lnb_ref, w00_ref, b0_ref, cw_ref, cb_ref, clg_ref,
     clb_ref, sw_ref, pw_ref, ps_ref, mog_ref, wout_ref, gx_ref, wq_ref) = refs[:SAMPLE_MIXER_INPUTS]
    n_earlier = 3 if layer else 0
    earlier = refs[SAMPLE_MIXER_INPUTS:SAMPLE_MIXER_INPUTS + n_earlier]
    o_ref, oq_ref, ovn_ref, *new_hist = refs[SAMPLE_MIXER_INPUTS + n_earlier:]
    for src, dst in zip(earlier, new_hist):
        dst[0:layer] = src[...]
    oglu_ref, osh_ref, opool_ref = [ref.at[layer] for ref in new_hist]
    x = x_ref[...]
    h = _rms(x, g_ref[...]).astype(BF16)
    z = _dot(h, win_ref[...])
    u, v, glu_a, glu_g, sc_b, sc_c, sc_x, pool_x = [z[:, i * D_GROUP:(i + 1) * D_GROUP]
                                                    for i in range(D_IN_PROJ // D_GROUP)]
    hist = lambda ref, k: ref[:, k, :]

    def push(new_ref, old_ref, row):
        n_hist = old_ref.shape[1]
        new_ref[:, 0:n_hist - 1, :] = old_ref[:, 1:n_hist, :]
        new_ref[:, n_hist - 1, :] = row

    vn = _layer_norm(v, lng_ref[...], lnb_ref[...])
    ovn_ref[...] = vn
    y_a = u * (w00_ref[...] * vn + b0_ref[...])

    glu = glu_a * _sigmoid(glu_g)
    n_hist = CONF_WIDTH - 1
    acc = cw_ref[n_hist:n_hist + 1, :] * glu
    for k in range(n_hist):
        acc = acc + cw_ref[k:k + 1, :] * hist(stg_ref, k)
    ln = _layer_norm(acc + cb_ref[...], clg_ref[...], clb_ref[...])
    y_b = ln * _sigmoid(ln)
    push(oglu_ref, stg_ref, glu)

    sxc = sc_c * sc_x
    y_c = sc_b * (sw_ref[0:1, :] * hist(sts_ref, 0) + sw_ref[1:2, :] * hist(sts_ref, 1) + sw_ref[2:3, :] * sxc)
    push(osh_ref, sts_ref, sxc)

    run = pool_x
    sums = []
    back = 0
    for w in POOL_WINDOWS:
        while back < w - 1:
            run = run + hist(stp_ref, POOL_BUF - 1 - back)
            back += 1
        sums.append(run)
    grp, win = _pool_window_lanes(pool_x.shape)
    cnt = jnp.minimum(win, float(PAST_LEN + 1))
    pooled = _select_by_group(grp, sums) / cnt - pool_x
    y_d = _dot(pooled.astype(BF16), pw_ref[...]) * ps_ref[...]
    push(opool_ref, stp_ref, pool_x)

    x_mixed = _mix_out(x, [y_a, y_b, y_c, y_d], mog_ref, wout_ref)
    o_ref[...] = x_mixed
    oq_ref[...] = _dot(_rms(x_mixed, gx_ref[...]).astype(BF16), wq_ref[...]) * (XATTN_HEAD_DIM ** -0.5)


def _mixer_sample(xs, st_glu, st_sh, st_pool, layer, lw, earlier=()):
    n = xs.shape[0]
    states = [st_glu, st_sh, st_pool]
    earlier = list(earlier)
    assert len(earlier) == (len(states) if layer else 0)
    consts = [lw['norm_mix'], lw['w_in'], lw['gmlp_ln_g'], lw['gmlp_ln_b'], lw['gmlp_w00'],
              lw['gmlp_b0'], lw['conf_dw'], lw['conf_dw_b'], lw['conf_ln_g'], lw['conf_ln_b'],
              lw['sc_dw'], lw['pool_w_bd'], lw['pool_scale'], lw['mix_out_g'], lw['w_out'],
              lw['norm_xattn']]
    assert 1 + len(states) + len(consts) + 1 == SAMPLE_MIXER_INPUTS
    out_shapes = [(n, D_MODEL), (n, D_MODEL), (n, D_GROUP)] + [(layer + 1,) + s.shape[1:] for s in states]
    return pl.pallas_call(
        functools.partial(_mixer_sample_kernel, layer=layer),
        grid=(1,),
        in_specs=[_const_spec(xs.shape)] + [_layer_spec(s, layer) for s in states]
        + [_param_spec(a, layer) for a in consts] + [_layer_spec(lw['w_xq_layers'], layer)]
        + [_const_spec(a.shape) for a in earlier],
        out_specs=[pl.BlockSpec(s, lambda i, nd=len(s): (0,) * nd) for s in out_shapes],
        out_shape=[jax.ShapeDtypeStruct(s, F32) for s in out_shapes],
        compiler_params=pltpu.CompilerParams(dimension_semantics=("arbitrary",), vmem_limit_bytes=VMEM_LIMIT),
        name="mixer_sample",
    )(xs, *states, *consts, lw['w_xq_layers'], *earlier)


MEM_KV_INPUTS = 6
MEM_KV_OUTPUTS = 6


def _mem_kv_kernel(*refs, n_cast):
    m_ref, g_ref, wk_ref, wv_ref, wq_ref, wo_ref = refs[:MEM_KV_INPUTS]
    cast_src = refs[MEM_KV_INPUTS:MEM_KV_INPUTS + n_cast]
    outs = refs[MEM_KV_INPUTS + n_cast:]
    ok_ref, ov_ref, oqk_ref, ovo_ref, wq_bf16, wo_bf16 = outs[:MEM_KV_OUTPUTS]
    cast_dst = outs[MEM_KV_OUTPUTS:MEM_KV_OUTPUTS + n_cast]
    wk_bf16, wv_bf16 = outs[MEM_KV_OUTPUTS + n_cast:]
    _run_cast_jobs(cast_src, cast_dst)

    @pl.when(pl.program_id(1) == 0)
    def _():
        for src, dst in ((wk_ref, wk_bf16), (wv_ref, wv_bf16), (wq_ref, wq_bf16), (wo_ref, wo_bf16)):
            dst[...] = src[...].astype(BF16)

    m = _rms(m_ref[0], g_ref[...]).astype(BF16)
    k = _dot(m, wk_bf16[...])
    v = _dot(m, wv_bf16[...])
    contract_last = (((1,), (1,)), ((), ()))
    for h in range(XATTN_HEADS):
        lo, hi = h * XATTN_HEAD_DIM, (h + 1) * XATTN_HEAD_DIM
        ok_ref[0, :, h, :] = k[:, lo:hi]
        ov_ref[0, :, h, :] = v[:, lo:hi]
        qk = lax.dot_general(wq_bf16[:, lo:hi], k[:, lo:hi].astype(BF16), contract_last,
                             preferred_element_type=F32)
        oqk_ref[0, :, h * MEM_LEN:(h + 1) * MEM_LEN] = (qk * (XATTN_HEAD_DIM ** -0.5)).astype(BF16)
        ovo_ref[0, h * MEM_LEN:(h + 1) * MEM_LEN, :] = _dot(v[:, lo:hi].astype(BF16), wo_bf16[lo:hi, :]).astype(BF16)


def _mem_kv(mem, norm_mem, w_xk, w_xv, w_xq, w_xo, *, cast=()):
    depth = w_xk.shape[0]
    bsz = mem.shape[0]
    g = norm_mem.reshape(depth, 1, D_MODEL)
    per_layer = lambda *shape, **kw: pl.BlockSpec((None,) + shape, lambda l, b: (l,) + (0,) * len(shape), **kw)
    out_blk = lambda *shape: pl.BlockSpec((None, 1) + shape, lambda l, b: (l, b) + (0,) * len(shape))
    weights = [w_xk, w_xv, w_xq, w_xo]
    jobs = [_cast_job(w, layer, depth * bsz, lambda l, b: l * bsz + b) for w, layer in cast]
    outs = pl.pallas_call(
        functools.partial(_mem_kv_kernel, n_cast=len(jobs)),
        grid=(depth, bsz),
        in_specs=[pl.BlockSpec((1, MEM_LEN, D_MODEL), lambda l, b: (b, 0, 0)), per_layer(1, D_MODEL)]
        + [per_layer(D_MODEL, D_MODEL, pipeline_mode=pl.Buffered(1))] * len(weights) + [j[0] for j in jobs],
        out_specs=[out_blk(MEM_LEN, XATTN_HEADS, XATTN_HEAD_DIM), out_blk(MEM_LEN, XATTN_HEADS, XATTN_HEAD_DIM),
                   out_blk(D_MODEL, XATTN_HEADS * MEM_LEN), out_blk(XATTN_HEADS * MEM_LEN, D_MODEL),
                   per_layer(D_MODEL, D_MODEL), per_layer(D_MODEL, D_MODEL)] + [j[1] for j in jobs],
        out_shape=[jax.ShapeDtypeStruct((depth, bsz, MEM_LEN, XATTN_HEADS, XATTN_HEAD_DIM), F32)] * 2
        + [jax.ShapeDtypeStruct((depth, bsz, D_MODEL, XATTN_HEADS * MEM_LEN), BF16),
           jax.ShapeDtypeStruct((depth, bsz, XATTN_HEADS * MEM_LEN, D_MODEL), BF16)]
        + [jax.ShapeDtypeStruct((depth, D_MODEL, D_MODEL), BF16)] * 2 + [j[2] for j in jobs],
        scratch_shapes=[pltpu.VMEM((D_MODEL, D_MODEL), BF16)] * 2,
        compiler_params=pltpu.CompilerParams(dimension_semantics=("arbitrary", "arbitrary"),
                                             vmem_limit_bytes=VMEM_LIMIT),
        name="mem_kv",
    )(mem, g, *weights, *[w for w, _ in cast])
    return (*outs[:MEM_KV_OUTPUTS], list(outs[MEM_KV_OUTPUTS:]))


def _xattn_prompt_kernel(x_ref, g_ref, qk_ref, vo_ref, o_ref):
    x = x_ref[0]
    h = _rms(x, g_ref[...]).astype(BF16)
    s = _dot(h, qk_ref[0])
    probs = []
    for hh in range(XATTN_HEADS):
        sh = s[:, hh * MEM_LEN:(hh + 1) * MEM_LEN]
        e = jnp.exp(sh - jnp.max(sh, axis=-1, keepdims=True))
        probs.append((e * (1.0 / jnp.sum(e, axis=-1, keepdims=True))).astype(BF16))
    o_ref[0] = x + _dot(jnp.concatenate(probs, axis=1), vo_ref[0])


def _xattn_prompt(x, qk, vo, layer, lw, *, tt=1024):
    bsz, seq, _ = x.shape
    g = lw['norm_xattn']
    folded = lambda a: pl.BlockSpec((None, 1) + a.shape[2:], lambda b, t: (layer, b, 0, 0))
    return pl.pallas_call(
        _xattn_prompt_kernel,
        grid=(bsz, seq // tt),
        in_specs=[pl.BlockSpec((1, tt, D_MODEL), lambda b, t: (b, t, 0)), _param_spec(g, layer),
                  folded(qk), folded(vo)],
        out_specs=pl.BlockSpec((1, tt, D_MODEL), lambda b, t: (b, t, 0)),
        out_shape=jax.ShapeDtypeStruct(x.shape, F32),
        compiler_params=pltpu.CompilerParams(dimension_semantics=("arbitrary", "arbitrary"),
                                             vmem_limit_bytes=VMEM_LIMIT),
        name="xattn_prompt",
    )(x, g, qk, vo)


def _split_head_dim(a):
    lead = a.shape[:-2]
    a = a.reshape(*lead, XATTN_HEADS, 2, XATTN_HEAD_DIM // 2)
    return jnp.swapaxes(a, -3, -2).reshape(*lead, 2 * XATTN_HEADS, XATTN_HEAD_DIM // 2)


def _merge_head_dim(a):
    lead = a.shape[:-2]
    a = a.reshape(*lead, 2, XATTN_HEADS, XATTN_HEAD_DIM // 2)
    return jnp.swapaxes(a, -3, -2).reshape(*lead, XATTN_HEADS, XATTN_HEAD_DIM)


def _attend_rows(q_ref, k_ref, v_ref, o_ref, rows):
    for r in rows:
        part = jnp.sum(k_ref[r] * q_ref[r][None], axis=-1, keepdims=True)
        s = part + pltpu.roll(part, XATTN_HEADS, axis=1)
        e = jnp.exp(s - jnp.max(s, axis=0, keepdims=True))
        o_ref[r] = jnp.sum(e * v_ref[r], axis=0) * (1.0 / jnp.sum(e, axis=0))


def _ffn_rows(x, g_ref, w1_ref, w2_ref, gf_ref, *, final_norm, between_chunks=None):
    h = _rms(x, g_ref[...]).astype(BF16)
    y = x
    for i, c in enumerate(range(0, D_FF, FF_CHUNK)):
        if between_chunks is not None:
            between_chunks(i)
        a = jnp.maximum(_dot(h, w1_ref[:, c:c + FF_CHUNK]), 0.0)
        y = y + _dot((a * a).astype(BF16), w2_ref[c:c + FF_CHUNK, :])
    return _rms(y, gf_ref[...]) if final_norm else y


def _ffn_kernel(x_ref, g_ref, w1_ref, w2_ref, gf_ref, o_ref, *, final_norm, between_chunks=None):
    o_ref[...] = _ffn_rows(x_ref[...], g_ref, w1_ref, w2_ref, gf_ref, final_norm=final_norm,
                           between_chunks=between_chunks)


def _oproj_ffn_kernel(x_ref, a_ref, wo_ref, g_ref, w1_ref, w2_ref, gf_ref, o_ref, h_buf, *, final_norm, n_chunks):
    c = pl.program_id(0)

    @pl.when(c == 0)
    def _():
        x = x_ref[...] + _dot(a_ref[...].astype(BF16), wo_ref[...])
        o_ref[...] = x
        h_buf[...] = _rms(x, g_ref[...]).astype(BF16)

    a = jnp.maximum(_dot(h_buf[...], w1_ref[...]), 0.0)
    o_ref[...] += _dot((a * a).astype(BF16), w2_ref[...])
    if final_norm:
        @pl.when(c == n_chunks - 1)
        def _():
            o_ref[...] = _rms(o_ref[...], gf_ref[...])


def _oproj_ffn(x, attn, layer, lw, norm_final, *, final_norm, chunk=1024):
    n_chunks = D_FF // chunk
    g, gf = lw['norm_ffn'], norm_final.reshape(1, -1)
    return pl.pallas_call(
        functools.partial(_oproj_ffn_kernel, final_norm=final_norm, n_chunks=n_chunks),
        grid=(n_chunks,),
        in_specs=[_const_spec(x.shape), _const_spec(attn.shape), _layer_spec(lw['w_xo_layers'], layer),
                  _param_spec(g, layer), pl.BlockSpec((D_MODEL, chunk), lambda c: (0, c)),
                  pl.BlockSpec((chunk, D_MODEL), lambda c: (c, 0)), _const_spec(gf.shape)],
        out_specs=pl.BlockSpec(x.shape, lambda c: (0, 0)),
        out_shape=jax.ShapeDtypeStruct(x.shape, F32),
        scratch_shapes=[pltpu.VMEM(x.shape, BF16)],
        compiler_params=pltpu.CompilerParams(dimension_semantics=("arbitrary",), vmem_limit_bytes=VMEM_LIMIT),
        name="oproj_ffn",
    )(x, attn, lw['w_xo_layers'], g, lw['w_ff1'], lw['w_ff2'], gf)


FFN_ATTEND_INPUTS = 8
FFN_ATTEND_OUTPUTS = 2


def _ffn_attend_kernel(*refs, final_norm, rows, n_cast):
    x_ref, g_ref, w1_ref, w2_ref, gf_ref, q_ref, k_ref, v_ref = refs[:FFN_ATTEND_INPUTS]
    cast_src = refs[FFN_ATTEND_INPUTS:FFN_ATTEND_INPUTS + n_cast]
    o_ref, oa_ref = refs[FFN_ATTEND_INPUTS + n_cast:FFN_ATTEND_INPUTS + n_cast + FFN_ATTEND_OUTPUTS]
    cast_dst = refs[FFN_ATTEND_INPUTS + n_cast + FFN_ATTEND_OUTPUTS:]
    _run_cast_jobs(cast_src, cast_dst)
    n_chunks = D_FF // FF_CHUNK
    assert rows % n_chunks == 0
    per_chunk = rows // n_chunks
    attend = lambda i: _attend_rows(q_ref, k_ref, v_ref, oa_ref, range(i * per_chunk, (i + 1) * per_chunk))
    _ffn_kernel(x_ref, g_ref, w1_ref, w2_ref, gf_ref, o_ref, final_norm=final_norm, between_chunks=attend)


def _ffn_attend(x2d, lw, norm_final, attend, *, final_norm, tm=512, cast=()):
    n = x2d.shape[0]
    q, cache_k, cache_v, layer = attend
    g = lw['norm_ffn']
    gf = norm_final.reshape(1, -1)
    in_specs = [pl.BlockSpec((tm, D_MODEL), lambda i: (i, 0)), _param_spec(g, layer),
                _const_spec(lw['w_ff1'].shape), _const_spec(lw['w_ff2'].shape), _const_spec(gf.shape)]
    out_spec = pl.BlockSpec((tm, D_MODEL), lambda i: (i, 0))
    out_shape = jax.ShapeDtypeStruct(x2d.shape, F32)
    params = pltpu.CompilerParams(dimension_semantics=("arbitrary",), vmem_limit_bytes=VMEM_LIMIT)
    rows = q.shape[0] // (n // tm)
    assert rows * (n // tm) == q.shape[0]
    q_spec = pl.BlockSpec((rows,) + q.shape[1:], lambda i: (i, 0, 0))
    kv_spec = pl.BlockSpec((None, rows) + cache_k.shape[2:], lambda i: (layer, i, 0, 0, 0))
    jobs = [_cast_job(w, wl, n // tm, lambda i: i) for w, wl in cast]
    outs = pl.pallas_call(
        functools.partial(_ffn_attend_kernel, final_norm=final_norm, rows=rows, n_cast=len(jobs)), grid=(n // tm,),
        in_specs=in_specs + [q_spec, kv_spec, kv_spec] + [j[0] for j in jobs],
        out_specs=[out_spec, q_spec] + [j[1] for j in jobs],
        out_shape=[out_shape, jax.ShapeDtypeStruct(q.shape, F32)] + [j[2] for j in jobs],
        compiler_params=params, name="ffn_attend",
    )(x2d, g, lw['w_ff1'], lw['w_ff2'], gf, q, cache_k, cache_v, *[w for w, _ in cast])
    return outs[0], outs[1], list(outs[FFN_ATTEND_OUTPUTS:])


PROJ_WEIGHTS = ('w_in', 'w_out')


def _small_params(p):
    depth = p['norm_mix'].shape[0]
    n_pool = len(POOL_WINDOWS)
    tril = jnp.tril(jnp.ones((CHUNK, CHUNK), dtype=bool))
    ws = jnp.where(tril, p['gmlp_ws'], 0.0)
    rows = lambda a: a.reshape(depth, 1, -1)
    sp = {k: rows(p[k]) for k in ('norm_mix', 'gmlp_ln_g', 'gmlp_ln_b', 'conf_dw_b', 'conf_ln_g', 'conf_ln_b',
                                  'pool_scale', 'mix_out_g', 'norm_xattn', 'norm_ffn')}
    sp.update(
        conf_dw=p['conf_dw'], sc_dw=p['sc_dw'],
        gmlp_wcat=jnp.concatenate([ws[:, h] for h in range(GMLP_HEADS)], axis=2).astype(BF16),
        gmlp_bias_tile=jnp.repeat(jnp.swapaxes(p['gmlp_bs'], 1, 2), GMLP_HEAD_DIM, axis=2),
        gmlp_w00=rows(jnp.repeat(ws[:, :, 0, 0], GMLP_HEAD_DIM, axis=1)),
        gmlp_b0=rows(jnp.repeat(p['gmlp_bs'][:, :, 0], GMLP_HEAD_DIM, axis=1)),
        pool_w_bd=jnp.einsum('lgij,gh->lgihj', p['pool_w'], jnp.eye(n_pool, dtype=F32)
                             ).reshape(depth, D_GROUP, D_GROUP).astype(BF16))
    return sp


def kernel(x_prompt, x_sample, mem_prompt, cache_mem_k, cache_mem_v, state_conv_glu, state_conv_short, state_pool, norm_mix, w_in, gmlp_ln_g, gmlp_ln_b, gmlp_ws, gmlp_bs, conf_dw, conf_dw_b, conf_ln_g, conf_ln_b, sc_dw, pool_w, pool_scale, mix_out_g, w_out, norm_xattn, norm_mem, w_xq, w_xk, w_xv, w_xo, norm_ffn, w_ff1, w_ff2, norm_final):
    params = dict(norm_mix=norm_mix, w_in=w_in, gmlp_ln_g=gmlp_ln_g, gmlp_ln_b=gmlp_ln_b, gmlp_ws=gmlp_ws,
                  gmlp_bs=gmlp_bs, conf_dw=conf_dw, conf_dw_b=conf_dw_b, conf_ln_g=conf_ln_g, conf_ln_b=conf_ln_b,
                  sc_dw=sc_dw, pool_w=pool_w, pool_scale=pool_scale, mix_out_g=mix_out_g, w_out=w_out,
                  norm_xattn=norm_xattn, norm_mem=norm_mem, w_xq=w_xq, w_xk=w_xk, w_xv=w_xv, w_xo=w_xo,
                  norm_ffn=norm_ffn, w_ff1=w_ff1, w_ff2=w_ff2)
    depth = w_in.shape[0]
    bsz, seq, _ = x_prompt.shape
    n_s = x_sample.shape[0]
    xp = x_prompt
    xs = x_sample.reshape(n_s, D_MODEL)
    outs = {k: [] for k in ('glu_p', 'sh_p', 'pl_p', 'v_s')}
    hist_s = ()
    mem_k, mem_v, mem_qk, mem_vo, wq_bf16, wo_bf16, proj = _mem_kv(
        mem_prompt, norm_mem, w_xk, w_xv, w_xq, w_xo, cast=[(params[k], 0) for k in PROJ_WEIGHTS])
    cache_k_split = _split_head_dim(cache_mem_k)
    cache_v_split = _split_head_dim(cache_mem_v)
    small = _small_params(params)
    for l in range(depth):
        lw = dict(small, **dict(zip(PROJ_WEIGHTS, proj)), w_xq_layers=wq_bf16, w_xo_layers=wo_bf16)
        last = l == depth - 1
        xs, q_s, vn_s, *hist_s = _mixer_sample(xs, state_conv_glu, state_conv_short, state_pool, l, lw, hist_s)
        q_s = _split_head_dim(q_s.reshape(n_s, XATTN_HEADS, XATTN_HEAD_DIM))
        xp, glu_p, sh_p, pool_p, (lw['w_ff1'], lw['w_ff2']) = _mixer_prompt(
            xp, l, lw, cast=[(params['w_ff1'], l), (params['w_ff2'], l)])
        xp = _xattn_prompt(xp, mem_qk, mem_vo, l, lw)
        outs['glu_p'].append(glu_p); outs['sh_p'].append(sh_p); outs['pl_p'].append(pool_p)
        xp, o_s, proj = _ffn_attend(xp.reshape(bsz * seq, D_MODEL), lw, norm_final,
                                    (q_s, cache_k_split, cache_v_split, l), final_norm=last,
                                    cast=[] if last else [(params[k], l + 1) for k in PROJ_WEIGHTS])
        xp = xp.reshape(bsz, seq, D_MODEL)
        xs = _oproj_ffn(xs, _merge_head_dim(o_s).reshape(n_s, D_MODEL), l, lw, norm_final, final_norm=last)
        outs['v_s'].append(vn_s.reshape(n_s, 1, D_GROUP))
    st = lambda k: jnp.stack(outs[k], axis=0)
    glu_s, sh_s, pool_s = hist_s
    return (xp, xs.reshape(n_s, 1, D_MODEL), mem_k, mem_v, st('glu_p'), glu_s, st('sh_p'), sh_s,
            st('pl_p'), pool_s, st('v_s'))
```

```python
import functools

import jax
import jax.numpy as jnp
from jax import lax
from jax.experimental import pallas as pl
from jax.experimental.pallas import tpu as pltpu

F32 = jnp.float32
BF16 = jnp.bfloat16

D_MODEL = 1024
D_GROUP = 256
N_GROUPS = 4
D_IN_PROJ = 8 * D_GROUP
GMLP_HEADS = 4
GMLP_HEAD_DIM = D_GROUP // GMLP_HEADS
CHUNK = 128
CONF_WIDTH = 31
SC_WIDTH = 3
POOL_WINDOWS = (2, 4, 8, 16)
POOL_GROUP_DIM = D_GROUP // len(POOL_WINDOWS)
POOL_BUF = max(POOL_WINDOWS) - 1
MEM_LEN = 256
XATTN_HEADS = 4
XATTN_HEAD_DIM = D_MODEL // XATTN_HEADS
D_FF = 4 * D_MODEL
PAST_LEN = 16384
EPS = 1e-6

GLU_HALO = 32
SC_HALO = 8
POOL_HALO = 32
LANES = 128
SUBLANES = 8
BF16_SUBLANES = 16
LANE_SLABS = D_GROUP // LANES
CONV_STRIDE = 4
CONV_ROWS = 256
FF_CHUNK = 4096
VMEM_LIMIT = 56 * 1024 * 1024


def _rms(x, g):
    return x * lax.rsqrt(jnp.mean(x * x, axis=-1, keepdims=True) + EPS) * g


def _layer_norm(x, g, b):
    xc = x - jnp.mean(x, axis=-1, keepdims=True)
    return xc * lax.rsqrt(jnp.mean(xc * xc, axis=-1, keepdims=True) + EPS) * g + b


def _sigmoid(x):
    return 0.5 * jnp.tanh(0.5 * x) + 0.5


def _dot(a, b):
    return jnp.dot(a, b, preferred_element_type=F32)


def _const_spec(shape):
    zeros = (0,) * len(shape)
    return pl.BlockSpec(shape, lambda *_: zeros, pipeline_mode=pl.Buffered(1))


def _layer_spec(stacked, layer):
    zeros = (0,) * (stacked.ndim - 1)
    return pl.BlockSpec((None,) + stacked.shape[1:], lambda *_: (layer,) + zeros, pipeline_mode=pl.Buffered(1))


def _cast_job(stacked, layer, steps, step_of):
    _, r, c = stacked.shape
    blk = r // steps
    assert blk * steps == r and blk % BF16_SUBLANES == 0
    return (pl.BlockSpec((None, blk, c), lambda *ids: (layer, step_of(*ids), 0)),
            pl.BlockSpec((blk, c), lambda *ids: (step_of(*ids), 0)),
            jax.ShapeDtypeStruct((r, c), BF16))


def _run_cast_jobs(src_refs, dst_refs):
    for src, dst in zip(src_refs, dst_refs):
        dst[...] = src[...].astype(BF16)


def _pool_window_lanes(shape):
    grp = lax.broadcasted_iota(jnp.int32, shape, len(shape) - 1) // POOL_GROUP_DIM
    win = jnp.full(shape, float(POOL_WINDOWS[-1]), F32)
    for g in range(len(POOL_WINDOWS) - 2, -1, -1):
        win = jnp.where(grp == g, float(POOL_WINDOWS[g]), win)
    return grp, win


def _select_by_group(grp, vals):
    out = vals[-1]
    for g in range(len(vals) - 2, -1, -1):
        out = jnp.where(grp == g, vals[g], out)
    return out


def _mix_out(x, y_groups, mog_ref, wout_ref):
    yn = [_rms(y, mog_ref[:, g * D_GROUP:(g + 1) * D_GROUP]).astype(BF16) for g, y in enumerate(y_groups)]
    return x + _dot(jnp.concatenate(yn, axis=1), wout_ref[...])


MIXER_INPUTS = 16
MIXER_OUTPUTS = 4


def _mixer_prompt_kernel(*refs, tt, n_t, n_cast):
    (x_ref, g_ref, win_ref, lng_ref, lnb_ref, wcat_ref, bst_ref, cw_ref, cb_ref, clg_ref,
     clb_ref, sw_ref, pw_ref, ps_ref, mog_ref, wout_ref) = refs[:MIXER_INPUTS]
    cast_src = refs[MIXER_INPUTS:MIXER_INPUTS + n_cast]
    outs = refs[MIXER_INPUTS + n_cast:]
    o_ref, oglu_ref, osh_ref, opool_ref = outs[:MIXER_OUTPUTS]
    cast_dst = outs[MIXER_OUTPUTS:MIXER_OUTPUTS + n_cast]
    glu_buf0, glu_buf1, yb_buf0, yb_buf1, sc_buf, p_buf, s2_buf, s4_buf, s8_buf = outs[MIXER_OUTPUTS + n_cast:]
    _run_cast_jobs(cast_src, cast_dst)
    t = pl.program_id(1)
    glu_buf = (glu_buf0, glu_buf1)
    yb_buf = (yb_buf0, yb_buf1)

    @pl.when(t == 0)
    def _():
        for s in range(LANE_SLABS):
            glu_buf[s][0:GLU_HALO, :] = jnp.zeros((GLU_HALO, LANES), F32)
        sc_buf[0:SC_HALO, :] = jnp.zeros((SC_HALO, D_GROUP), F32)
        p_buf[0:POOL_HALO, :] = jnp.zeros((POOL_HALO, D_GROUP), F32)

    _mix_tile(t, x_ref, g_ref, win_ref, lng_ref, lnb_ref, wcat_ref, bst_ref, cw_ref, cb_ref, clg_ref,
              clb_ref, sw_ref, pw_ref, ps_ref, mog_ref, wout_ref, o_ref,
              glu_buf, yb_buf, sc_buf, p_buf, s2_buf, s4_buf, s8_buf, tt=tt)

    @pl.when(t == n_t - 1)
    def _():
        oglu_ref[0] = jnp.concatenate(
            [glu_buf[s][GLU_HALO + tt - (CONF_WIDTH - 1):GLU_HALO + tt, :] for s in range(LANE_SLABS)], axis=1)
        osh_ref[0] = sc_buf[SC_HALO + tt - (SC_WIDTH - 1):SC_HALO + tt, :]
        opool_ref[0] = p_buf[POOL_HALO + tt - POOL_BUF:POOL_HALO + tt, :]

    for s in range(LANE_SLABS):
        glu_buf[s][0:GLU_HALO, :] = glu_buf[s][tt:tt + GLU_HALO, :]
    sc_buf[0:SC_HALO, :] = sc_buf[tt:tt + SC_HALO, :]
    p_buf[0:POOL_HALO, :] = p_buf[tt:tt + POOL_HALO, :]


def _mix_tile(t, x_ref, g_ref, win_ref, lng_ref, lnb_ref, wcat_ref, bst_ref, cw_ref, cb_ref, clg_ref,
              clb_ref, sw_ref, pw_ref, ps_ref, mog_ref, wout_ref, o_ref,
              glu_buf, yb_buf, sc_buf, p_buf, s2_buf, s4_buf, s8_buf, *, tt):
    x = x_ref[0]
    z = _dot(_rms(x, g_ref[...]).astype(BF16), win_ref[...])
    zcol = lambda i: z[:, i * D_GROUP:(i + 1) * D_GROUP]
    group_norm = lambda g, y: _rms(y, mog_ref[:, g * D_GROUP:(g + 1) * D_GROUP]).astype(BF16)
    yn = [None] * N_GROUPS

    vn = _layer_norm(zcol(1), lng_ref[...], lnb_ref[...])
    head = lax.broadcasted_iota(jnp.int32, (CHUNK, D_GROUP), 1) // GMLP_HEAD_DIM
    gates = []
    for c in range(tt // CHUNK):
        vc = vn[c * CHUNK:(c + 1) * CHUNK]
        stack = jnp.concatenate([jnp.where(head == hh, vc, 0.0) for hh in range(GMLP_HEADS)], axis=0)
        gates.append(_dot(wcat_ref[...], stack.astype(BF16)) + bst_ref[...])
    yn[0] = group_norm(0, zcol(0) * jnp.concatenate(gates, axis=0))

    first = GLU_HALO - (CONF_WIDTH - 1)
    span = SUBLANES * CONV_STRIDE
    glu = zcol(2) * _sigmoid(zcol(3))
    for s in range(LANE_SLABS):
        glu_buf[s][GLU_HALO:GLU_HALO + tt, :] = glu[:, s * LANES:(s + 1) * LANES]
    for c0 in range(0, tt, CONV_ROWS):
        starts = [c0 + (i // CONV_STRIDE) * span + i % CONV_STRIDE for i in range(CONV_ROWS // SUBLANES)]
        accs = [[None] * len(starts) for _ in range(LANE_SLABS)]
        for k in range(CONF_WIDTH):
            for s in range(LANE_SLABS):
                wk = jnp.broadcast_to(cw_ref[k:k + 1, s * LANES:(s + 1) * LANES], (SUBLANES, LANES))
                for i, t0 in enumerate(starts):
                    term = wk * glu_buf[s][pl.ds(first + t0 + k, SUBLANES, stride=CONV_STRIDE), :]
                    accs[s][i] = term if k == 0 else accs[s][i] + term
        conv = jnp.concatenate([jnp.concatenate(a, axis=0) for a in accs], axis=1)
        ln = _layer_norm(conv + cb_ref[...], clg_ref[...], clb_ref[...])
        yb = ln * _sigmoid(ln)
        for s in range(LANE_SLABS):
            for i, t0 in enumerate(starts):
                yb_buf[s][pl.ds(t0, SUBLANES, stride=CONV_STRIDE), :] = (
                    yb[i * SUBLANES:(i + 1) * SUBLANES, s * LANES:(s + 1) * LANES])
    yn[1] = group_norm(1, jnp.concatenate([yb_buf[s][...] for s in range(LANE_SLABS)], axis=1))

    sxc = zcol(5) * zcol(6)
    sc_buf[SC_HALO:SC_HALO + tt, :] = sxc
    conv_c = (sw_ref[0:1, :] * sc_buf[SC_HALO - 2:SC_HALO - 2 + tt, :]
              + sw_ref[1:2, :] * sc_buf[SC_HALO - 1:SC_HALO - 1 + tt, :]
              + sw_ref[2:3, :] * sxc)
    yn[2] = group_norm(2, zcol(4) * conv_c)

    pool_x = zcol(7)
    lo, end = POOL_HALO, POOL_HALO + tt
    p_buf[lo:end, :] = pool_x
    lo2, lo4, lo8 = lo - 3 * SUBLANES, lo - 2 * SUBLANES, lo - SUBLANES
    s2_buf[lo2:end, :] = p_buf[lo2:end, :] + p_buf[lo2 - 1:end - 1, :]
    s4_buf[lo4:end, :] = s2_buf[lo4:end, :] + s2_buf[lo4 - 2:end - 2, :]
    s8_buf[lo8:end, :] = s4_buf[lo8:end, :] + s4_buf[lo8 - 4:end - 4, :]
    s16 = s8_buf[lo:end, :] + s8_buf[lo - 8:end - 8, :]
    grp = _pool_window_lanes((tt, D_GROUP))[0]
    ssum = _select_by_group(grp, [s2_buf[lo:end, :], s4_buf[lo:end, :], s8_buf[lo:end, :], s16])
    head_rows = POOL_BUF + 1
    pos = t * tt + lax.broadcasted_iota(jnp.int32, (head_rows, D_GROUP), 0)
    cnt = jnp.minimum(_pool_window_lanes((head_rows, D_GROUP))[1], (pos + 1).astype(F32))
    inv_win = _select_by_group(_pool_window_lanes((tt - head_rows, D_GROUP))[0], [1.0 / w for w in POOL_WINDOWS])
    mean = jnp.concatenate([ssum[:head_rows] / cnt, ssum[head_rows:] * inv_win], axis=0)
    pooled = mean - pool_x
    yn[3] = group_norm(3, _dot(pooled.astype(BF16), pw_ref[...]) * ps_ref[...])

    o_ref[0] = x + _dot(jnp.concatenate(yn, axis=1), wout_ref[...])


def _mixer_prompt(x, lw, *, tt=1024, cast=()):
    bsz, seq, _ = x.shape
    n_t = seq // tt
    step_of = lambda b, t: b * n_t + t
    row = lambda a: a.reshape(1, -1)
    small = [row(lw['norm_mix']), lw['w_in'], row(lw['gmlp_ln_g']), row(lw['gmlp_ln_b']), lw['gmlp_wcat'],
             lw['gmlp_bias_tile'], lw['conf_dw'], row(lw['conf_dw_b']), row(lw['conf_ln_g']), row(lw['conf_ln_b']),
             lw['sc_dw'], lw['pool_w_bd'], row(lw['pool_scale']), row(lw['mix_out_g']), lw['w_out']]
    assert 1 + len(small) == MIXER_INPUTS
    jobs = [_cast_job(w, layer, bsz * n_t, step_of) for w, layer in cast]
    state_spec = lambda rows: pl.BlockSpec((1, rows, D_GROUP), lambda b, t: (b, 0, 0))
    outs = pl.pallas_call(
        functools.partial(_mixer_prompt_kernel, tt=tt, n_t=n_t, n_cast=len(jobs)),
        grid=(bsz, n_t),
        in_specs=[pl.BlockSpec((1, tt, D_MODEL), lambda b, t: (b, t, 0))] + [_const_spec(a.shape) for a in small]
        + [j[0] for j in jobs],
        out_specs=[pl.BlockSpec((1, tt, D_MODEL), lambda b, t: (b, t, 0)),
                   state_spec(CONF_WIDTH - 1), state_spec(SC_WIDTH - 1), state_spec(POOL_BUF)] + [j[1] for j in jobs],
        out_shape=[jax.ShapeDtypeStruct(x.shape, F32),
                   jax.ShapeDtypeStruct((bsz, CONF_WIDTH - 1, D_GROUP), F32),
                   jax.ShapeDtypeStruct((bsz, SC_WIDTH - 1, D_GROUP), F32),
                   jax.ShapeDtypeStruct((bsz, POOL_BUF, D_GROUP), F32)] + [j[2] for j in jobs],
        scratch_shapes=[pltpu.VMEM((GLU_HALO + tt, LANES), F32)] * LANE_SLABS + [pltpu.VMEM((tt, LANES), F32)] * LANE_SLABS
        + [pltpu.VMEM((SC_HALO + tt, D_GROUP), F32)]
        + [pltpu.VMEM((POOL_HALO + tt, D_GROUP), F32)] * 4,
        compiler_params=pltpu.CompilerParams(dimension_semantics=("arbitrary", "arbitrary"),
                                             vmem_limit_bytes=VMEM_LIMIT),
        name="mixer_prompt",
    )(x, *small, *[w for w, _ in cast])
    return (*outs[:MIXER_OUTPUTS], list(outs[MIXER_OUTPUTS:]))


SAMPLE_MIXER_INPUTS = 21


def _mixer_sample_kernel(*refs, layer):
    (x_ref, stg_ref, sts_ref, stp_ref, g_ref, win_ref, lng_ref, lnb_ref, w00_ref, b0_ref, cw_ref, cb_ref, clg_ref,
     clb_ref, sw_ref, pw_ref, ps_ref, mog_ref, wout_ref, gx_ref, wq_ref) = refs[:SAMPLE_MIXER_INPUTS]
    n_earlier = 3 if layer else 0
    earlier = refs[SAMPLE_MIXER_INPUTS:SAMPLE_MIXER_INPUTS + n_earlier]
    o_ref, oq_ref, ovn_ref, *new_hist = refs[SAMPLE_MIXER_INPUTS + n_earlier:]
    for src, dst in zip(earlier, new_hist):
        dst[0:layer] = src[...]
    oglu_ref, osh_ref, opool_ref = [ref.at[layer] for ref in new_hist]
    x = x_ref[...]
    h = _rms(x, g_ref[...]).astype(BF16)
    z = _dot(h, win_ref[...])
    u, v, glu_a, glu_g, sc_b, sc_c, sc_x, pool_x = [z[:, i * D_GROUP:(i + 1) * D_GROUP]
                                                    for i in range(D_IN_PROJ // D_GROUP)]
    hist = lambda ref, k: ref[:, k, :]

    def push(new_ref, old_ref, row):
        n_hist = old_ref.shape[1]
        new_ref[:, 0:n_hist - 1, :] = old_ref[:, 1:n_hist, :]
        new_ref[:, n_hist - 1, :] = row

    vn = _layer_norm(v, lng_ref[...], lnb_ref[...])
    ovn_ref[...] = vn
    y_a = u * (w00_ref[...] * vn + b0_ref[...])

    glu = glu_a * _sigmoid(glu_g)
    n_hist = CONF_WIDTH - 1
    acc = cw_ref[n_hist:n_hist + 1, :] * glu
    for k in range(n_hist):
        acc = acc + cw_ref[k:k + 1, :] * hist(stg_ref, k)
    ln = _layer_norm(acc + cb_ref[...], clg_ref[...], clb_ref[...])
    y_b = ln * _sigmoid(ln)
    push(oglu_ref, stg_ref, glu)

    sxc = sc_c * sc_x
    y_c = sc_b * (sw_ref[0:1, :] * hist(sts_ref, 0) + sw_ref[1:2, :] * hist(sts_ref, 1) + sw_ref[2:3, :] * sxc)
    push(osh_ref, sts_ref, sxc)

    run = pool_x
    sums = []
    back = 0
    for w in POOL_WINDOWS:
        while back < w - 1:
            run = run + hist(stp_ref, POOL_BUF - 1 - back)
            back += 1
        sums.append(run)
    grp, win = _pool_window_lanes(pool_x.shape)
    cnt = jnp.minimum(win, float(PAST_LEN + 1))
    pooled = _select_by_group(grp, sums) / cnt - pool_x
    y_d = _dot(pooled.astype(BF16), pw_ref[...]) * ps_ref[...]
    push(opool_ref, stp_ref, pool_x)

    x_mixed = _mix_out(x, [y_a, y_b, y_c, y_d], mog_ref, wout_ref)
    o_ref[...] = x_mixed
    oq_ref[...] = _dot(_rms(x_mixed, gx_ref[...]).astype(BF16), wq_ref[...]) * (XATTN_HEAD_DIM ** -0.5)


def _mixer_sample(xs, st_glu, st_sh, st_pool, layer, lw, earlier=()):
    n = xs.shape[0]
    row = lambda a: a.reshape(1, -1)
    states = [st_glu, st_sh, st_pool]
    earlier = list(earlier)
    assert len(earlier) == (len(states) if layer else 0)
    consts = [row(lw['norm_mix']), lw['w_in'], row(lw['gmlp_ln_g']), row(lw['gmlp_ln_b']), row(lw['gmlp_w00']),
              row(lw['gmlp_b0']), lw['conf_dw'], row(lw['conf_dw_b']), row(lw['conf_ln_g']), row(lw['conf_ln_b']),
              lw['sc_dw'], lw['pool_w_bd'], row(lw['pool_scale']), row(lw['mix_out_g']), lw['w_out'],
              row(lw['norm_xattn'])]
    assert 1 + len(states) + len(consts) + 1 == SAMPLE_MIXER_INPUTS
    out_shapes = [(n, D_MODEL), (n, D_MODEL), (n, D_GROUP)] + [(layer + 1,) + s.shape[1:] for s in states]
    return pl.pallas_call(
        functools.partial(_mixer_sample_kernel, layer=layer),
        grid=(1,),
        in_specs=[_const_spec(xs.shape)] + [_layer_spec(s, layer) for s in states]
        + [_const_spec(a.shape) for a in consts] + [_layer_spec(lw['w_xq_layers'], layer)]
        + [_const_spec(a.shape) for a in earlier],
        out_specs=[pl.BlockSpec(s, lambda i, nd=len(s): (0,) * nd) for s in out_shapes],
        out_shape=[jax.ShapeDtypeStruct(s, F32) for s in out_shapes],
        compiler_params=pltpu.CompilerParams(dimension_semantics=("arbitrary",), vmem_limit_bytes=VMEM_LIMIT),
        name="mixer_sample",
    )(xs, *states, *consts, lw['w_xq_layers'], *earlier)


MEM_KV_INPUTS = 6
MEM_KV_OUTPUTS = 6


def _mem_kv_kernel(*refs, n_cast):
    m_ref, g_ref, wk_ref, wv_ref, wq_ref, wo_ref = refs[:MEM_KV_INPUTS]
    cast_src = refs[MEM_KV_INPUTS:MEM_KV_INPUTS + n_cast]
    outs = refs[MEM_KV_INPUTS + n_cast:]
    ok_ref, ov_ref, oqk_ref, ovo_ref, wq_bf16, wo_bf16 = outs[:MEM_KV_OUTPUTS]
    cast_dst = outs[MEM_KV_OUTPUTS:MEM_KV_OUTPUTS + n_cast]
    wk_bf16, wv_bf16 = outs[MEM_KV_OUTPUTS + n_cast:]
    _run_cast_jobs(cast_src, cast_dst)

    @pl.when(pl.program_id(1) == 0)
    def _():
        for src, dst in ((wk_ref, wk_bf16), (wv_ref, wv_bf16), (wq_ref, wq_bf16), (wo_ref, wo_bf16)):
            dst[...] = src[...].astype(BF16)

    m = _rms(m_ref[0], g_ref[...]).astype(BF16)
    k = _dot(m, wk_bf16[...])
    v = _dot(m, wv_bf16[...])
    contract_last = (((1,), (1,)), ((), ()))
    for h in range(XATTN_HEADS):
        lo, hi = h * XATTN_HEAD_DIM, (h + 1) * XATTN_HEAD_DIM
        ok_ref[0, :, h, :] = k[:, lo:hi]
        ov_ref[0, :, h, :] = v[:, lo:hi]
        qk = lax.dot_general(wq_bf16[:, lo:hi], k[:, lo:hi].astype(BF16), contract_last,
                             preferred_element_type=F32)
        oqk_ref[0, :, h * MEM_LEN:(h + 1) * MEM_LEN] = (qk * (XATTN_HEAD_DIM ** -0.5)).astype(BF16)
        ovo_ref[0, h * MEM_LEN:(h + 1) * MEM_LEN, :] = _dot(v[:, lo:hi].astype(BF16), wo_bf16[lo:hi, :]).astype(BF16)


def _mem_kv(mem, norm_mem, w_xk, w_xv, w_xq, w_xo, *, cast=()):
    depth = w_xk.shape[0]
    bsz = mem.shape[0]
    g = norm_mem.reshape(depth, 1, D_MODEL)
    per_layer = lambda *shape, **kw: pl.BlockSpec((None,) + shape, lambda l, b: (l,) + (0,) * len(shape), **kw)
    out_blk = lambda *shape: pl.BlockSpec((None, 1) + shape, lambda l, b: (l, b) + (0,) * len(shape))
    weights = [w_xk, w_xv, w_xq, w_xo]
    jobs = [_cast_job(w, layer, depth * bsz, lambda l, b: l * bsz + b) for w, layer in cast]
    outs = pl.pallas_call(
        functools.partial(_mem_kv_kernel, n_cast=len(jobs)),
        grid=(depth, bsz),
        in_specs=[pl.BlockSpec((1, MEM_LEN, D_MODEL), lambda l, b: (b, 0, 0)), per_layer(1, D_MODEL)]
        + [per_layer(D_MODEL, D_MODEL, pipeline_mode=pl.Buffered(1))] * len(weights) + [j[0] for j in jobs],
        out_specs=[out_blk(MEM_LEN, XATTN_HEADS, XATTN_HEAD_DIM), out_blk(MEM_LEN, XATTN_HEADS, XATTN_HEAD_DIM),
                   out_blk(D_MODEL, XATTN_HEADS * MEM_LEN), out_blk(XATTN_HEADS * MEM_LEN, D_MODEL),
                   per_layer(D_MODEL, D_MODEL), per_layer(D_MODEL, D_MODEL)] + [j[1] for j in jobs],
        out_shape=[jax.ShapeDtypeStruct((depth, bsz, MEM_LEN, XATTN_HEADS, XATTN_HEAD_DIM), F32)] * 2
        + [jax.ShapeDtypeStruct((depth, bsz, D_MODEL, XATTN_HEADS * MEM_LEN), BF16),
           jax.ShapeDtypeStruct((depth, bsz, XATTN_HEADS * MEM_LEN, D_MODEL), BF16)]
        + [jax.ShapeDtypeStruct((depth, D_MODEL, D_MODEL), BF16)] * 2 + [j[2] for j in jobs],
        scratch_shapes=[pltpu.VMEM((D_MODEL, D_MODEL), BF16)] * 2,
        compiler_params=pltpu.CompilerParams(dimension_semantics=("arbitrary", "arbitrary"),
                                             vmem_limit_bytes=VMEM_LIMIT),
        name="mem_kv",
    )(mem, g, *weights, *[w for w, _ in cast])
    return (*outs[:MEM_KV_OUTPUTS], list(outs[MEM_KV_OUTPUTS:]))


def _xattn_prompt_kernel(x_ref, g_ref, qk_ref, vo_ref, o_ref):
    x = x_ref[0]
    h = _rms(x, g_ref[...]).astype(BF16)
    s = _dot(h, qk_ref[0])
    probs = []
    for hh in range(XATTN_HEADS):
        sh = s[:, hh * MEM_LEN:(hh + 1) * MEM_LEN]
        e = jnp.exp(sh - jnp.max(sh, axis=-1, keepdims=True))
        probs.append((e * (1.0 / jnp.sum(e, axis=-1, keepdims=True))).astype(BF16))
    o_ref[0] = x + _dot(jnp.concatenate(probs, axis=1), vo_ref[0])


def _xattn_prompt(x, qk, vo, layer, lw, *, tt=1024):
    bsz, seq, _ = x.shape
    g = lw['norm_xattn'].reshape(1, -1)
    folded = lambda a: pl.BlockSpec((None, 1) + a.shape[2:], lambda b, t: (layer, b, 0, 0))
    return pl.pallas_call(
        _xattn_prompt_kernel,
        grid=(bsz, seq // tt),
        in_specs=[pl.BlockSpec((1, tt, D_MODEL), lambda b, t: (b, t, 0)), _const_spec(g.shape),
                  folded(qk), folded(vo)],
        out_specs=pl.BlockSpec((1, tt, D_MODEL), lambda b, t: (b, t, 0)),
        out_shape=jax.ShapeDtypeStruct(x.shape, F32),
        compiler_params=pltpu.CompilerParams(dimension_semantics=("arbitrary", "arbitrary"),
                                             vmem_limit_bytes=VMEM_LIMIT),
        name="xattn_prompt",
    )(x, g, qk, vo)


def _split_head_dim(a):
    lead = a.shape[:-2]
    a = a.reshape(*lead, XATTN_HEADS, 2, XATTN_HEAD_DIM // 2)
    return jnp.swapaxes(a, -3, -2).reshape(*lead, 2 * XATTN_HEADS, XATTN_HEAD_DIM // 2)


def _merge_head_dim(a):
    lead = a.shape[:-2]
    a = a.reshape(*lead, 2, XATTN_HEADS, XATTN_HEAD_DIM // 2)
    return jnp.swapaxes(a, -3, -2).reshape(*lead, XATTN_HEADS, XATTN_HEAD_DIM)


def _attend_rows(q_ref, k_ref, v_ref, o_ref, rows):
    for r in rows:
        part = jnp.sum(k_ref[r] * q_ref[r][None], axis=-1, keepdims=True)
        s = part + pltpu.roll(part, XATTN_HEADS, axis=1)
        e = jnp.exp(s - jnp.max(s, axis=0, keepdims=True))
        o_ref[r] = jnp.sum(e * v_ref[r], axis=0) * (1.0 / jnp.sum(e, axis=0))


def _ffn_rows(x, g_ref, w1_ref, w2_ref, gf_ref, *, final_norm, between_chunks=None):
    h = _rms(x, g_ref[...]).astype(BF16)
    y = x
    for i, c in enumerate(range(0, D_FF, FF_CHUNK)):
        if between_chunks is not None:
            between_chunks(i)
        a = jnp.maximum(_dot(h, w1_ref[:, c:c + FF_CHUNK]), 0.0)
        y = y + _dot((a * a).astype(BF16), w2_ref[c:c + FF_CHUNK, :])
    return _rms(y, gf_ref[...]) if final_norm else y


def _ffn_kernel(x_ref, g_ref, w1_ref, w2_ref, gf_ref, o_ref, *, final_norm, between_chunks=None):
    o_ref[...] = _ffn_rows(x_ref[...], g_ref, w1_ref, w2_ref, gf_ref, final_norm=final_norm,
                           between_chunks=between_chunks)


def _oproj_ffn_kernel(x_ref, a_ref, wo_ref, g_ref, w1_ref, w2_ref, gf_ref, o_ref, h_buf, *, final_norm, n_chunks):
    c = pl.program_id(0)

    @pl.when(c == 0)
    def _():
        x = x_ref[...] + _dot(a_ref[...].astype(BF16), wo_ref[...])
        o_ref[...] = x
        h_buf[...] = _rms(x, g_ref[...]).astype(BF16)

    a = jnp.maximum(_dot(h_buf[...], w1_ref[...]), 0.0)
    o_ref[...] += _dot((a * a).astype(BF16), w2_ref[...])
    if final_norm:
        @pl.when(c == n_chunks - 1)
        def _():
            o_ref[...] = _rms(o_ref[...], gf_ref[...])


def _oproj_ffn(x, attn, layer, lw, norm_final, *, final_norm, chunk=1024):
    n_chunks = D_FF // chunk
    g, gf = lw['norm_ffn'].reshape(1, -1), norm_final.reshape(1, -1)
    return pl.pallas_call(
        functools.partial(_oproj_ffn_kernel, final_norm=final_norm, n_chunks=n_chunks),
        grid=(n_chunks,),
        in_specs=[_const_spec(x.shape), _const_spec(attn.shape), _layer_spec(lw['w_xo_layers'], layer),
                  _const_spec(g.shape), pl.BlockSpec((D_MODEL, chunk), lambda c: (0, c)),
                  pl.BlockSpec((chunk, D_MODEL), lambda c: (c, 0)), _const_spec(gf.shape)],
        out_specs=pl.BlockSpec(x.shape, lambda c: (0, 0)),
        out_shape=jax.ShapeDtypeStruct(x.shape, F32),
        scratch_shapes=[pltpu.VMEM(x.shape, BF16)],
        compiler_params=pltpu.CompilerParams(dimension_semantics=("arbitrary",), vmem_limit_bytes=VMEM_LIMIT),
        name="oproj_ffn",
    )(x, attn, lw['w_xo_layers'], g, lw['w_ff1'], lw['w_ff2'], gf)


FFN_ATTEND_INPUTS = 8
FFN_ATTEND_OUTPUTS = 2


def _ffn_attend_kernel(*refs, final_norm, rows, n_cast):
    x_ref, g_ref, w1_ref, w2_ref, gf_ref, q_ref, k_ref, v_ref = refs[:FFN_ATTEND_INPUTS]
    cast_src = refs[FFN_ATTEND_INPUTS:FFN_ATTEND_INPUTS + n_cast]
    o_ref, oa_ref = refs[FFN_ATTEND_INPUTS + n_cast:FFN_ATTEND_INPUTS + n_cast + FFN_ATTEND_OUTPUTS]
    cast_dst = refs[FFN_ATTEND_INPUTS + n_cast + FFN_ATTEND_OUTPUTS:]
    _run_cast_jobs(cast_src, cast_dst)
    n_chunks = D_FF // FF_CHUNK
    assert rows % n_chunks == 0
    per_chunk = rows // n_chunks
    attend = lambda i: _attend_rows(q_ref, k_ref, v_ref, oa_ref, range(i * per_chunk, (i + 1) * per_chunk))
    _ffn_kernel(x_ref, g_ref, w1_ref, w2_ref, gf_ref, o_ref, final_norm=final_norm, between_chunks=attend)


def _ffn_attend(x2d, lw, norm_final, attend, *, final_norm, tm=512, cast=()):
    n = x2d.shape[0]
    g = lw['norm_ffn'].reshape(1, -1)
    gf = norm_final.reshape(1, -1)
    in_specs = [pl.BlockSpec((tm, D_MODEL), lambda i: (i, 0)), _const_spec(g.shape),
                _const_spec(lw['w_ff1'].shape), _const_spec(lw['w_ff2'].shape), _const_spec(gf.shape)]
    out_spec = pl.BlockSpec((tm, D_MODEL), lambda i: (i, 0))
    out_shape = jax.ShapeDtypeStruct(x2d.shape, F32)
    params = pltpu.CompilerParams(dimension_semantics=("arbitrary",), vmem_limit_bytes=VMEM_LIMIT)
    q, cache_k, cache_v, layer = attend
    rows = q.shape[0] // (n // tm)
    assert rows * (n // tm) == q.shape[0]
    q_spec = pl.BlockSpec((rows,) + q.shape[1:], lambda i: (i, 0, 0))
    kv_spec = pl.BlockSpec((None, rows) + cache_k.shape[2:], lambda i: (layer, i, 0, 0, 0))
    jobs = [_cast_job(w, wl, n // tm, lambda i: i) for w, wl in cast]
    outs = pl.pallas_call(
        functools.partial(_ffn_attend_kernel, final_norm=final_norm, rows=rows, n_cast=len(jobs)), grid=(n // tm,),
        in_specs=in_specs + [q_spec, kv_spec, kv_spec] + [j[0] for j in jobs],
        out_specs=[out_spec, q_spec] + [j[1] for j in jobs],
        out_shape=[out_shape, jax.ShapeDtypeStruct(q.shape, F32)] + [j[2] for j in jobs],
        compiler_params=params, name="ffn_attend",
    )(x2d, g, lw['w_ff1'], lw['w_ff2'], gf, q, cache_k, cache_v, *[w for w, _ in cast])
    return outs[0], outs[1], list(outs[FFN_ATTEND_OUTPUTS:])


PROJ_WEIGHTS = ('w_in', 'w_out')


def _layer_weights(l, p, proj_bf16):
    tril = jnp.tril(jnp.ones((CHUNK, CHUNK), dtype=bool))
    ws = jnp.where(tril[None], p['gmlp_ws'][l], 0.0)
    pool_bd = jax.scipy.linalg.block_diag(*[p['pool_w'][l, g] for g in range(len(POOL_WINDOWS))])
    lw = {k: p[k][l] for k in ('norm_mix', 'gmlp_ln_g', 'gmlp_ln_b', 'conf_dw', 'conf_dw_b', 'conf_ln_g',
                               'conf_ln_b', 'sc_dw', 'pool_scale', 'mix_out_g', 'norm_xattn', 'norm_ffn')}
    lw.update(proj_bf16)
    lw.update(
        gmlp_wcat=jnp.concatenate([ws[h] for h in range(GMLP_HEADS)], axis=1).astype(BF16),
        gmlp_bias_tile=jnp.repeat(p['gmlp_bs'][l].T, GMLP_HEAD_DIM, axis=1),
        gmlp_w00=jnp.repeat(ws[:, 0, 0], GMLP_HEAD_DIM), gmlp_b0=jnp.repeat(p['gmlp_bs'][l][:, 0], GMLP_HEAD_DIM),
        pool_w_bd=pool_bd.astype(BF16))
    return lw


def kernel(x_prompt, x_sample, mem_prompt, cache_mem_k, cache_mem_v, state_conv_glu, state_conv_short, state_pool, norm_mix, w_in, gmlp_ln_g, gmlp_ln_b, gmlp_ws, gmlp_bs, conf_dw, conf_dw_b, conf_ln_g, conf_ln_b, sc_dw, pool_w, pool_scale, mix_out_g, w_out, norm_xattn, norm_mem, w_xq, w_xk, w_xv, w_xo, norm_ffn, w_ff1, w_ff2, norm_final):
    params = dict(norm_mix=norm_mix, w_in=w_in, gmlp_ln_g=gmlp_ln_g, gmlp_ln_b=gmlp_ln_b, gmlp_ws=gmlp_ws,
                  gmlp_bs=gmlp_bs, conf_dw=conf_dw, conf_dw_b=conf_dw_b, conf_ln_g=conf_ln_g, conf_ln_b=conf_ln_b,
                  sc_dw=sc_dw, pool_w=pool_w, pool_scale=pool_scale, mix_out_g=mix_out_g, w_out=w_out,
                  norm_xattn=norm_xattn, norm_mem=norm_mem, w_xq=w_xq, w_xk=w_xk, w_xv=w_xv, w_xo=w_xo,
                  norm_ffn=norm_ffn, w_ff1=w_ff1, w_ff2=w_ff2)
    depth = w_in.shape[0]
    bsz, seq, _ = x_prompt.shape
    n_s = x_sample.shape[0]
    xp = x_prompt
    xs = x_sample.reshape(n_s, D_MODEL)
    outs = {k: [] for k in ('glu_p', 'sh_p', 'pl_p', 'v_s')}
    hist_s = ()
    mem_k, mem_v, mem_qk, mem_vo, wq_bf16, wo_bf16, proj = _mem_kv(
        mem_prompt, norm_mem, w_xk, w_xv, w_xq, w_xo, cast=[(params[k], 0) for k in PROJ_WEIGHTS])
    cache_k_split = _split_head_dim(cache_mem_k)
    cache_v_split = _split_head_dim(cache_mem_v)
    for l in range(depth):
        lw = _layer_weights(l, params, dict(zip(PROJ_WEIGHTS, proj), w_xq_layers=wq_bf16, w_xo_layers=wo_bf16))
        last = l == depth - 1
        xs, q_s, vn_s, *hist_s = _mixer_sample(xs, state_conv_glu, state_conv_short, state_pool, l, lw, hist_s)
        q_s = _split_head_dim(q_s.reshape(n_s, XATTN_HEADS, XATTN_HEAD_DIM))
        xp, glu_p, sh_p, pool_p, (lw['w_ff1'], lw['w_ff2']) = _mixer_prompt(
            xp, lw, cast=[(params['w_ff1'], l), (params['w_ff2'], l)])
        xp = _xattn_prompt(xp, mem_qk, mem_vo, l, lw)
        outs['glu_p'].append(glu_p); outs['sh_p'].append(sh_p); outs['pl_p'].append(pool_p)
        xp, o_s, proj = _ffn_attend(xp.reshape(bsz * seq, D_MODEL), lw, norm_final,
                                    (q_s, cache_k_split, cache_v_split, l), final_norm=last,
                                    cast=[] if last else [(params[k], l + 1) for k in PROJ_WEIGHTS])
        xp = xp.reshape(bsz, seq, D_MODEL)
        xs = _oproj_ffn(xs, _merge_head_dim(o_s).reshape(n_s, D_MODEL), l, lw, norm_final, final_norm=last)
        outs['v_s'].append(vn_s.reshape(n_s, 1, D_GROUP))
    st = lambda k: jnp.stack(outs[k], axis=0)
    glu_s, sh_s, pool_s = hist_s
    return (xp, xs.reshape(n_s, 1, D_MODEL), mem_k, mem_v, st('glu_p'), glu_s, st('sh_p'), sh_s,
            st('pl_p'), pool_s, st('v_s'))
```

```python
import functools

import jax
import jax.numpy as jnp
from jax import lax
from jax.experimental import pallas as pl
from jax.experimental.pallas import tpu as pltpu

F32 = jnp.float32
BF16 = jnp.bfloat16

D_MODEL = 1024
D_GROUP = 256
N_GROUPS = 4
D_IN_PROJ = 8 * D_GROUP
GMLP_HEADS = 4
GMLP_HEAD_DIM = D_GROUP // GMLP_HEADS
CHUNK = 128
CONF_WIDTH = 31
SC_WIDTH = 3
POOL_WINDOWS = (2, 4, 8, 16)
POOL_GROUP_DIM = D_GROUP // len(POOL_WINDOWS)
POOL_BUF = max(POOL_WINDOWS) - 1
MEM_LEN = 256
XATTN_HEADS = 4
XATTN_HEAD_DIM = D_MODEL // XATTN_HEADS
D_FF = 4 * D_MODEL
PAST_LEN = 16384
EPS = 1e-6

GLU_HALO = 32
SC_HALO = 8
POOL_HALO = 32
LANES = 128
SUBLANES = 8
BF16_SUBLANES = 16
LANE_SLABS = D_GROUP // LANES
CONV_STRIDE = 4
CONV_ROWS = 256
FF_CHUNK = 4096
VMEM_LIMIT = 56 * 1024 * 1024


def _rms(x, g):
    return x * lax.rsqrt(jnp.mean(x * x, axis=-1, keepdims=True) + EPS) * g


def _layer_norm(x, g, b):
    xc = x - jnp.mean(x, axis=-1, keepdims=True)
    return xc * lax.rsqrt(jnp.mean(xc * xc, axis=-1, keepdims=True) + EPS) * g + b


def _sigmoid(x):
    return 0.5 * jnp.tanh(0.5 * x) + 0.5


def _dot(a, b):
    return jnp.dot(a, b, preferred_element_type=F32)


def _const_spec(shape):
    zeros = (0,) * len(shape)
    return pl.BlockSpec(shape, lambda *_: zeros, pipeline_mode=pl.Buffered(1))


def _layer_spec(stacked, layer):
    zeros = (0,) * (stacked.ndim - 1)
    return pl.BlockSpec((None,) + stacked.shape[1:], lambda *_: (layer,) + zeros, pipeline_mode=pl.Buffered(1))


def _cast_job(stacked, layer, steps, step_of):
    _, r, c = stacked.shape
    blk = r // steps
    assert blk * steps == r and blk % BF16_SUBLANES == 0
    return (pl.BlockSpec((None, blk, c), lambda *ids: (layer, step_of(*ids), 0)),
            pl.BlockSpec((blk, c), lambda *ids: (step_of(*ids), 0)),
            jax.ShapeDtypeStruct((r, c), BF16))


def _run_cast_jobs(src_refs, dst_refs):
    for src, dst in zip(src_refs, dst_refs):
        dst[...] = src[...].astype(BF16)


def _pool_window_lanes(shape):
    grp = lax.broadcasted_iota(jnp.int32, shape, len(shape) - 1) // POOL_GROUP_DIM
    win = jnp.full(shape, float(POOL_WINDOWS[-1]), F32)
    for g in range(len(POOL_WINDOWS) - 2, -1, -1):
        win = jnp.where(grp == g, float(POOL_WINDOWS[g]), win)
    return grp, win


def _select_by_group(grp, vals):
    out = vals[-1]
    for g in range(len(vals) - 2, -1, -1):
        out = jnp.where(grp == g, vals[g], out)
    return out


def _mix_out(x, y_groups, mog_ref, wout_ref):
    yn = [_rms(y, mog_ref[:, g * D_GROUP:(g + 1) * D_GROUP]).astype(BF16) for g, y in enumerate(y_groups)]
    return x + _dot(jnp.concatenate(yn, axis=1), wout_ref[...])


MIXER_INPUTS = 16
MIXER_OUTPUTS = 4


def _mixer_prompt_kernel(*refs, tt, n_t, n_cast):
    (x_ref, g_ref, win_ref, lng_ref, lnb_ref, wcat_ref, bst_ref, cw_ref, cb_ref, clg_ref,
     clb_ref, sw_ref, pw_ref, ps_ref, mog_ref, wout_ref) = refs[:MIXER_INPUTS]
    cast_src = refs[MIXER_INPUTS:MIXER_INPUTS + n_cast]
    outs = refs[MIXER_INPUTS + n_cast:]
    o_ref, oglu_ref, osh_ref, opool_ref = outs[:MIXER_OUTPUTS]
    cast_dst = outs[MIXER_OUTPUTS:MIXER_OUTPUTS + n_cast]
    glu_buf0, glu_buf1, yb_buf0, yb_buf1, sc_buf, p_buf, s2_buf, s4_buf, s8_buf = outs[MIXER_OUTPUTS + n_cast:]
    _run_cast_jobs(cast_src, cast_dst)
    t = pl.program_id(1)
    glu_buf = (glu_buf0, glu_buf1)
    yb_buf = (yb_buf0, yb_buf1)

    @pl.when(t == 0)
    def _():
        for s in range(LANE_SLABS):
            glu_buf[s][0:GLU_HALO, :] = jnp.zeros((GLU_HALO, LANES), F32)
        sc_buf[0:SC_HALO, :] = jnp.zeros((SC_HALO, D_GROUP), F32)
        p_buf[0:POOL_HALO, :] = jnp.zeros((POOL_HALO, D_GROUP), F32)

    _mix_tile(t, x_ref, g_ref, win_ref, lng_ref, lnb_ref, wcat_ref, bst_ref, cw_ref, cb_ref, clg_ref,
              clb_ref, sw_ref, pw_ref, ps_ref, mog_ref, wout_ref, o_ref,
              glu_buf, yb_buf, sc_buf, p_buf, s2_buf, s4_buf, s8_buf, tt=tt)

    @pl.when(t == n_t - 1)
    def _():
        oglu_ref[0] = jnp.concatenate(
            [glu_buf[s][GLU_HALO + tt - (CONF_WIDTH - 1):GLU_HALO + tt, :] for s in range(LANE_SLABS)], axis=1)
        osh_ref[0] = sc_buf[SC_HALO + tt - (SC_WIDTH - 1):SC_HALO + tt, :]
        opool_ref[0] = p_buf[POOL_HALO + tt - POOL_BUF:POOL_HALO + tt, :]

    for s in range(LANE_SLABS):
        glu_buf[s][0:GLU_HALO, :] = glu_buf[s][tt:tt + GLU_HALO, :]
    sc_buf[0:SC_HALO, :] = sc_buf[tt:tt + SC_HALO, :]
    p_buf[0:POOL_HALO, :] = p_buf[tt:tt + POOL_HALO, :]


def _mix_tile(t, x_ref, g_ref, win_ref, lng_ref, lnb_ref, wcat_ref, bst_ref, cw_ref, cb_ref, clg_ref,
              clb_ref, sw_ref, pw_ref, ps_ref, mog_ref, wout_ref, o_ref,
              glu_buf, yb_buf, sc_buf, p_buf, s2_buf, s4_buf, s8_buf, *, tt):
    x = x_ref[0]
    z = _dot(_rms(x, g_ref[...]).astype(BF16), win_ref[...])
    zcol = lambda i: z[:, i * D_GROUP:(i + 1) * D_GROUP]
    group_norm = lambda g, y: _rms(y, mog_ref[:, g * D_GROUP:(g + 1) * D_GROUP]).astype(BF16)
    yn = [None] * N_GROUPS

    vn = _layer_norm(zcol(1), lng_ref[...], lnb_ref[...])
    head = lax.broadcasted_iota(jnp.int32, (CHUNK, D_GROUP), 1) // GMLP_HEAD_DIM
    gates = []
    for c in range(tt // CHUNK):
        vc = vn[c * CHUNK:(c + 1) * CHUNK]
        stack = jnp.concatenate([jnp.where(head == hh, vc, 0.0) for hh in range(GMLP_HEADS)], axis=0)
        gates.append(_dot(wcat_ref[...], stack.astype(BF16)) + bst_ref[...])
    yn[0] = group_norm(0, zcol(0) * jnp.concatenate(gates, axis=0))

    first = GLU_HALO - (CONF_WIDTH - 1)
    span = SUBLANES * CONV_STRIDE
    glu = zcol(2) * _sigmoid(zcol(3))
    for s in range(LANE_SLABS):
        glu_buf[s][GLU_HALO:GLU_HALO + tt, :] = glu[:, s * LANES:(s + 1) * LANES]
    for c0 in range(0, tt, CONV_ROWS):
        starts = [c0 + (i // CONV_STRIDE) * span + i % CONV_STRIDE for i in range(CONV_ROWS // SUBLANES)]
        accs = [[None] * len(starts) for _ in range(LANE_SLABS)]
        for k in range(CONF_WIDTH):
            for s in range(LANE_SLABS):
                wk = jnp.broadcast_to(cw_ref[k:k + 1, s * LANES:(s + 1) * LANES], (SUBLANES, LANES))
                for i, t0 in enumerate(starts):
                    term = wk * glu_buf[s][pl.ds(first + t0 + k, SUBLANES, stride=CONV_STRIDE), :]
                    accs[s][i] = term if k == 0 else accs[s][i] + term
        conv = jnp.concatenate([jnp.concatenate(a, axis=0) for a in accs], axis=1)
        ln = _layer_norm(conv + cb_ref[...], clg_ref[...], clb_ref[...])
        yb = ln * _sigmoid(ln)
        for s in range(LANE_SLABS):
            for i, t0 in enumerate(starts):
                yb_buf[s][pl.ds(t0, SUBLANES, stride=CONV_STRIDE), :] = (
                    yb[i * SUBLANES:(i + 1) * SUBLANES, s * LANES:(s + 1) * LANES])
    yn[1] = group_norm(1, jnp.concatenate([yb_buf[s][...] for s in range(LANE_SLABS)], axis=1))

    sxc = zcol(5) * zcol(6)
    sc_buf[SC_HALO:SC_HALO + tt, :] = sxc
    conv_c = (sw_ref[0:1, :] * sc_buf[SC_HALO - 2:SC_HALO - 2 + tt, :]
              + sw_ref[1:2, :] * sc_buf[SC_HALO - 1:SC_HALO - 1 + tt, :]
              + sw_ref[2:3, :] * sxc)
    yn[2] = group_norm(2, zcol(4) * conv_c)

    pool_x = zcol(7)
    lo, end = POOL_HALO, POOL_HALO + tt
    p_buf[lo:end, :] = pool_x
    lo2, lo4, lo8 = lo - 3 * SUBLANES, lo - 2 * SUBLANES, lo - SUBLANES
    s2_buf[lo2:end, :] = p_buf[lo2:end, :] + p_buf[lo2 - 1:end - 1, :]
    s4_buf[lo4:end, :] = s2_buf[lo4:end, :] + s2_buf[lo4 - 2:end - 2, :]
    s8_buf[lo8:end, :] = s4_buf[lo8:end, :] + s4_buf[lo8 - 4:end - 4, :]
    s16 = s8_buf[lo:end, :] + s8_buf[lo - 8:end - 8, :]
    grp = _pool_window_lanes((tt, D_GROUP))[0]
    ssum = _select_by_group(grp, [s2_buf[lo:end, :], s4_buf[lo:end, :], s8_buf[lo:end, :], s16])
    head_rows = POOL_BUF + 1
    pos = t * tt + lax.broadcasted_iota(jnp.int32, (head_rows, D_GROUP), 0)
    cnt = jnp.minimum(_pool_window_lanes((head_rows, D_GROUP))[1], (pos + 1).astype(F32))
    inv_win = _select_by_group(_pool_window_lanes((tt - head_rows, D_GROUP))[0], [1.0 / w for w in POOL_WINDOWS])
    mean = jnp.concatenate([ssum[:head_rows] / cnt, ssum[head_rows:] * inv_win], axis=0)
    pooled = mean - pool_x
    yn[3] = group_norm(3, _dot(pooled.astype(BF16), pw_ref[...]) * ps_ref[...])

    o_ref[0] = x + _dot(jnp.concatenate(yn, axis=1), wout_ref[...])


def _mixer_prompt(x, lw, *, tt=1024, cast=()):
    bsz, seq, _ = x.shape
    n_t = seq // tt
    step_of = lambda b, t: b * n_t + t
    row = lambda a: a.reshape(1, -1)
    small = [row(lw['norm_mix']), lw['w_in'], row(lw['gmlp_ln_g']), row(lw['gmlp_ln_b']), lw['gmlp_wcat'],
             lw['gmlp_bias_tile'], lw['conf_dw'], row(lw['conf_dw_b']), row(lw['conf_ln_g']), row(lw['conf_ln_b']),
             lw['sc_dw'], lw['pool_w_bd'], row(lw['pool_scale']), row(lw['mix_out_g']), lw['w_out']]
    assert 1 + len(small) == MIXER_INPUTS
    jobs = [_cast_job(w, layer, bsz * n_t, step_of) for w, layer in cast]
    state_spec = lambda rows: pl.BlockSpec((1, rows, D_GROUP), lambda b, t: (b, 0, 0))
    outs = pl.pallas_call(
        functools.partial(_mixer_prompt_kernel, tt=tt, n_t=n_t, n_cast=len(jobs)),
        grid=(bsz, n_t),
        in_specs=[pl.BlockSpec((1, tt, D_MODEL), lambda b, t: (b, t, 0))] + [_const_spec(a.shape) for a in small]
        + [j[0] for j in jobs],
        out_specs=[pl.BlockSpec((1, tt, D_MODEL), lambda b, t: (b, t, 0)),
                   state_spec(CONF_WIDTH - 1), state_spec(SC_WIDTH - 1), state_spec(POOL_BUF)] + [j[1] for j in jobs],
        out_shape=[jax.ShapeDtypeStruct(x.shape, F32),
                   jax.ShapeDtypeStruct((bsz, CONF_WIDTH - 1, D_GROUP), F32),
                   jax.ShapeDtypeStruct((bsz, SC_WIDTH - 1, D_GROUP), F32),
                   jax.ShapeDtypeStruct((bsz, POOL_BUF, D_GROUP), F32)] + [j[2] for j in jobs],
        scratch_shapes=[pltpu.VMEM((GLU_HALO + tt, LANES), F32)] * LANE_SLABS + [pltpu.VMEM((tt, LANES), F32)] * LANE_SLABS
        + [pltpu.VMEM((SC_HALO + tt, D_GROUP), F32)]
        + [pltpu.VMEM((POOL_HALO + tt, D_GROUP), F32)] * 4,
        compiler_params=pltpu.CompilerParams(dimension_semantics=("arbitrary", "arbitrary"),
                                             vmem_limit_bytes=VMEM_LIMIT),
        name="mixer_prompt",
    )(x, *small, *[w for w, _ in cast])
    return (*outs[:MIXER_OUTPUTS], list(outs[MIXER_OUTPUTS:]))


SAMPLE_MIXER_INPUTS = 21


def _mixer_sample_kernel(*refs, layer):
    (x_ref, stg_ref, sts_ref, stp_ref, g_ref, win_ref, lng_ref, lnb_ref, w00_ref, b0_ref, cw_ref, cb_ref, clg_ref,
     clb_ref, sw_ref, pw_ref, ps_ref, mog_ref, wout_ref, gx_ref, wq_ref) = refs[:SAMPLE_MIXER_INPUTS]
    n_earlier = 3 if layer else 0
    earlier = refs[SAMPLE_MIXER_INPUTS:SAMPLE_MIXER_INPUTS + n_earlier]
    o_ref, oq_ref, ovn_ref, *new_hist = refs[SAMPLE_MIXER_INPUTS + n_earlier:]
    for src, dst in zip(earlier, new_hist):
        dst[0:layer] = src[...]
    oglu_ref, osh_ref, opool_ref = [ref.at[layer] for ref in new_hist]
    x = x_ref[...]
    h = _rms(x, g_ref[...]).astype(BF16)
    z = _dot(h, win_ref[...])
    u, v, glu_a, glu_g, sc_b, sc_c, sc_x, pool_x = [z[:, i * D_GROUP:(i + 1) * D_GROUP]
                                                    for i in range(D_IN_PROJ // D_GROUP)]
    hist = lambda ref, k: ref[:, k, :]

    def push(new_ref, old_ref, row):
        n_hist = old_ref.shape[1]
        new_ref[:, 0:n_hist - 1, :] = old_ref[:, 1:n_hist, :]
        new_ref[:, n_hist - 1, :] = row

    vn = _layer_norm(v, lng_ref[...], lnb_ref[...])
    ovn_ref[...] = vn
    y_a = u * (w00_ref[...] * vn + b0_ref[...])

    glu = glu_a * _sigmoid(glu_g)
    n_hist = CONF_WIDTH - 1
    acc = cw_ref[n_hist:n_hist + 1, :] * glu
    for k in range(n_hist):
        acc = acc + cw_ref[k:k + 1, :] * hist(stg_ref, k)
    ln = _layer_norm(acc + cb_ref[...], clg_ref[...], clb_ref[...])
    y_b = ln * _sigmoid(ln)
    push(oglu_ref, stg_ref, glu)

    sxc = sc_c * sc_x
    y_c = sc_b * (sw_ref[0:1, :] * hist(sts_ref, 0) + sw_ref[1:2, :] * hist(sts_ref, 1) + sw_ref[2:3, :] * sxc)
    push(osh_ref, sts_ref, sxc)

    run = pool_x
    sums = []
    back = 0
    for w in POOL_WINDOWS:
        while back < w - 1:
            run = run + hist(stp_ref, POOL_BUF - 1 - back)
            back += 1
        sums.append(run)
    grp, win = _pool_window_lanes(pool_x.shape)
    cnt = jnp.minimum(win, float(PAST_LEN + 1))
    pooled = _select_by_group(grp, sums) / cnt - pool_x
    y_d = _dot(pooled.astype(BF16), pw_ref[...]) * ps_ref[...]
    push(opool_ref, stp_ref, pool_x)

    x_mixed = _mix_out(x, [y_a, y_b, y_c, y_d], mog_ref, wout_ref)
    o_ref[...] = x_mixed
    oq_ref[...] = _dot(_rms(x_mixed, gx_ref[...]).astype(BF16), wq_ref[...]) * (XATTN_HEAD_DIM ** -0.5)


def _mixer_sample(xs, st_glu, st_sh, st_pool, layer, lw, earlier=()):
    n = xs.shape[0]
    row = lambda a: a.reshape(1, -1)
    states = [st_glu, st_sh, st_pool]
    earlier = list(earlier)
    assert len(earlier) == (len(states) if layer else 0)
    consts = [row(lw['norm_mix']), lw['w_in'], row(lw['gmlp_ln_g']), row(lw['gmlp_ln_b']), row(lw['gmlp_w00']),
              row(lw['gmlp_b0']), lw['conf_dw'], row(lw['conf_dw_b']), row(lw['conf_ln_g']), row(lw['conf_ln_b']),
              lw['sc_dw'], lw['pool_w_bd'], row(lw['pool_scale']), row(lw['mix_out_g']), lw['w_out'],
              row(lw['norm_xattn'])]
    assert 1 + len(states) + len(consts) + 1 == SAMPLE_MIXER_INPUTS
    out_shapes = [(n, D_MODEL), (n, D_MODEL), (n, D_GROUP)] + [(layer + 1,) + s.shape[1:] for s in states]
    return pl.pallas_call(
        functools.partial(_mixer_sample_kernel, layer=layer),
        grid=(1,),
        in_specs=[_const_spec(xs.shape)] + [_layer_spec(s, layer) for s in states]
        + [_const_spec(a.shape) for a in consts] + [_layer_spec(lw['w_xq_layers'], layer)]
        + [_const_spec(a.shape) for a in earlier],
        out_specs=[pl.BlockSpec(s, lambda i, nd=len(s): (0,) * nd) for s in out_shapes],
        out_shape=[jax.ShapeDtypeStruct(s, F32) for s in out_shapes],
        compiler_params=pltpu.CompilerParams(dimension_semantics=("arbitrary",), vmem_limit_bytes=VMEM_LIMIT),
        name="mixer_sample",
    )(xs, *states, *consts, lw['w_xq_layers'], *earlier)


MEM_KV_INPUTS = 6
MEM_KV_OUTPUTS = 6


def _mem_kv_kernel(*refs, n_cast):
    m_ref, g_ref, wk_ref, wv_ref, wq_ref, wo_ref = refs[:MEM_KV_INPUTS]
    cast_src = refs[MEM_KV_INPUTS:MEM_KV_INPUTS + n_cast]
    outs = refs[MEM_KV_INPUTS + n_cast:]
    ok_ref, ov_ref, oqk_ref, ovo_ref, wq_bf16, wo_bf16 = outs[:MEM_KV_OUTPUTS]
    cast_dst = outs[MEM_KV_OUTPUTS:MEM_KV_OUTPUTS + n_cast]
    wk_bf16, wv_bf16 = outs[MEM_KV_OUTPUTS + n_cast:]
    _run_cast_jobs(cast_src, cast_dst)

    @pl.when(pl.program_id(1) == 0)
    def _():
        for src, dst in ((wk_ref, wk_bf16), (wv_ref, wv_bf16), (wq_ref, wq_bf16), (wo_ref, wo_bf16)):
            dst[...] = src[...].astype(BF16)

    m = _rms(m_ref[0], g_ref[...]).astype(BF16)
    k = _dot(m, wk_bf16[...])
    v = _dot(m, wv_bf16[...])
    contract_last = (((1,), (1,)), ((), ()))
    for h in range(XATTN_HEADS):
        lo, hi = h * XATTN_HEAD_DIM, (h + 1) * XATTN_HEAD_DIM
        ok_ref[0, :, h, :] = k[:, lo:hi]
        ov_ref[0, :, h, :] = v[:, lo:hi]
        qk = lax.dot_general(wq_bf16[:, lo:hi], k[:, lo:hi].astype(BF16), contract_last,
                             preferred_element_type=F32)
        oqk_ref[0, :, h * MEM_LEN:(h + 1) * MEM_LEN] = (qk * (XATTN_HEAD_DIM ** -0.5)).astype(BF16)
        ovo_ref[0, h * MEM_LEN:(h + 1) * MEM_LEN, :] = _dot(v[:, lo:hi].astype(BF16), wo_bf16[lo:hi, :]).astype(BF16)


def _mem_kv(mem, norm_mem, w_xk, w_xv, w_xq, w_xo, *, cast=()):
    depth = w_xk.shape[0]
    bsz = mem.shape[0]
    g = norm_mem.reshape(depth, 1, D_MODEL)
    per_layer = lambda *shape, **kw: pl.BlockSpec((None,) + shape, lambda l, b: (l,) + (0,) * len(shape), **kw)
    out_blk = lambda *shape: pl.BlockSpec((None, 1) + shape, lambda l, b: (l, b) + (0,) * len(shape))
    weights = [w_xk, w_xv, w_xq, w_xo]
    jobs = [_cast_job(w, layer, depth * bsz, lambda l, b: l * bsz + b) for w, layer in cast]
    outs = pl.pallas_call(
        functools.partial(_mem_kv_kernel, n_cast=len(jobs)),
        grid=(depth, bsz),
        in_specs=[pl.BlockSpec((1, MEM_LEN, D_MODEL), lambda l, b: (b, 0, 0)), per_layer(1, D_MODEL)]
        + [per_layer(D_MODEL, D_MODEL, pipeline_mode=pl.Buffered(1))] * len(weights) + [j[0] for j in jobs],
        out_specs=[out_blk(MEM_LEN, XATTN_HEADS, XATTN_HEAD_DIM), out_blk(MEM_LEN, XATTN_HEADS, XATTN_HEAD_DIM),
                   out_blk(D_MODEL, XATTN_HEADS * MEM_LEN), out_blk(XATTN_HEADS * MEM_LEN, D_MODEL),
                   per_layer(D_MODEL, D_MODEL), per_layer(D_MODEL, D_MODEL)] + [j[1] for j in jobs],
        out_shape=[jax.ShapeDtypeStruct((depth, bsz, MEM_LEN, XATTN_HEADS, XATTN_HEAD_DIM), F32)] * 2
        + [jax.ShapeDtypeStruct((depth, bsz, D_MODEL, XATTN_HEADS * MEM_LEN), BF16),
           jax.ShapeDtypeStruct((depth, bsz, XATTN_HEADS * MEM_LEN, D_MODEL), BF16)]
        + [jax.ShapeDtypeStruct((depth, D_MODEL, D_MODEL), BF16)] * 2 + [j[2] for j in jobs],
        scratch_shapes=[pltpu.VMEM((D_MODEL, D_MODEL), BF16)] * 2,
        compiler_params=pltpu.CompilerParams(dimension_semantics=("arbitrary", "arbitrary"),
                                             vmem_limit_bytes=VMEM_LIMIT),
        name="mem_kv",
    )(mem, g, *weights, *[w for w, _ in cast])
    return (*outs[:MEM_KV_OUTPUTS], list(outs[MEM_KV_OUTPUTS:]))


def _xattn_prompt_kernel(x_ref, g_ref, qk_ref, vo_ref, o_ref):
    x = x_ref[0]
    h = _rms(x, g_ref[...]).astype(BF16)
    s = _dot(h, qk_ref[0])
    probs = []
    for hh in range(XATTN_HEADS):
        sh = s[:, hh * MEM_LEN:(hh + 1) * MEM_LEN]
        e = jnp.exp(sh - jnp.max(sh, axis=-1, keepdims=True))
        probs.append((e * (1.0 / jnp.sum(e, axis=-1, keepdims=True))).astype(BF16))
    o_ref[0] = x + _dot(jnp.concatenate(probs, axis=1), vo_ref[0])


def _xattn_prompt(x, qk, vo, layer, lw, *, tt=1024):
    bsz, seq, _ = x.shape
    g = lw['norm_xattn'].reshape(1, -1)
    folded = lambda a: pl.BlockSpec((None, 1) + a.shape[2:], lambda b, t: (layer, b, 0, 0))
    return pl.pallas_call(
        _xattn_prompt_kernel,
        grid=(bsz, seq // tt),
        in_specs=[pl.BlockSpec((1, tt, D_MODEL), lambda b, t: (b, t, 0)), _const_spec(g.shape),
                  folded(qk), folded(vo)],
        out_specs=pl.BlockSpec((1, tt, D_MODEL), lambda b, t: (b, t, 0)),
        out_shape=jax.ShapeDtypeStruct(x.shape, F32),
        compiler_params=pltpu.CompilerParams(dimension_semantics=("arbitrary", "arbitrary"),
                                             vmem_limit_bytes=VMEM_LIMIT),
        name="xattn_prompt",
    )(x, g, qk, vo)


def _split_head_dim(a):
    lead = a.shape[:-2]
    a = a.reshape(*lead, XATTN_HEADS, 2, XATTN_HEAD_DIM // 2)
    return jnp.swapaxes(a, -3, -2).reshape(*lead, 2 * XATTN_HEADS, XATTN_HEAD_DIM // 2)


def _merge_head_dim(a):
    lead = a.shape[:-2]
    a = a.reshape(*lead, 2, XATTN_HEADS, XATTN_HEAD_DIM // 2)
    return jnp.swapaxes(a, -3, -2).reshape(*lead, XATTN_HEADS, XATTN_HEAD_DIM)


def _attend_rows(q_ref, k_ref, v_ref, o_ref, rows):
    for r in rows:
        part = jnp.sum(k_ref[r] * q_ref[r][None], axis=-1, keepdims=True)
        s = part + pltpu.roll(part, XATTN_HEADS, axis=1)
        e = jnp.exp(s - jnp.max(s, axis=0, keepdims=True))
        o_ref[r] = jnp.sum(e * v_ref[r], axis=0) * (1.0 / jnp.sum(e, axis=0))


def _ffn_rows(x, g_ref, w1_ref, w2_ref, gf_ref, *, final_norm, between_chunks=None):
    h = _rms(x, g_ref[...]).astype(BF16)
    y = x
    for i, c in enumerate(range(0, D_FF, FF_CHUNK)):
        if between_chunks is not None:
            between_chunks(i)
        a = jnp.maximum(_dot(h, w1_ref[:, c:c + FF_CHUNK]), 0.0)
        y = y + _dot((a * a).astype(BF16), w2_ref[c:c + FF_CHUNK, :])
    return _rms(y, gf_ref[...]) if final_norm else y


def _ffn_kernel(x_ref, g_ref, w1_ref, w2_ref, gf_ref, o_ref, *, final_norm, between_chunks=None):
    o_ref[...] = _ffn_rows(x_ref[...], g_ref, w1_ref, w2_ref, gf_ref, final_norm=final_norm,
                           between_chunks=between_chunks)


def _oproj_ffn_kernel(x_ref, a_ref, wo_ref, g_ref, w1_ref, w2_ref, gf_ref, o_ref, h_buf, *, final_norm, n_chunks):
    c = pl.program_id(0)

    @pl.when(c == 0)
    def _():
        x = x_ref[...] + _dot(a_ref[...].astype(BF16), wo_ref[...])
        o_ref[...] = x
        h_buf[...] = _rms(x, g_ref[...]).astype(BF16)

    a = jnp.maximum(_dot(h_buf[...], w1_ref[...]), 0.0)
    o_ref[...] += _dot((a * a).astype(BF16), w2_ref[...])
    if final_norm:
        @pl.when(c == n_chunks - 1)
        def _():
            o_ref[...] = _rms(o_ref[...], gf_ref[...])


def _oproj_ffn(x, attn, layer, lw, norm_final, *, final_norm, chunk=1024):
    n_chunks = D_FF // chunk
    g, gf = lw['norm_ffn'].reshape(1, -1), norm_final.reshape(1, -1)
    return pl.pallas_call(
        functools.partial(_oproj_ffn_kernel, final_norm=final_norm, n_chunks=n_chunks),
        grid=(n_chunks,),
        in_specs=[_const_spec(x.shape), _const_spec(attn.shape), _layer_spec(lw['w_xo_layers'], layer),
                  _const_spec(g.shape), pl.BlockSpec((D_MODEL, chunk), lambda c: (0, c)),
                  pl.BlockSpec((chunk, D_MODEL), lambda c: (c, 0)), _const_spec(gf.shape)],
        out_specs=pl.BlockSpec(x.shape, lambda c: (0, 0)),
        out_shape=jax.ShapeDtypeStruct(x.shape, F32),
        scratch_shapes=[pltpu.VMEM(x.shape, BF16)],
        compiler_params=pltpu.CompilerParams(dimension_semantics=("arbitrary",), vmem_limit_bytes=VMEM_LIMIT),
        name="oproj_ffn",
    )(x, attn, lw['w_xo_layers'], g, lw['w_ff1'], lw['w_ff2'], gf)


FFN_ATTEND_INPUTS = 8
FFN_ATTEND_OUTPUTS = 2


def _ffn_attend_kernel(*refs, final_norm, rows, n_cast):
    x_ref, g_ref, w1_ref, w2_ref, gf_ref, q_ref, k_ref, v_ref = refs[:FFN_ATTEND_INPUTS]
    cast_src = refs[FFN_ATTEND_INPUTS:FFN_ATTEND_INPUTS + n_cast]
    o_ref, oa_ref = refs[FFN_ATTEND_INPUTS + n_cast:FFN_ATTEND_INPUTS + n_cast + FFN_ATTEND_OUTPUTS]
    cast_dst = refs[FFN_ATTEND_INPUTS + n_cast + FFN_ATTEND_OUTPUTS:]
    _run_cast_jobs(cast_src, cast_dst)
    n_chunks = D_FF // FF_CHUNK
    assert rows % n_chunks == 0
    per_chunk = rows // n_chunks
    attend = lambda i: _attend_rows(q_ref, k_ref, v_ref, oa_ref, range(i * per_chunk, (i + 1) * per_chunk))
    _ffn_kernel(x_ref, g_ref, w1_ref, w2_ref, gf_ref, o_ref, final_norm=final_norm, between_chunks=attend)


def _ffn_attend(x2d, lw, norm_final, attend, *, final_norm, tm=512, cast=()):
    n = x2d.shape[0]
    g = lw['norm_ffn'].reshape(1, -1)
    gf = norm_final.reshape(1, -1)
    in_specs = [pl.BlockSpec((tm, D_MODEL), lambda i: (i, 0)), _const_spec(g.shape),
                _const_spec(lw['w_ff1'].shape), _const_spec(lw['w_ff2'].shape), _const_spec(gf.shape)]
    out_spec = pl.BlockSpec((tm, D_MODEL), lambda i: (i, 0))
    out_shape = jax.ShapeDtypeStruct(x2d.shape, F32)
    q, cache_k, cache_v, layer = attend
    fuse_q = [i == 5 for i in range(FFN_ATTEND_INPUTS + len(cast))]
    params = pltpu.CompilerParams(dimension_semantics=("arbitrary",), vmem_limit_bytes=VMEM_LIMIT,
                                  allow_input_fusion=fuse_q)
    rows = q.shape[0] // (n // tm)
    assert rows * (n // tm) == q.shape[0]
    q_spec = pl.BlockSpec((rows,) + q.shape[1:], lambda i: (i, 0, 0))
    kv_spec = pl.BlockSpec((None, rows) + cache_k.shape[2:], lambda i: (layer, i, 0, 0, 0))
    jobs = [_cast_job(w, wl, n // tm, lambda i: i) for w, wl in cast]
    outs = pl.pallas_call(
        functools.partial(_ffn_attend_kernel, final_norm=final_norm, rows=rows, n_cast=len(jobs)), grid=(n // tm,),
        in_specs=in_specs + [q_spec, kv_spec, kv_spec] + [j[0] for j in jobs],
        out_specs=[out_spec, q_spec] + [j[1] for j in jobs],
        out_shape=[out_shape, jax.ShapeDtypeStruct(q.shape, F32)] + [j[2] for j in jobs],
        compiler_params=params, name="ffn_attend",
    )(x2d, g, lw['w_ff1'], lw['w_ff2'], gf, q, cache_k, cache_v, *[w for w, _ in cast])
    return outs[0], outs[1], list(outs[FFN_ATTEND_OUTPUTS:])


PROJ_WEIGHTS = ('w_in', 'w_out')


def _layer_weights(l, p, proj_bf16):
    tril = jnp.tril(jnp.ones((CHUNK, CHUNK), dtype=bool))
    ws = jnp.where(tril[None], p['gmlp_ws'][l], 0.0)
    pool_bd = jax.scipy.linalg.block_diag(*[p['pool_w'][l, g] for g in range(len(POOL_WINDOWS))])
    lw = {k: p[k][l] for k in ('norm_mix', 'gmlp_ln_g', 'gmlp_ln_b', 'conf_dw', 'conf_dw_b', 'conf_ln_g',
                               'conf_ln_b', 'sc_dw', 'pool_scale', 'mix_out_g', 'norm_xattn', 'norm_ffn')}
    lw.update(proj_bf16)
    lw.update(
        gmlp_wcat=jnp.concatenate([ws[h] for h in range(GMLP_HEADS)], axis=1).astype(BF16),
        gmlp_bias_tile=jnp.repeat(p['gmlp_bs'][l].T, GMLP_HEAD_DIM, axis=1),
        gmlp_w00=jnp.repeat(ws[:, 0, 0], GMLP_HEAD_DIM), gmlp_b0=jnp.repeat(p['gmlp_bs'][l][:, 0], GMLP_HEAD_DIM),
        pool_w_bd=pool_bd.astype(BF16))
    return lw


def kernel(x_prompt, x_sample, mem_prompt, cache_mem_k, cache_mem_v, state_conv_glu, state_conv_short, state_pool, norm_mix, w_in, gmlp_ln_g, gmlp_ln_b, gmlp_ws, gmlp_bs, conf_dw, conf_dw_b, conf_ln_g, conf_ln_b, sc_dw, pool_w, pool_scale, mix_out_g, w_out, norm_xattn, norm_mem, w_xq, w_xk, w_xv, w_xo, norm_ffn, w_ff1, w_ff2, norm_final):
    params = dict(norm_mix=norm_mix, w_in=w_in, gmlp_ln_g=gmlp_ln_g, gmlp_ln_b=gmlp_ln_b, gmlp_ws=gmlp_ws,
                  gmlp_bs=gmlp_bs, conf_dw=conf_dw, conf_dw_b=conf_dw_b, conf_ln_g=conf_ln_g, conf_ln_b=conf_ln_b,
                  sc_dw=sc_dw, pool_w=pool_w, pool_scale=pool_scale, mix_out_g=mix_out_g, w_out=w_out,
                  norm_xattn=norm_xattn, norm_mem=norm_mem, w_xq=w_xq, w_xk=w_xk, w_xv=w_xv, w_xo=w_xo,
                  norm_ffn=norm_ffn, w_ff1=w_ff1, w_ff2=w_ff2)
    depth = w_in.shape[0]
    bsz, seq, _ = x_prompt.shape
    n_s = x_sample.shape[0]
    xp = x_prompt
    xs = x_sample.reshape(n_s, D_MODEL)
    outs = {k: [] for k in ('glu_p', 'sh_p', 'pl_p', 'v_s')}
    hist_s = ()
    mem_k, mem_v, mem_qk, mem_vo, wq_bf16, wo_bf16, proj = _mem_kv(
        mem_prompt, norm_mem, w_xk, w_xv, w_xq, w_xo, cast=[(params[k], 0) for k in PROJ_WEIGHTS])
    cache_k_split = _split_head_dim(cache_mem_k)
    cache_v_split = _split_head_dim(cache_mem_v)
    for l in range(depth):
        lw = _layer_weights(l, params, dict(zip(PROJ_WEIGHTS, proj), w_xq_layers=wq_bf16, w_xo_layers=wo_bf16))
        last = l == depth - 1
        xs, q_s, vn_s, *hist_s = _mixer_sample(xs, state_conv_glu, state_conv_short, state_pool, l, lw, hist_s)
        q_s = _split_head_dim(q_s.reshape(n_s, XATTN_HEADS, XATTN_HEAD_DIM))
        xp, glu_p, sh_p, pool_p, (lw['w_ff1'], lw['w_ff2']) = _mixer_prompt(
            xp, lw, cast=[(params['w_ff1'], l), (params['w_ff2'], l)])
        xp = _xattn_prompt(xp, mem_qk, mem_vo, l, lw)
        outs['glu_p'].append(glu_p); outs['sh_p'].append(sh_p); outs['pl_p'].append(pool_p)
        xp, o_s, proj = _ffn_attend(xp.reshape(bsz * seq, D_MODEL), lw, norm_final,
                                    (q_s, cache_k_split, cache_v_split, l), final_norm=last,
                                    cast=[] if last else [(params[k], l + 1) for k in PROJ_WEIGHTS])
        xp = xp.reshape(bsz, seq, D_MODEL)
        xs = _oproj_ffn(xs, _merge_head_dim(o_s).reshape(n_s, D_MODEL), l, lw, norm_final, final_norm=last)
        outs['v_s'].append(vn_s.reshape(n_s, 1, D_GROUP))
    st = lambda k: jnp.stack(outs[k], axis=0)
    glu_s, sh_s, pool_s = hist_s
    return (xp, xs.reshape(n_s, 1, D_MODEL), mem_k, mem_v, st('glu_p'), glu_s, st('sh_p'), sh_s,
            st('pl_p'), pool_s, st('v_s'))
```
